```python
import jax, jax.numpy as jnp
from jax import lax
import numpy as np

D_MODEL = 1024
BATCH = 8
SEQ = 2048
DEPTH = 4

N_HEADS = 8
HEAD_DIM = 64
ATTN_WIDTH = N_HEADS * HEAD_DIM
CONV_CH = 512
CONV_K = 31
D_FF = 2816
D_PLE = 256
Q_BLOCK = 128
EPS = 1e-6
FFN_RES = 0.5

Q0 = 0
K0 = Q0 + ATTN_WIDTH
V0 = K0 + ATTN_WIDTH
F0 = V0 + ATTN_WIDTH
C0 = F0 + N_HEADS
GA0 = C0 + 2 * CONV_CH
GC0 = GA0 + D_MODEL
IN_COLS = GC0 + D_MODEL

kernel_name = "macaron_fox_conformer_hybrid"


def rmsnorm(x, g):
    xf = x.astype(jnp.float32)
    y = xf * lax.rsqrt(jnp.mean(xf * xf, axis=-1, keepdims=True) + EPS)
    return (y * g.astype(jnp.float32)).astype(x.dtype)


def swiglu(x, w_in, w_out):
    a, b = jnp.split(x @ w_in, 2, axis=-1)
    return (jax.nn.silu(a) * b) @ w_out


def forgetting_attention(q, k, v, f_logit):
    b, s, h, dh = q.shape
    scale = 1.0 / float(np.sqrt(dh))
    c = jnp.cumsum(jax.nn.log_sigmoid(f_logit.astype(jnp.float32)), axis=1)
    c = jnp.transpose(c, (0, 2, 1))
    qh = jnp.transpose(q, (0, 2, 1, 3))
    kh = jnp.transpose(k, (0, 2, 1, 3))
    vh = jnp.transpose(v, (0, 2, 1, 3))
    outs = []
    for blk in range(s // Q_BLOCK):
        qs, qe = blk * Q_BLOCK, (blk + 1) * Q_BLOCK
        sc = jnp.einsum('bhqd,bhkd->bhqk', qh[:, :, qs:qe], kh[:, :, :qe]).astype(jnp.float32) * scale
        sc = sc + (c[:, :, qs:qe, None] - c[:, :, None, :qe])
        causal = jnp.arange(qs, qe)[:, None] >= jnp.arange(qe)[None, :]
        sc = jnp.where(causal[None, None], sc, -jnp.inf)
        pr = jax.nn.softmax(sc, axis=-1).astype(vh.dtype)
        outs.append(jnp.einsum('bhqk,bhkd->bhqd', pr, vh[:, :, :qe]))
    o = jnp.concatenate(outs, axis=2)
    return jnp.transpose(o, (0, 2, 1, 3)).reshape(b, s, h * dh)


def conformer_conv(glu_in, conv_w, conv_b, g_conv):
    a = glu_in[..., :CONV_CH] * jax.nn.sigmoid(glu_in[..., CONV_CH:])
    y = lax.conv_general_dilated(
        a, conv_w[:, None, :].astype(a.dtype), window_strides=(1,),
        padding=[(CONV_K - 1, 0)], dimension_numbers=('NWC', 'WIO', 'NWC'),
        feature_group_count=CONV_CH) + conv_b
    return jax.nn.silu(rmsnorm(y, g_conv))


def hybrid_mixer(u, w_in, b_f, w_attn_out, conv_w, conv_b, g_conv, w_conv_out, w_out):
    b, s, _ = u.shape
    z = u @ w_in
    q = z[..., Q0:K0].reshape(b, s, N_HEADS, HEAD_DIM)
    k = z[..., K0:V0].reshape(b, s, N_HEADS, HEAD_DIM)
    v = z[..., V0:F0].reshape(b, s, N_HEADS, HEAD_DIM)
    f_logit = z[..., F0:C0] + b_f
    y_attn = forgetting_attention(q, k, v, f_logit) @ w_attn_out
    y_conv = conformer_conv(z[..., C0:GA0], conv_w, conv_b, g_conv) @ w_conv_out
    merged = jax.nn.sigmoid(z[..., GA0:GC0]) * y_attn + jax.nn.sigmoid(z[..., GC0:]) * y_conv
    return merged @ w_out


def _fwd_setup_inputs(seed: int = 0) -> dict:
    key = jax.random.key(seed)
    ks = jax.random.split(key, 24)
    L, D, F = DEPTH, D_MODEL, D_FF
    f32 = jnp.float32

    def w(k, shape, fan_in):
        return jax.random.normal(k, shape, f32) * (fan_in ** -0.5)

    def gain(k, shape):
        return 1.0 + 0.05 * jax.random.normal(k, shape, f32)

    return {
        "x": jax.random.normal(ks[0], (BATCH, SEQ, D), f32),
        "p": jax.random.normal(ks[1], (DEPTH, BATCH, SEQ, D_PLE), f32),
        "g_ff1": gain(ks[2], (L, D)),
        "w_ff1_in": w(ks[3], (L, D, 2 * F), D),
        "w_ff1_out": w(ks[4], (L, F, D), F),
        "g_mix": gain(ks[5], (L, D)),
        "w_in": w(ks[6], (L, D, IN_COLS), D),
        "b_f": 2.0 + 0.5 * jax.random.normal(ks[7], (L, N_HEADS), f32),
        "w_attn_out": w(ks[8], (L, ATTN_WIDTH, D), ATTN_WIDTH),
        "conv_w": w(ks[9], (L, CONV_K, CONV_CH), CONV_K),
        "conv_b": 0.02 * jax.random.normal(ks[10], (L, CONV_CH), f32),
        "g_conv": gain(ks[11], (L, CONV_CH)),
        "w_conv_out": w(ks[12], (L, CONV_CH, D), CONV_CH),
        "w_out": w(ks[13], (L, D, D), D),
        "g_ff2": gain(ks[14], (L, D)),
        "w_ff2_in": w(ks[15], (L, D, 2 * F), D),
        "w_ff2_out": w(ks[16], (L, F, D), F),
        "g_ple": gain(ks[17], (L, D)),
        "w_ple_gate": w(ks[18], (L, D, D), D),
        "w_ple_proj": w(ks[19], (L, D_PLE, D), D_PLE),
        "g_final": gain(ks[20], (D,)),
    }


def _fwd_reference(x, p, g_ff1, w_ff1_in, w_ff1_out, g_mix, w_in, b_f, w_attn_out,
              conv_w, conv_b, g_conv, w_conv_out, w_out, g_ff2, w_ff2_in, w_ff2_out,
              g_ple, w_ple_gate, w_ple_proj, g_final):
    h = x
    for i in range(DEPTH):
        h = h + FFN_RES * swiglu(rmsnorm(h, g_ff1[i]), w_ff1_in[i], w_ff1_out[i])
        h = h + hybrid_mixer(rmsnorm(h, g_mix[i]), w_in[i], b_f[i], w_attn_out[i],
                             conv_w[i], conv_b[i], g_conv[i], w_conv_out[i], w_out[i])
        h = h + FFN_RES * swiglu(rmsnorm(h, g_ff2[i]), w_ff2_in[i], w_ff2_out[i])
        gate = jax.nn.sigmoid(rmsnorm(h, g_ple[i]) @ w_ple_gate[i])
        h = h + gate * (p[i] @ w_ple_proj[i])
    return rmsnorm(h, g_final)


import jax as _jax
import jax.numpy as _jnp

TWIN_FORMAT = 'train_step'
FWD_PARAMS = ['x', 'p', 'g_ff1', 'w_ff1_in', 'w_ff1_out', 'g_mix', 'w_in', 'b_f', 'w_attn_out', 'conv_w', 'conv_b', 'g_conv', 'w_conv_out', 'w_out', 'g_ff2', 'w_ff2_in', 'w_ff2_out', 'g_ple', 'w_ple_gate', 'w_ple_proj', 'g_final']
TWIN_WEIGHTS = ['g_ff1', 'w_ff1_in', 'w_ff1_out', 'g_mix', 'w_in', 'b_f', 'w_attn_out', 'conv_w', 'conv_b', 'g_conv', 'w_conv_out', 'w_out', 'g_ff2', 'w_ff2_in', 'w_ff2_out', 'g_ple', 'w_ple_gate', 'w_ple_proj', 'g_final']
TWIN_DIFF_INPUT = 'x'
TWIN_INPUTS = ['x', 'p', 'g_ff1', 'w_ff1_in', 'w_ff1_out', 'g_mix', 'w_in', 'b_f', 'w_attn_out', 'conv_w', 'conv_b', 'g_conv', 'w_conv_out', 'w_out', 'g_ff2', 'w_ff2_in', 'w_ff2_out', 'g_ple', 'w_ple_gate', 'w_ple_proj', 'g_final', 'loss_target', 'm_g_ff1', 'm_w_ff1_in', 'm_w_ff1_out', 'm_g_mix', 'm_w_in', 'm_b_f', 'm_w_attn_out', 'm_conv_w', 'm_conv_b', 'm_g_conv', 'm_w_conv_out', 'm_w_out', 'm_g_ff2', 'm_w_ff2_in', 'm_w_ff2_out', 'm_g_ple', 'm_w_ple_gate', 'm_w_ple_proj', 'm_g_final', 'v_g_ff1', 'v_w_ff1_in', 'v_w_ff1_out', 'v_g_mix', 'v_w_in', 'v_b_f', 'v_w_attn_out', 'v_conv_w', 'v_conv_b', 'v_g_conv', 'v_w_conv_out', 'v_w_out', 'v_g_ff2', 'v_w_ff2_in', 'v_w_ff2_out', 'v_g_ple', 'v_w_ple_gate', 'v_w_ple_proj', 'v_g_final']
TWIN_OUTPUTS = ['loss', 'grad_x', 'grad_g_ff1', 'grad_w_ff1_in', 'grad_w_ff1_out', 'grad_g_mix', 'grad_w_in', 'grad_b_f', 'grad_w_attn_out', 'grad_conv_w', 'grad_conv_b', 'grad_g_conv', 'grad_w_conv_out', 'grad_w_out', 'grad_g_ff2', 'grad_w_ff2_in', 'grad_w_ff2_out', 'grad_g_ple', 'grad_w_ple_gate', 'grad_w_ple_proj', 'grad_g_final', 'delta_g_ff1', 'delta_w_ff1_in', 'delta_w_ff1_out', 'delta_g_mix', 'delta_w_in', 'delta_b_f', 'delta_w_attn_out', 'delta_conv_w', 'delta_conv_b', 'delta_g_conv', 'delta_w_conv_out', 'delta_w_out', 'delta_g_ff2', 'delta_w_ff2_in', 'delta_w_ff2_out', 'delta_g_ple', 'delta_w_ple_gate', 'delta_w_ple_proj', 'delta_g_final', 'new_m_g_ff1', 'new_m_w_ff1_in', 'new_m_w_ff1_out', 'new_m_g_mix', 'new_m_w_in', 'new_m_b_f', 'new_m_w_attn_out', 'new_m_conv_w', 'new_m_conv_b', 'new_m_g_conv', 'new_m_w_conv_out', 'new_m_w_out', 'new_m_g_ff2', 'new_m_w_ff2_in', 'new_m_w_ff2_out', 'new_m_g_ple', 'new_m_w_ple_gate', 'new_m_w_ple_proj', 'new_m_g_final', 'new_v_g_ff1', 'new_v_w_ff1_in', 'new_v_w_ff1_out', 'new_v_g_mix', 'new_v_w_in', 'new_v_b_f', 'new_v_w_attn_out', 'new_v_conv_w', 'new_v_conv_b', 'new_v_g_conv', 'new_v_w_conv_out', 'new_v_w_out', 'new_v_g_ff2', 'new_v_w_ff2_in', 'new_v_w_ff2_out', 'new_v_g_ple', 'new_v_w_ple_gate', 'new_v_w_ple_proj', 'new_v_g_final']
TWIN_LEAF_KINDS = {'loss': 'loss', 'grad_x': 'grad_x', 'grad_g_ff1': 'grad_w', 'grad_w_ff1_in': 'grad_w', 'grad_w_ff1_out': 'grad_w', 'grad_g_mix': 'grad_w', 'grad_w_in': 'grad_w', 'grad_b_f': 'grad_w', 'grad_w_attn_out': 'grad_w', 'grad_conv_w': 'grad_w', 'grad_conv_b': 'grad_w', 'grad_g_conv': 'grad_w', 'grad_w_conv_out': 'grad_w', 'grad_w_out': 'grad_w', 'grad_g_ff2': 'grad_w', 'grad_w_ff2_in': 'grad_w', 'grad_w_ff2_out': 'grad_w', 'grad_g_ple': 'grad_w', 'grad_w_ple_gate': 'grad_w', 'grad_w_ple_proj': 'grad_w', 'grad_g_final': 'grad_w', 'delta_g_ff1': 'delta_w', 'delta_w_ff1_in': 'delta_w', 'delta_w_ff1_out': 'delta_w', 'delta_g_mix': 'delta_w', 'delta_w_in': 'delta_w', 'delta_b_f': 'delta_w', 'delta_w_attn_out': 'delta_w', 'delta_conv_w': 'delta_w', 'delta_conv_b': 'delta_w', 'delta_g_conv': 'delta_w', 'delta_w_conv_out': 'delta_w', 'delta_w_out': 'delta_w', 'delta_g_ff2': 'delta_w', 'delta_w_ff2_in': 'delta_w', 'delta_w_ff2_out': 'delta_w', 'delta_g_ple': 'delta_w', 'delta_w_ple_gate': 'delta_w', 'delta_w_ple_proj': 'delta_w', 'delta_g_final': 'delta_w', 'new_m_g_ff1': 'new_m', 'new_m_w_ff1_in': 'new_m', 'new_m_w_ff1_out': 'new_m', 'new_m_g_mix': 'new_m', 'new_m_w_in': 'new_m', 'new_m_b_f': 'new_m', 'new_m_w_attn_out': 'new_m', 'new_m_conv_w': 'new_m', 'new_m_conv_b': 'new_m', 'new_m_g_conv': 'new_m', 'new_m_w_conv_out': 'new_m', 'new_m_w_out': 'new_m', 'new_m_g_ff2': 'new_m', 'new_m_w_ff2_in': 'new_m', 'new_m_w_ff2_out': 'new_m', 'new_m_g_ple': 'new_m', 'new_m_w_ple_gate': 'new_m', 'new_m_w_ple_proj': 'new_m', 'new_m_g_final': 'new_m', 'new_v_g_ff1': 'new_v', 'new_v_w_ff1_in': 'new_v', 'new_v_w_ff1_out': 'new_v', 'new_v_g_mix': 'new_v', 'new_v_w_in': 'new_v', 'new_v_b_f': 'new_v', 'new_v_w_attn_out': 'new_v', 'new_v_conv_w': 'new_v', 'new_v_conv_b': 'new_v', 'new_v_g_conv': 'new_v', 'new_v_w_conv_out': 'new_v', 'new_v_w_out': 'new_v', 'new_v_g_ff2': 'new_v', 'new_v_w_ff2_in': 'new_v', 'new_v_w_ff2_out': 'new_v', 'new_v_g_ple': 'new_v', 'new_v_w_ple_gate': 'new_v', 'new_v_w_ple_proj': 'new_v', 'new_v_g_final': 'new_v'}


def _forward(args):
    return _fwd_reference(*[args[k] for k in FWD_PARAMS])


def _output_shape():
    out = _jax.eval_shape(lambda: _forward(_fwd_setup_inputs(0)))
    return out.shape, out.dtype

N_MICROBATCH = 1
ADAM_LR = 0.001
ADAM_B1 = 0.9
ADAM_B2 = 0.999
ADAM_EPS = 1e-08
ADAM_WD = 0.01
ADAM_STEP = 10
PER_EXAMPLE_BATCH_AXIS = {'x': 0, 'p': 1, 'loss_target': 0}
SHARED_INPUTS = []
_WEIGHT_DTYPES = {'g_ff1': _jnp.float32, 'w_ff1_in': _jnp.float32, 'w_ff1_out': _jnp.float32, 'g_mix': _jnp.float32, 'w_in': _jnp.float32, 'b_f': _jnp.float32, 'w_attn_out': _jnp.float32, 'conv_w': _jnp.float32, 'conv_b': _jnp.float32, 'g_conv': _jnp.float32, 'w_conv_out': _jnp.float32, 'w_out': _jnp.float32, 'g_ff2': _jnp.float32, 'w_ff2_in': _jnp.float32, 'w_ff2_out': _jnp.float32, 'g_ple': _jnp.float32, 'w_ple_gate': _jnp.float32, 'w_ple_proj': _jnp.float32, 'g_final': _jnp.float32}
MOMENT_SCALE = {'g_ff1': 4.153864e-02, 'w_ff1_in': 1.774910e-02, 'w_ff1_out': 2.895966e-02, 'g_mix': 4.701875e-02, 'w_in': 2.219725e-02, 'b_f': 2.204813e-01, 'w_attn_out': 2.203260e-02, 'conv_w': 4.242713e-02, 'conv_b': 9.777670e-02, 'g_conv': 5.543121e-02, 'w_conv_out': 3.040773e-02, 'w_out': 3.707307e-02, 'g_ff2': 3.720842e-02, 'w_ff2_in': 1.556399e-02, 'w_ff2_out': 2.541942e-02, 'g_ple': 1.769381e-02, 'w_ple_gate': 1.805550e-02, 'w_ple_proj': 4.557264e-02, 'g_final': 1.606150e+01}


def _to_microbatches(a, axis):
    t = _jnp.moveaxis(a, axis, 0)
    t = t.reshape((N_MICROBATCH, t.shape[0] // N_MICROBATCH) + t.shape[1:])
    return _jnp.moveaxis(t, 1, axis + 1)


def setup_inputs(seed: int = 0) -> dict:
    inp = _fwd_setup_inputs(seed)
    key = _jax.random.fold_in(_jax.random.key(seed), 7919)
    shape, _ = _output_shape()
    out = dict(inp)
    out["loss_target"] = _jax.random.normal(_jax.random.fold_in(key, 0), shape, _jnp.float32)
    for i, name in enumerate(TWIN_WEIGHTS):
        w = inp[name].astype(_jnp.float32)
        if MOMENT_SCALE is None:
            s = _jnp.sqrt(_jnp.mean(_jnp.square(w)) + 1e-30)
        else:
            s = MOMENT_SCALE[name]
        km, kv = _jax.random.split(_jax.random.fold_in(key, i + 1))
        out[name] = w
        out["m_" + name] = s * _jax.random.normal(km, w.shape, _jnp.float32)
        out["v_" + name] = (s * s) * _jax.random.uniform(kv, w.shape, _jnp.float32, 0.5, 1.5)
    if N_MICROBATCH > 1:
        for name, axis in PER_EXAMPLE_BATCH_AXIS.items():
            out[name] = _to_microbatches(out[name], axis)
    return {'x': out['x'], 'p': out['p'], 'g_ff1': out['g_ff1'], 'w_ff1_in': out['w_ff1_in'], 'w_ff1_out': out['w_ff1_out'], 'g_mix': out['g_mix'], 'w_in': out['w_in'], 'b_f': out['b_f'], 'w_attn_out': out['w_attn_out'], 'conv_w': out['conv_w'], 'conv_b': out['conv_b'], 'g_conv': out['g_conv'], 'w_conv_out': out['w_conv_out'], 'w_out': out['w_out'], 'g_ff2': out['g_ff2'], 'w_ff2_in': out['w_ff2_in'], 'w_ff2_out': out['w_ff2_out'], 'g_ple': out['g_ple'], 'w_ple_gate': out['w_ple_gate'], 'w_ple_proj': out['w_ple_proj'], 'g_final': out['g_final'], 'loss_target': out['loss_target'], 'm_g_ff1': out['m_g_ff1'], 'm_w_ff1_in': out['m_w_ff1_in'], 'm_w_ff1_out': out['m_w_ff1_out'], 'm_g_mix': out['m_g_mix'], 'm_w_in': out['m_w_in'], 'm_b_f': out['m_b_f'], 'm_w_attn_out': out['m_w_attn_out'], 'm_conv_w': out['m_conv_w'], 'm_conv_b': out['m_conv_b'], 'm_g_conv': out['m_g_conv'], 'm_w_conv_out': out['m_w_conv_out'], 'm_w_out': out['m_w_out'], 'm_g_ff2': out['m_g_ff2'], 'm_w_ff2_in': out['m_w_ff2_in'], 'm_w_ff2_out': out['m_w_ff2_out'], 'm_g_ple': out['m_g_ple'], 'm_w_ple_gate': out['m_w_ple_gate'], 'm_w_ple_proj': out['m_w_ple_proj'], 'm_g_final': out['m_g_final'], 'v_g_ff1': out['v_g_ff1'], 'v_w_ff1_in': out['v_w_ff1_in'], 'v_w_ff1_out': out['v_w_ff1_out'], 'v_g_mix': out['v_g_mix'], 'v_w_in': out['v_w_in'], 'v_b_f': out['v_b_f'], 'v_w_attn_out': out['v_w_attn_out'], 'v_conv_w': out['v_conv_w'], 'v_conv_b': out['v_conv_b'], 'v_g_conv': out['v_g_conv'], 'v_w_conv_out': out['v_w_conv_out'], 'v_w_out': out['v_w_out'], 'v_g_ff2': out['v_g_ff2'], 'v_w_ff2_in': out['v_w_ff2_in'], 'v_w_ff2_out': out['v_w_ff2_out'], 'v_g_ple': out['v_g_ple'], 'v_w_ple_gate': out['v_w_ple_gate'], 'v_w_ple_proj': out['v_w_ple_proj'], 'v_g_final': out['v_g_final']}


def _loss(weights, diff, rest, loss_target):
    with _jax.named_scope("forward"):
        args = {**rest, TWIN_DIFF_INPUT: diff, **{k: w.astype(_WEIGHT_DTYPES[k]) for k, w in weights.items()}}
        y = _forward(args)
    with _jax.named_scope("loss_head"):
        err = _jnp.square(y.astype(_jnp.float32) - loss_target)
        return 0.5 * _jnp.sum(_jnp.mean(err, axis=-1)) if err.ndim else 0.5 * err


def _adamw(w, g, m, v):
    m = ADAM_B1 * m + (1.0 - ADAM_B1) * g
    v = ADAM_B2 * v + (1.0 - ADAM_B2) * _jnp.square(g)
    m_hat = m / (1.0 - ADAM_B1 ** ADAM_STEP)
    v_hat = v / (1.0 - ADAM_B2 ** ADAM_STEP)
    delta = -ADAM_LR * (m_hat / (_jnp.sqrt(v_hat) + ADAM_EPS) + ADAM_WD * w)
    return delta, m, v


def reference(x, p, g_ff1, w_ff1_in, w_ff1_out, g_mix, w_in, b_f, w_attn_out, conv_w, conv_b, g_conv, w_conv_out, w_out, g_ff2, w_ff2_in, w_ff2_out, g_ple, w_ple_gate, w_ple_proj, g_final, loss_target, m_g_ff1, m_w_ff1_in, m_w_ff1_out, m_g_mix, m_w_in, m_b_f, m_w_attn_out, m_conv_w, m_conv_b, m_g_conv, m_w_conv_out, m_w_out, m_g_ff2, m_w_ff2_in, m_w_ff2_out, m_g_ple, m_w_ple_gate, m_w_ple_proj, m_g_final, v_g_ff1, v_w_ff1_in, v_w_ff1_out, v_g_mix, v_w_in, v_b_f, v_w_attn_out, v_conv_w, v_conv_b, v_g_conv, v_w_conv_out, v_w_out, v_g_ff2, v_w_ff2_in, v_w_ff2_out, v_g_ple, v_w_ple_gate, v_w_ple_proj, v_g_final):
    given = dict(x=x, p=p, g_ff1=g_ff1, w_ff1_in=w_ff1_in, w_ff1_out=w_ff1_out, g_mix=g_mix, w_in=w_in, b_f=b_f, w_attn_out=w_attn_out, conv_w=conv_w, conv_b=conv_b, g_conv=g_conv, w_conv_out=w_conv_out, w_out=w_out, g_ff2=g_ff2, w_ff2_in=w_ff2_in, w_ff2_out=w_ff2_out, g_ple=g_ple, w_ple_gate=w_ple_gate, w_ple_proj=w_ple_proj, g_final=g_final, loss_target=loss_target, m_g_ff1=m_g_ff1, m_w_ff1_in=m_w_ff1_in, m_w_ff1_out=m_w_ff1_out, m_g_mix=m_g_mix, m_w_in=m_w_in, m_b_f=m_b_f, m_w_attn_out=m_w_attn_out, m_conv_w=m_conv_w, m_conv_b=m_conv_b, m_g_conv=m_g_conv, m_w_conv_out=m_w_conv_out, m_w_out=m_w_out, m_g_ff2=m_g_ff2, m_w_ff2_in=m_w_ff2_in, m_w_ff2_out=m_w_ff2_out, m_g_ple=m_g_ple, m_w_ple_gate=m_w_ple_gate, m_w_ple_proj=m_w_ple_proj, m_g_final=m_g_final, v_g_ff1=v_g_ff1, v_w_ff1_in=v_w_ff1_in, v_w_ff1_out=v_w_ff1_out, v_g_mix=v_g_mix, v_w_in=v_w_in, v_b_f=v_b_f, v_w_attn_out=v_w_attn_out, v_conv_w=v_conv_w, v_conv_b=v_conv_b, v_g_conv=v_g_conv, v_w_conv_out=v_w_conv_out, v_w_out=v_w_out, v_g_ff2=v_g_ff2, v_w_ff2_in=v_w_ff2_in, v_w_ff2_out=v_w_ff2_out, v_g_ple=v_g_ple, v_w_ple_gate=v_w_ple_gate, v_w_ple_proj=v_w_ple_proj, v_g_final=v_g_final)
    weights = {n: given[n] for n in TWIN_WEIGHTS}
    shared = {n: given[n] for n in SHARED_INPUTS}
    per_example = {n: given[n] for n in ['x', 'p']}
    grad_fn = _jax.value_and_grad(_loss, argnums=(0, 1))

    def one_microbatch(ex, loss_target):
        ex = dict(ex)
        diff = ex.pop(TWIN_DIFF_INPUT)
        return grad_fn(weights, diff, {**shared, **ex}, loss_target)

    if N_MICROBATCH == 1:
        loss, (grad_w, grad_x) = one_microbatch(per_example, given["loss_target"])
    else:
        def body(carry, xs):
            loss_sum, grad_sum = carry
            l_k, (gw_k, gx_k) = one_microbatch(xs[0], xs[1])
            with _jax.named_scope("update"):
                return (loss_sum + l_k, _jax.tree.map(_jnp.add, grad_sum, gw_k)), gx_k

        init = (_jnp.zeros((), _jnp.float32), _jax.tree.map(_jnp.zeros_like, weights))
        (loss, grad_w), grad_x = _jax.lax.scan(body, init, (per_example, given["loss_target"]))
    with _jax.named_scope("update"):
        delta_w, new_m, new_v = {}, {}, {}
        for n in TWIN_WEIGHTS:
            delta_w[n], new_m[n], new_v[n] = _adamw(weights[n], grad_w[n], given["m_" + n], given["v_" + n])
    return (loss, grad_x, *[grad_w[n] for n in TWIN_WEIGHTS], *[delta_w[n] for n in TWIN_WEIGHTS],
            *[new_m[n] for n in TWIN_WEIGHTS], *[new_v[n] for n in TWIN_WEIGHTS])
```

```python
import functools
import math

import jax
import jax.numpy as jnp
from jax import lax
from jax.experimental import pallas as pl
from jax.experimental.pallas import tpu as pltpu

F32 = jnp.float32
BF16 = jnp.bfloat16

N_DEV = 8
D_MODEL = 1024
N_HEADS = 8
HEAD_DIM = 64
ATTN_W = N_HEADS * HEAD_DIM
CONV_CH = 512
CONV_K = 31
CONV_HALO = 32
D_FF = 2816
FF_SHARD = 2 * D_FF // N_DEV
D_PLE = 256
EPS = 1e-6
FFN_RES = 0.5
LANES = 128
IN_COLS = 3 * ATTN_W + N_HEADS + 2 * CONV_CH + 2 * D_MODEL
IN_SHARD = IN_COLS // N_DEV
F_PAD = LANES
P_Q, P_K, P_V = 0, ATTN_W, 2 * ATTN_W
P_F = 3 * ATTN_W
P_C = P_F + F_PAD
P_G = P_C + 2 * CONV_CH
IN_PAD = P_G + 2 * D_MODEL
NEG_BIG = -1e30

ADAM_LR, ADAM_B1, ADAM_B2, ADAM_EPS, ADAM_WD, ADAM_STEP = 0.001, 0.9, 0.999, 1e-08, 0.01, 10

VMEM_CAP = 60 * 1024 * 1024
VMEM_SLACK = 12 * 1024 * 1024

NN = (((1,), (0,)), ((), ()))
NT = (((1,), (1,)), ((), ()))
TN = (((0,), (0,)), ((), ()))

MESH = pl.DeviceIdType.MESH
ANY = pl.BlockSpec(memory_space=pl.ANY)


def _nbytes(shape, dtype):
    return math.prod(d for d in shape if d is not None) * jnp.dtype(dtype).itemsize


def _params(block_bytes, n_axes):
    limit = min(VMEM_CAP, 2 * block_bytes + VMEM_SLACK)
    return pltpu.CompilerParams(dimension_semantics=("arbitrary",) * n_axes, vmem_limit_bytes=limit)


def _call(body, name, grid, in_arrays, in_specs, out_shapes, out_specs, scratch=(), aliases=None, extra_bytes=0):
    total = extra_bytes
    for a, s in zip(in_arrays, in_specs):
        if s.block_shape is not None:
            total += _nbytes(s.block_shape, a.dtype)
    for o, s in zip(out_shapes, out_specs):
        if s.block_shape is not None:
            total += _nbytes(s.block_shape, o.dtype)
    return pl.pallas_call(
        body, name=name, grid=grid, in_specs=list(in_specs), out_specs=list(out_specs), out_shape=list(out_shapes),
        scratch_shapes=list(scratch), input_output_aliases=aliases or {},
        compiler_params=_params(total, len(grid)),
    )(*in_arrays)


def _sig(x):
    return 1.0 / (1.0 + jnp.exp(-x))


def _dot(a, b, dims):
    return lax.dot_general(a.astype(BF16), b.astype(BF16), dims, preferred_element_type=F32)


def _rms_stats(x):
    r = lax.rsqrt(jnp.mean(x * x, axis=-1, keepdims=True) + EPS)
    return r, x * r


def _rms_bwd(dn, x, g):
    r, xh = _rms_stats(x)
    t = dn * g
    dx = r * (t - xh * jnp.mean(t * xh, axis=-1, keepdims=True))
    return dx, jnp.sum(dn * xh, axis=0, keepdims=True)


def _row_tile(m, want):
    t = min(m, want)
    assert m % t == 0
    return t


def _mm(name, a, b, *, grid, a_spec, b_spec, out, o_spec, dims, nk=None, k_axis=None, alpha=1.0,
        res=None, res_spec=None, rms=None, acc_shape=None):
    first_axes = len(grid)

    def body(*refs):
        refs = list(refs)
        a_ref, b_ref = refs[:2]
        pos = 2
        if res is not None:
            res_ref = refs[pos]
            pos += 1
        if rms is not None:
            h_ref, g_ref, dres_ref = refs[pos:pos + 3]
            pos += 3
        o_ref = refs[pos]
        pos += 1
        if rms is not None:
            dg_ref = refs[pos]
            pos += 1
        acc_ref = refs[pos] if k_axis is not None else None
        ids = [pl.program_id(ax) for ax in range(first_axes)]
        row_axes = [ids[ax] == 0 for ax in range(first_axes) if ax != k_axis]
        is_first = functools.reduce(jnp.logical_and, row_axes) if row_axes else None

        part = _dot(a_ref[...], b_ref[...], dims)

        def finish(acc):
            val = acc if alpha == 1.0 else acc * alpha
            if res is not None:
                val = val + res_ref[...]
            if rms is not None:
                dx, dg = _rms_bwd(val, h_ref[...], g_ref[...])
                o_ref[...] = (dres_ref[...] + dx).astype(o_ref.dtype)

                @pl.when(is_first)
                def _():
                    dg_ref[...] = dg

                @pl.when(jnp.logical_not(is_first))
                def _():
                    dg_ref[...] += dg
            else:
                o_ref[...] = val.astype(o_ref.dtype)

        if k_axis is None:
            finish(part)
        else:
            k = ids[k_axis]

            @pl.when(k == 0)
            def _():
                acc_ref[...] = part

            @pl.when(k > 0)
            def _():
                acc_ref[...] += part

            @pl.when(k == nk - 1)
            def _():
                finish(acc_ref[...])

    in_arrays, in_specs = [a, b], [a_spec, b_spec]
    if res is not None:
        in_arrays.append(res)
        in_specs.append(res_spec)
    outs, o_specs = [out], [o_spec]
    if rms is not None:
        h, g, dres, row_spec, g_spec = rms
        in_arrays += [h, g, dres]
        in_specs += [row_spec, g_spec, row_spec]
        outs.append(jax.ShapeDtypeStruct(g.shape, F32))
        o_specs.append(g_spec)
    scratch, extra = [], 0
    if k_axis is not None:
        scratch = [pltpu.VMEM(acc_shape, F32)]
        extra = _nbytes(acc_shape, F32)
    res_out = _call(body, name, grid, in_arrays, in_specs, outs, o_specs, scratch, extra_bytes=extra)
    return res_out if rms is not None else res_out[0]


def _sds(shape, dtype):
    return jax.ShapeDtypeStruct(shape, dtype)


def _rmsnorm_fwd(name, h, g):
    m, d = h.shape
    tm = _row_tile(m, 512)

    def body(h_ref, g_ref, o_ref):
        _, xh = _rms_stats(h_ref[...])
        o_ref[...] = (xh * g_ref[...]).astype(BF16)

    row = pl.BlockSpec((tm, d), lambda i: (i, 0))
    return _call(body, name, (m // tm,), [h, g], [row, pl.BlockSpec((1, d), lambda i: (0, 0))],
                 [_sds((m, d), BF16)], [row])[0]


def _swiglu_fwd(name, ab):
    _, m, w = ab.shape
    half = N_DEV // 2
    tm = _row_tile(m, 512)
    ab4 = ab.reshape(2, half, m, w)

    def body(ab_ref, o_ref):
        a, b = ab_ref[0], ab_ref[1]
        o_ref[...] = (a * _sig(a) * b).astype(BF16)

    return _call(body, name, (half, m // tm), [ab4],
                 [pl.BlockSpec((2, None, tm, w), lambda j, i: (0, j, i, 0))],
                 [_sds((half, m, w), BF16)], [pl.BlockSpec((None, tm, w), lambda j, i: (j, i, 0))])[0]


def _swiglu_bwd(name, ab, ds):
    _, m, w = ab.shape
    half = N_DEV // 2
    tm = _row_tile(m, 512)
    ab4 = ab.reshape(2, half, m, w)

    def body(ab_ref, ds_ref, o_ref):
        a, b, d = ab_ref[0], ab_ref[1], ds_ref[...]
        sg = _sig(a)
        o_ref[0] = (d * b * sg * (1.0 + a * (1.0 - sg))).astype(BF16)
        o_ref[1] = (d * a * sg).astype(BF16)

    blk = pl.BlockSpec((2, None, tm, w), lambda j, i: (0, j, i, 0))
    out = _call(body, name, (half, m // tm), [ab4, ds],
                [blk, pl.BlockSpec((None, tm, w), lambda j, i: (j, i, 0))],
                [_sds((2, half, m, w), BF16)], [blk])[0]
    return out.reshape(N_DEV, m, w)


def _inproj(name, u, w_in_p):
    m, d = u.shape
    tm = _row_tile(m, 256)

    def body(u_ref, w_ref, q_ref, k_ref, v_ref, f_ref, c_ref, g_ref):
        z = _dot(u_ref[...], w_ref[...], NN)
        q_ref[...] = z[:, P_Q:P_K].astype(BF16)
        k_ref[...] = z[:, P_K:P_V].astype(BF16)
        v_ref[...] = z[:, P_V:P_F].astype(BF16)
        f_ref[...] = z[:, P_F:P_C]
        c_ref[...] = z[:, P_C:P_G]
        g_ref[...] = z[:, P_G:IN_PAD]

    def rows(width):
        return pl.BlockSpec((tm, width), lambda i: (i, 0))

    widths = (ATTN_W, ATTN_W, ATTN_W, F_PAD, 2 * CONV_CH, 2 * D_MODEL)
    dtypes = (BF16, BF16, BF16, F32, F32, F32)
    return _call(body, name, (m // tm,), [u, w_in_p], [rows(d), pl.BlockSpec((d, IN_PAD), lambda i: (0, 0))],
                 [_sds((m, wd), dt) for wd, dt in zip(widths, dtypes)], [rows(wd) for wd in widths])


def _cumsum_rows(x, reverse):
    m = x.shape[0]
    row = lax.broadcasted_iota(jnp.int32, x.shape, 0)
    sh = 1
    while sh < m:
        if reverse:
            x = x + jnp.where(row < m - sh, pltpu.roll(x, m - sh, axis=0), 0.0)
        else:
            x = x + jnp.where(row >= sh, pltpu.roll(x, sh, axis=0), 0.0)
        sh *= 2
    return x


def _log_sigmoid(x):
    return jnp.minimum(x, 0.0) - jnp.log(1.0 + jnp.exp(-jnp.abs(x)))


def _fgate_fwd(name, zf, bf):
    m, w = zf.shape

    def body(z_ref, b_ref, c_ref):
        c_ref[...] = _cumsum_rows(_log_sigmoid(z_ref[...] + b_ref[...]), reverse=False)

    full = pl.BlockSpec((m, w), lambda i: (0, 0))
    return _call(body, name, (1,), [zf, bf], [full, pl.BlockSpec((1, w), lambda i: (0, 0))], [_sds((m, w), F32)], [full])[0]


def _fgate_bwd(name, dc, zf, bf):
    m, w = zf.shape

    def body(dc_ref, z_ref, b_ref, dz_ref, db_ref):
        dls = _cumsum_rows(dc_ref[...], reverse=True)
        dz = dls * _sig(-(z_ref[...] + b_ref[...]))
        lane = lax.broadcasted_iota(jnp.int32, dz.shape, 1)
        dz = jnp.where(lane < N_HEADS, dz, 0.0)
        dz_ref[...] = dz.astype(BF16)
        db_ref[...] = jnp.sum(dz, axis=0, keepdims=True)

    full = pl.BlockSpec((m, w), lambda i: (0, 0))
    one = pl.BlockSpec((1, w), lambda i: (0, 0))
    return _call(body, name, (1,), [dc, zf, bf], [full, full, one], [_sds((m, w), BF16), _sds((1, w), F32)], [full, one])


def _lane_pick(x, idx):
    lane = lax.broadcasted_iota(jnp.int32, x.shape, 1)
    return jnp.sum(jnp.where(lane == idx, x, 0.0), axis=1, keepdims=True)


def _row_pick(x, idx):
    sub = lax.broadcasted_iota(jnp.int32, x.shape, 0)
    return jnp.sum(jnp.where(sub == idx, x, 0.0), axis=0, keepdims=True)


def _attn_fwd(name, q, k, v, c, ct, t):
    m = q.shape[0]
    n_chunks = m // t
    pairs = N_HEADS // 2
    scale = 1.0 / math.sqrt(HEAD_DIM)

    def body(q_ref, k_ref, v_ref, c_ref, ct_ref, o_ref, lse_ref):
        p = pl.program_id(0)
        i = pl.program_id(1)
        lane = lax.broadcasted_iota(jnp.int32, (t, LANES), 1)
        first = lane < HEAD_DIM
        q2 = q_ref[...]
        zero = jnp.zeros_like(q2)
        q_heads = (jnp.where(first, q2, zero), jnp.where(first, zero, q2))
        cblk = c_ref[...]
        c_cols = (_lane_pick(cblk, 2 * p), _lane_pick(cblk, 2 * p + 1))
        rows = i * t + lax.broadcasted_iota(jnp.int32, (t, t), 0)

        def step(j, carry):
            off = pl.multiple_of(j * t, t)
            kj = k_ref[pl.ds(off, t), :]
            vj = v_ref[pl.ds(off, t), :]
            ctj = ct_ref[j]
            visible = rows >= off + lax.broadcasted_iota(jnp.int32, (t, t), 1)
            new = []
            for e in range(2):
                mx, den, acc = carry[3 * e:3 * e + 3]
                s = _dot(q_heads[e], kj, NT) * scale + (c_cols[e] - _row_pick(ctj, 2 * p + e))
                s = jnp.where(visible, s, NEG_BIG)
                mx2 = jnp.maximum(mx, jnp.max(s, axis=1, keepdims=True))
                corr = jnp.exp(mx - mx2)
                pe = jnp.exp(s - mx2)
                new += [mx2, corr * den + jnp.sum(pe, axis=1, keepdims=True), corr * acc + _dot(pe, vj, NN)]
            return tuple(new)

        col = jnp.full((t, 1), NEG_BIG, F32), jnp.zeros((t, 1), F32), jnp.zeros((t, LANES), F32)
        fin = lax.fori_loop(0, i + 1, step, col + col)
        o_ref[...] = jnp.where(first, fin[2] / fin[1], fin[5] / fin[4]).astype(BF16)
        lse_a = fin[0] + jnp.log(fin[1])
        lse_b = fin[3] + jnp.log(fin[4])
        lse_ref[...] = jnp.where(lane == 0, lse_a, jnp.where(lane == 1, lse_b, 0.0))

    seq = pl.BlockSpec((m, LANES), lambda p, i: (0, p))
    blk = pl.BlockSpec((t, LANES), lambda p, i: (i, p))
    return _call(body, name, (pairs, n_chunks), [q, k, v, c, ct],
                 [blk, seq, seq, pl.BlockSpec((t, LANES), lambda p, i: (i, 0)),
                  pl.BlockSpec((n_chunks, 8, t), lambda p, i: (0, 0, 0))],
                 [_sds((m, ATTN_W), BF16), _sds((pairs, m, LANES), F32)],
                 [blk, pl.BlockSpec((None, t, LANES), lambda p, i: (p, i, 0))])


def _attn_bwd(name, q, k, v, c, ct, o, lse, do, t):
    m = q.shape[0]
    n_chunks = m // t
    pairs = N_HEADS // 2
    scale = 1.0 / math.sqrt(HEAD_DIM)

    def body(q_ref, do_ref, o_ref, lse_ref, c_ref, ct_ref, k_ref, v_ref, dq_ref, dk_ref, dv_ref, dcs_ref, drs_ref):
        p = pl.program_id(0)
        j = pl.program_id(1)

        @pl.when(j == 0)
        def _():
            dq_ref[...] = jnp.zeros_like(dq_ref)
            drs_ref[...] = jnp.zeros_like(drs_ref)

        lane = lax.broadcasted_iota(jnp.int32, (t, LANES), 1)
        first = lane < HEAD_DIM
        kj, vj = k_ref[...], v_ref[...]
        zero = jnp.zeros_like(kj)
        k_heads = (jnp.where(first, kj, zero), jnp.where(first, zero, kj))
        v_heads = (jnp.where(first, vj, zero), jnp.where(first, zero, vj))
        ctj = ct_ref[j]
        c_rows = (_row_pick(ctj, 2 * p), _row_pick(ctj, 2 * p + 1))
        cols = j * t + lax.broadcasted_iota(jnp.int32, (t, t), 1)

        def step(i, carry):
            off = pl.multiple_of(i * t, t)
            qi = q_ref[pl.ds(off, t), :]
            doi = do_ref[pl.ds(off, t), :]
            prod = doi * o_ref[pl.ds(off, t), :].astype(F32)
            lsei = lse_ref[pl.ds(off, t), :]
            ci = c_ref[pl.ds(off, t), :]
            dob = doi.astype(BF16)
            visible = off + lax.broadcasted_iota(jnp.int32, (t, t), 0) >= cols
            new = []
            dq = jnp.zeros((t, LANES), F32)
            drow = jnp.zeros((t, LANES), F32)
            for e in range(2):
                dk, dv, dcol = carry[3 * e:3 * e + 3]
                delta = jnp.sum(jnp.where(first if e == 0 else jnp.logical_not(first), prod, 0.0), axis=1, keepdims=True)
                s = _dot(qi, k_heads[e], NT) * scale + (_lane_pick(ci, 2 * p + e) - c_rows[e])
                pe = jnp.where(visible, jnp.exp(s - _lane_pick(lsei, e)), 0.0)
                dsc = pe * (_dot(dob, v_heads[e], NT) - delta)
                dsb = dsc.astype(BF16)
                dq = dq + _dot(dsb, k_heads[e], NN)
                drow = jnp.where(lane == e, jnp.sum(dsc, axis=1, keepdims=True), drow)
                new += [dk + _dot(dsb, qi, TN), dv + _dot(pe, dob, TN), dcol + jnp.sum(dsc, axis=0, keepdims=True)]
            dq_ref[pl.ds(off, t), :] += dq * scale
            drs_ref[pl.ds(off, t), :] += drow
            return tuple(new)

        z = jnp.zeros((t, LANES), F32), jnp.zeros((t, LANES), F32), jnp.zeros((1, t), F32)
        fin = lax.fori_loop(j, n_chunks, step, z + z)
        dk_ref[...] = jnp.where(first, fin[0], fin[3]) * scale
        dv_ref[...] = jnp.where(first, fin[1], fin[4])
        sub = lax.broadcasted_iota(jnp.int32, (8, t), 0)
        dcs_ref[...] = jnp.where(sub == 0, fin[2], jnp.where(sub == 1, fin[5], 0.0))

    seq = pl.BlockSpec((m, LANES), lambda p, j: (0, p))
    blk = pl.BlockSpec((t, LANES), lambda p, j: (j, p))
    return _call(body, name, (pairs, n_chunks), [q, do, o, lse, c, ct, k, v],
                 [seq, seq, seq, pl.BlockSpec((None, m, LANES), lambda p, j: (p, 0, 0)),
                  pl.BlockSpec((m, LANES), lambda p, j: (0, 0)),
                  pl.BlockSpec((n_chunks, 8, t), lambda p, j: (0, 0, 0)), blk, blk],
                 [_sds((m, ATTN_W), F32), _sds((m, ATTN_W), F32), _sds((m, ATTN_W), F32),
                  _sds((pairs, n_chunks, 8, t), F32), _sds((pairs, m, LANES), F32)],
                 [seq, blk, blk, pl.BlockSpec((None, None, 8, t), lambda p, j: (p, j, 0, 0)),
                  pl.BlockSpec((None, m, LANES), lambda p, j: (p, 0, 0))])


def _glu(z):
    return z[:, :CONV_CH] * _sig(z[:, CONV_CH:])


def _shifted(x, lead, tm):
    n = x.shape[0]
    return pltpu.roll(x, (n - lead) % n, axis=0)[:tm]


def _conv_fwd(name, zc, w, b, g, tm):
    m = zc.shape[0]
    hb = tm // CONV_HALO

    def body(cur_ref, prev_ref, w_ref, b_ref, g_ref, cv_ref, y_ref):
        i = pl.program_id(0)
        a_prev = jnp.where(i > 0, _glu(prev_ref[...]), 0.0)
        af = jnp.concatenate([a_prev, _glu(cur_ref[...])], axis=0)
        y = jnp.zeros((tm, CONV_CH), F32)
        for tap in range(CONV_K):
            y = y + w_ref[pl.ds(tap, 1), :] * _shifted(af, CONV_HALO - (CONV_K - 1) + tap, tm)
        y = y + b_ref[...]
        y_ref[...] = y
        _, xh = _rms_stats(y)
        rn = xh * g_ref[...]
        cv_ref[...] = (rn * _sig(rn)).astype(BF16)

    one = pl.BlockSpec((1, CONV_CH), lambda i: (0, 0))
    out = pl.BlockSpec((tm, CONV_CH), lambda i: (i, 0))
    return _call(body, name, (m // tm,), [zc, zc, w, b, g],
                 [pl.BlockSpec((tm, 2 * CONV_CH), lambda i: (i, 0)),
                  pl.BlockSpec((CONV_HALO, 2 * CONV_CH), lambda i: (jnp.maximum(i * hb - 1, 0), 0)),
                  pl.BlockSpec((CONV_HALO, CONV_CH), lambda i: (0, 0)), one, one],
                 [_sds((m, CONV_CH), BF16), _sds((m, CONV_CH), F32)], [out, out])


def _conv_bwd(name, dcv, y, zc, w, g, tm):
    m = zc.shape[0]
    hb = tm // CONV_HALO
    n_blocks = m // tm

    def body(dcv_ref, dcvn_ref, y_ref, yn_ref, cur_ref, prev_ref, w_ref, g_ref, dz_ref, dw_ref, db_ref, dg_ref):
        i = pl.program_id(0)
        gv = g_ref[...]

        def dy_of(d, yv):
            r, xh = _rms_stats(yv)
            rn = xh * gv
            sg = _sig(rn)
            drn = d * sg * (1.0 + rn * (1.0 - sg))
            tt = drn * gv
            return r * (tt - xh * jnp.mean(tt * xh, axis=-1, keepdims=True)), drn * xh

        dy, dgt = dy_of(dcv_ref[...], y_ref[...])
        dy_next, _ = dy_of(dcvn_ref[...], yn_ref[...])
        dyf = jnp.concatenate([dy, jnp.where(i < n_blocks - 1, dy_next, 0.0)], axis=0)
        cur = cur_ref[...]
        af = jnp.concatenate([jnp.where(i > 0, _glu(prev_ref[...]), 0.0), _glu(cur)], axis=0)

        @pl.when(i == 0)
        def _():
            dw_ref[...] = jnp.zeros_like(dw_ref)
            db_ref[...] = jnp.zeros_like(db_ref)
            dg_ref[...] = jnp.zeros_like(dg_ref)

        da = jnp.zeros((tm, CONV_CH), F32)
        for tap in range(CONV_K):
            da = da + w_ref[pl.ds(tap, 1), :] * _shifted(dyf, CONV_K - 1 - tap, tm)
            a_tap = _shifted(af, CONV_HALO - (CONV_K - 1) + tap, tm)
            dw_ref[pl.ds(tap, 1), :] += jnp.sum(dy * a_tap, axis=0, keepdims=True)
        db_ref[...] += jnp.sum(dy, axis=0, keepdims=True)
        dg_ref[...] += jnp.sum(dgt, axis=0, keepdims=True)
        c1, sg2 = cur[:, :CONV_CH], _sig(cur[:, CONV_CH:])
        dz_ref[:, :CONV_CH] = (da * sg2).astype(BF16)
        dz_ref[:, CONV_CH:] = (da * c1 * sg2 * (1.0 - sg2)).astype(BF16)

    one = pl.BlockSpec((1, CONV_CH), lambda i: (0, 0))
    taps = pl.BlockSpec((CONV_HALO, CONV_CH), lambda i: (0, 0))
    row = pl.BlockSpec((tm, CONV_CH), lambda i: (i, 0))
    nxt = pl.BlockSpec((CONV_HALO, CONV_CH), lambda i: (jnp.minimum((i + 1) * hb, m // CONV_HALO - 1), 0))
    row2 = pl.BlockSpec((tm, 2 * CONV_CH), lambda i: (i, 0))
    return _call(body, name, (n_blocks,), [dcv, dcv, y, y, zc, zc, w, g],
                 [row, nxt, row, nxt, row2,
                  pl.BlockSpec((CONV_HALO, 2 * CONV_CH), lambda i: (jnp.maximum(i * hb - 1, 0), 0)), taps, one],
                 [_sds((m, 2 * CONV_CH), BF16), _sds((CONV_HALO, CONV_CH), F32), _sds((1, CONV_CH), F32), _sds((1, CONV_CH), F32)],
                 [row2, taps, one, one])


def _mixout_fwd(name, o, cv, zg, h, w_ao, w_co, w_out, tm):
    m, d = h.shape

    def body(o_ref, cv_ref, zg_ref, h_ref, wa_ref, wc_ref, wo_ref, hn_ref, mg_ref, ya_ref, yc_ref):
        ya = _dot(o_ref[...], wa_ref[...], NN)
        yc = _dot(cv_ref[...], wc_ref[...], NN)
        zg_v = zg_ref[...]
        mg = (_sig(zg_v[:, :d]) * ya + _sig(zg_v[:, d:]) * yc).astype(BF16)
        ya_ref[...] = ya
        yc_ref[...] = yc
        mg_ref[...] = mg
        hn_ref[...] = h_ref[...] + _dot(mg, wo_ref[...], NN)

    def rows(wd):
        return pl.BlockSpec((tm, wd), lambda i: (i, 0))

    def whole(a):
        return pl.BlockSpec(a.shape, lambda i: (0, 0))

    return _call(body, name, (m // tm,), [o, cv, zg, h, w_ao, w_co, w_out],
                 [rows(ATTN_W), rows(CONV_CH), rows(2 * d), rows(d), whole(w_ao), whole(w_co), whole(w_out)],
                 [_sds((m, d), F32), _sds((m, d), BF16), _sds((m, d), F32), _sds((m, d), F32)],
                 [rows(d), rows(d), rows(d), rows(d)])


def _mixout_bwd(name, dh, zg, ya, yc, w_ao, w_co, w_out, tm):
    m, d = dh.shape

    def body(dh_ref, zg_ref, ya_ref, yc_ref, wa_ref, wc_ref, wo_ref, do_ref, dcv_ref, dzg_ref, dya_ref, dyc_ref):
        dm = _dot(dh_ref[...], wo_ref[...], NT)
        zg_v = zg_ref[...]
        sa, sc = _sig(zg_v[:, :d]), _sig(zg_v[:, d:])
        dya = (dm * sa).astype(BF16)
        dyc = (dm * sc).astype(BF16)
        dzg_ref[:, :d] = (dm * ya_ref[...] * sa * (1.0 - sa)).astype(BF16)
        dzg_ref[:, d:] = (dm * yc_ref[...] * sc * (1.0 - sc)).astype(BF16)
        dya_ref[...] = dya
        dyc_ref[...] = dyc
        do_ref[...] = _dot(dya, wa_ref[...], NT)
        dcv_ref[...] = _dot(dyc, wc_ref[...], NT)

    def rows(wd):
        return pl.BlockSpec((tm, wd), lambda i: (i, 0))

    def whole(a):
        return pl.BlockSpec(a.shape, lambda i: (0, 0))

    return _call(body, name, (m // tm,), [dh, zg, ya, yc, w_ao, w_co, w_out],
                 [rows(d), rows(2 * d), rows(d), rows(d), whole(w_ao), whole(w_co), whole(w_out)],
                 [_sds((m, ATTN_W), F32), _sds((m, CONV_CH), F32), _sds((m, 2 * d), BF16), _sds((m, d), BF16), _sds((m, d), BF16)],
                 [rows(ATTN_W), rows(CONV_CH), rows(2 * d), rows(d), rows(d)])


def _ple_fwd(name, h, p, g, w_gate, w_proj, tm):
    m, d = h.shape

    def body(h_ref, p_ref, g_ref, wg_ref, wp_ref, hn_ref, n_ref, gl_ref, pp_ref):
        hv = h_ref[...]
        _, xh = _rms_stats(hv)
        n = (xh * g_ref[...]).astype(BF16)
        gl = _dot(n, wg_ref[...], NN)
        pp = _dot(p_ref[...], wp_ref[...], NN)
        n_ref[...] = n
        gl_ref[...] = gl
        pp_ref[...] = pp
        hn_ref[...] = hv + _sig(gl) * pp

    def rows(wd):
        return pl.BlockSpec((tm, wd), lambda i: (i, 0))

    return _call(body, name, (m // tm,), [h, p, g, w_gate, w_proj],
                 [rows(d), rows(D_PLE), pl.BlockSpec((1, d), lambda i: (0, 0)),
                  pl.BlockSpec((d, d), lambda i: (0, 0)), pl.BlockSpec((D_PLE, d), lambda i: (0, 0))],
                 [_sds((m, d), F32), _sds((m, d), BF16), _sds((m, d), F32), _sds((m, d), F32)],
                 [rows(d)] * 4)


def _ple_bwd(name, dh, h, g, gl, pp, w_gate, tm):
    m, d = h.shape

    def body(dh_ref, h_ref, g_ref, gl_ref, pp_ref, wg_ref, dhn_ref, dgl_ref, dpp_ref, dg_ref):
        dhv = dh_ref[...]
        sg = _sig(gl_ref[...])
        dgl = (dhv * pp_ref[...] * sg * (1.0 - sg)).astype(BF16)
        dgl_ref[...] = dgl
        dpp_ref[...] = (dhv * sg).astype(BF16)
        dx, dg = _rms_bwd(_dot(dgl, wg_ref[...], NT), h_ref[...], g_ref[...])
        dhn_ref[...] = dhv + dx

        @pl.when(pl.program_id(0) == 0)
        def _():
            dg_ref[...] = dg

        @pl.when(pl.program_id(0) > 0)
        def _():
            dg_ref[...] += dg

    rows = pl.BlockSpec((tm, d), lambda i: (i, 0))
    one = pl.BlockSpec((1, d), lambda i: (0, 0))
    return _call(body, name, (m // tm,), [dh, h, g, gl, pp, w_gate],
                 [rows, rows, one, rows, rows, pl.BlockSpec((d, d), lambda i: (0, 0))],
                 [_sds((m, d), F32), _sds((m, d), BF16), _sds((m, d), BF16), _sds((1, d), F32)],
                 [rows, rows, rows, one])


def _loss_head(name, h, g, target, tm):
    m, d = h.shape

    def body(h_ref, g_ref, t_ref, loss_ref, dh_ref, dg_ref):
        hv, gv = h_ref[...], g_ref[...]
        r, xh = _rms_stats(hv)
        err = xh * gv - t_ref[...]
        dy = err * (1.0 / d)
        tt = dy * gv
        dh_ref[...] = r * (tt - xh * jnp.mean(tt * xh, axis=-1, keepdims=True))
        dg = jnp.sum(dy * xh, axis=0, keepdims=True)
        part = jnp.zeros((1, LANES), F32) + 0.5 * jnp.sum(jnp.sum(err * err, axis=1, keepdims=True), axis=0, keepdims=True) * (1.0 / d)

        @pl.when(pl.program_id(0) == 0)
        def _():
            dg_ref[...] = dg
            loss_ref[...] = part

        @pl.when(pl.program_id(0) > 0)
        def _():
            dg_ref[...] += dg
            loss_ref[...] += part

    rows = pl.BlockSpec((tm, d), lambda i: (i, 0))
    one = pl.BlockSpec((1, d), lambda i: (0, 0))
    return _call(body, name, (m // tm,), [h, g, target], [rows, one, rows],
                 [_sds((1, LANES), F32), _sds((m, d), F32), _sds((1, d), F32)],
                 [pl.BlockSpec((1, LANES), lambda i: (0, 0)), rows, one])


def _place():
    return lax.axis_index("x"), lax.axis_index("y"), lax.axis_index("c")


def _linear(px, py, pc):
    return 4 * px + 2 * py + pc


def _block_of(ref, idx, rows_per_block):
    if rows_per_block is None:
        return ref.at[idx]
    return ref.at[pl.ds(pl.multiple_of(idx * rows_per_block, 16), rows_per_block), :]


def _all_gather(name, shards, row_bands):
    n_items = len(shards)
    outs = []
    for s, band in zip(shards, row_bands):
        shape = (N_DEV,) + s.shape if band is None else (N_DEV * band,) + s.shape[1:]
        outs.append(_sds(shape, s.dtype))

    def body(*refs):
        srcs, dsts = refs[:n_items], refs[n_items:2 * n_items]
        send_sems, recv_sems, local_sems = refs[2 * n_items:]
        x, y, c = _place()
        me, sibling = (x, y, c), (x, y, 1 - c)
        chips = [(1 - x, y), (x, 1 - y), (1 - x, 1 - y)]

        def copy(t, k, owner, to, from_input=False):
            view = _block_of(dsts[t], _linear(*owner), row_bands[t])
            return pltpu.make_async_remote_copy(
                src_ref=srcs[t] if from_input else view, dst_ref=view,
                send_sem=send_sems.at[t, k], recv_sem=recv_sems.at[t, k], device_id=to, device_id_type=MESH)

        mine = [pltpu.make_async_copy(srcs[t], _block_of(dsts[t], _linear(*me), row_bands[t]), local_sems.at[t])
                for t in range(n_items)]
        first, passed = [], []
        for t in range(n_items):
            mine[t].start()
            own = [copy(t, 0, me, sibling, True)] + [copy(t, 1 + j, me, (*chip, c), True) for j, chip in enumerate(chips)]
            for cp in own:
                cp.start()
            first += own
        for j, chip in enumerate(chips):
            for t in range(n_items):
                copy(t, 1 + j, (*chip, c), me).wait_recv()
                fwd = copy(t, 4 + j, (*chip, c), sibling)
                fwd.start()
                passed.append(fwd)
        for t in range(n_items):
            copy(t, 0, sibling, me).wait_recv()
            for j, chip in enumerate(chips):
                copy(t, 4 + j, (*chip, 1 - c), me).wait_recv()
        for cp in first + passed:
            cp.wait_send()
        for cp in mine:
            cp.wait()

    return pl.pallas_call(
        body, name=name, in_specs=[ANY] * n_items, out_specs=[ANY] * n_items, out_shape=outs,
        scratch_shapes=[pltpu.SemaphoreType.DMA((n_items, 7)), pltpu.SemaphoreType.DMA((n_items, 7)),
                        pltpu.SemaphoreType.DMA((n_items,))],
    )(*shards)


def _exchange(name, grads, row_bands, gathered):
    n_items = len(grads)
    outs = []
    for gr, band, whole in zip(grads, row_bands, gathered):
        if whole:
            blk = gr.shape
        elif band is None:
            blk = gr.shape[1:]
        else:
            blk = (band,) + gr.shape[1:]
        outs.append(_sds((N_DEV,) + blk, gr.dtype))
    flips = [(dx, dy, dc) for dx in (0, 1) for dy in (0, 1) for dc in (0, 1)][1:]

    def body(*refs):
        srcs, dsts = refs[:n_items], refs[n_items:2 * n_items]
        send_sems, recv_sems, local_sems = refs[2 * n_items:]
        x, y, c = _place()
        my_idx = _linear(x, y, c)

        def src_of(t, idx):
            return srcs[t] if gathered[t] else _block_of(srcs[t], idx, row_bands[t])

        copies = []
        for t in range(n_items):
            local = pltpu.make_async_copy(src_of(t, my_idx), dsts[t].at[my_idx], local_sems.at[t])
            local.start()
            copies.append(local)
            for k, (dx, dy, dc) in enumerate(flips):
                peer = (1 - x if dx else x, 1 - y if dy else y, 1 - c if dc else c)
                cp = pltpu.make_async_remote_copy(
                    src_ref=src_of(t, _linear(*peer)), dst_ref=dsts[t].at[my_idx],
                    send_sem=send_sems.at[t, k], recv_sem=recv_sems.at[t, k], device_id=peer, device_id_type=MESH)
                cp.start()
                copies.append(cp)
        for cp in copies:
            cp.wait()

    return pl.pallas_call(
        body, name=name, in_specs=[ANY] * n_items, out_specs=[ANY] * n_items, out_shape=outs,
        scratch_shapes=[pltpu.SemaphoreType.DMA((n_items, 7)), pltpu.SemaphoreType.DMA((n_items, 7)),
                        pltpu.SemaphoreType.DMA((n_items,))],
    )(*grads)


def _adam_math(g, w, m, v):
    m2 = ADAM_B1 * m + (1.0 - ADAM_B1) * g
    v2 = ADAM_B2 * v + (1.0 - ADAM_B2) * (g * g)
    m_hat = m2 / (1.0 - ADAM_B1 ** ADAM_STEP)
    v_hat = v2 / (1.0 - ADAM_B2 ** ADAM_STEP)
    return -ADAM_LR * (m_hat / (jnp.sqrt(v_hat) + ADAM_EPS) + ADAM_WD * w), m2, v2


def _adamw(name, recv, w, m, v, layer, prev):
    n_layers, r, c = w.shape
    tr = r
    for cand in (256, 176, 128):
        if r > cand and r % cand == 0:
            tr = cand
            break

    def body(*refs):
        recv_ref, w_ref, m_ref, v_ref = refs[:4]
        g_ref, d_ref, m2_ref, v2_ref = refs[-4:]
        g = recv_ref[0].astype(F32)
        for s in range(1, N_DEV):
            g = g + recv_ref[s].astype(F32)
        delta, m2, v2 = _adam_math(g, w_ref[...], m_ref[...], v_ref[...])
        g_ref[...] = g
        d_ref[...] = delta
        m2_ref[...] = m2
        v2_ref[...] = v2

    blk = pl.BlockSpec((None, tr, c), lambda i: (layer, i, 0))
    ins = [recv, w, m, v]
    specs = [pl.BlockSpec((N_DEV, tr, c), lambda i: (0, i, 0)), blk, blk, blk]
    aliases = {}
    if prev is not None:
        ins += list(prev)
        specs += [ANY] * 4
        aliases = {4 + n: n for n in range(4)}
    return _call(body, name, (r // tr,), ins, specs, [_sds(w.shape, F32)] * 4, [blk] * 4, aliases=aliases)


def _adamw_small(name, recv, w, m, v):
    r, c = w.shape

    def body(recv_ref, w_ref, m_ref, v_ref, g_ref, d_ref, m2_ref, v2_ref):
        g = recv_ref[0]
        for s in range(1, N_DEV):
            g = g + recv_ref[s]
        delta, m2, v2 = _adam_math(g, w_ref[...], m_ref[...], v_ref[...])
        g_ref[...] = g
        d_ref[...] = delta
        m2_ref[...] = m2
        v2_ref[...] = v2

    full = pl.BlockSpec((r, c), lambda i: (0, 0))
    return _call(body, name, (1,), [recv, w, m, v], [pl.BlockSpec((N_DEV, r, c), lambda i: (0, 0, 0)), full, full, full],
                 [_sds((r, c), F32)] * 4, [full] * 4)


def _ffn_fwd(tag, h, g, wi, wo):
    m, d = h.shape
    tm = _row_tile(m, 512)
    n = _rmsnorm_fwd(tag + "_norm", h, g)
    ab = _mm(tag + "_in", n, wi, grid=(N_DEV, m // tm), dims=NN,
             a_spec=pl.BlockSpec((tm, d), lambda s, i: (i, 0)), b_spec=pl.BlockSpec((None, d, FF_SHARD), lambda s, i: (s, 0, 0)),
             out=_sds((N_DEV, m, FF_SHARD), F32), o_spec=pl.BlockSpec((None, tm, FF_SHARD), lambda s, i: (s, i, 0)))
    act = _swiglu_fwd(tag + "_act", ab)
    nk = N_DEV // 2
    row = pl.BlockSpec((tm, d), lambda i, k: (i, 0))
    h2 = _mm(tag + "_out", act, wo, grid=(m // tm, nk), dims=NN, nk=nk, k_axis=1, alpha=FFN_RES,
             a_spec=pl.BlockSpec((None, tm, FF_SHARD), lambda i, k: (k, i, 0)), b_spec=pl.BlockSpec((FF_SHARD, d), lambda i, k: (k, 0)),
             out=_sds((m, d), F32), o_spec=row, res=h, res_spec=row, acc_shape=(tm, d))
    return h2, (h, n, ab, act)


def _ffn_bwd(tag, dh, saved, g, wi, wo):
    h, n, ab, act = saved
    m, d = h.shape
    tm = _row_tile(m, 512)
    nk = N_DEV // 2
    ds = _mm(tag + "_dact", dh, wo, grid=(nk, m // tm), dims=NT, alpha=FFN_RES,
             a_spec=pl.BlockSpec((tm, d), lambda j, i: (i, 0)), b_spec=pl.BlockSpec((FF_SHARD, d), lambda j, i: (j, 0)),
             out=_sds((nk, m, FF_SHARD), F32), o_spec=pl.BlockSpec((None, tm, FF_SHARD), lambda j, i: (j, i, 0)))
    dwo = _mm(tag + "_dwo", act, dh, grid=(nk, m // tm), dims=TN, nk=m // tm, k_axis=1, alpha=FFN_RES,
              a_spec=pl.BlockSpec((None, tm, FF_SHARD), lambda j, k: (j, k, 0)), b_spec=pl.BlockSpec((tm, d), lambda j, k: (k, 0)),
              out=_sds((D_FF, d), BF16), o_spec=pl.BlockSpec((FF_SHARD, d), lambda j, k: (j, 0)), acc_shape=(FF_SHARD, d))
    dab = _swiglu_bwd(tag + "_dab", ab, ds)
    row = pl.BlockSpec((tm, d), lambda i, k: (i, 0))
    one = pl.BlockSpec((1, d), lambda i, k: (0, 0))
    dh2, dg = _mm(tag + "_dn", dab, wi, grid=(m // tm, N_DEV), dims=NT, nk=N_DEV, k_axis=1,
                  a_spec=pl.BlockSpec((None, tm, FF_SHARD), lambda i, k: (k, i, 0)),
                  b_spec=pl.BlockSpec((None, d, FF_SHARD), lambda i, k: (k, 0, 0)),
                  out=_sds((m, d), F32), o_spec=row, rms=(h, g, dh, row, one), acc_shape=(tm, d))
    dwi = _mm(tag + "_dwi", n, dab, grid=(N_DEV, m // tm), dims=TN, nk=m // tm, k_axis=1,
              a_spec=pl.BlockSpec((tm, d), lambda s, k: (k, 0)), b_spec=pl.BlockSpec((None, tm, FF_SHARD), lambda s, k: (s, k, 0)),
              out=_sds((N_DEV, d, FF_SHARD), BF16), o_spec=pl.BlockSpec((None, d, FF_SHARD), lambda s, k: (s, 0, 0)),
              acc_shape=(d, FF_SHARD))
    return dh2, dg, dwi, dwo


def _wgrad(name, a, b, tk):
    m = a.shape[0]
    ka, kb = a.shape[1], b.shape[1]
    return _mm(name, a, b, grid=(m // tk,), dims=TN, nk=m // tk, k_axis=0,
               a_spec=pl.BlockSpec((tk, ka), lambda k: (k, 0)), b_spec=pl.BlockSpec((tk, kb), lambda k: (k, 0)),
               out=_sds((ka, kb), BF16), o_spec=pl.BlockSpec((ka, kb), lambda k: (0, 0)), acc_shape=(ka, kb))


def _mixer_fwd(tag, h, g, w_in_p, bf, conv_w, conv_b, g_conv, w_ao, w_co, w_out):
    m, d = h.shape
    t = _row_tile(m, 256)
    u = _rmsnorm_fwd(tag + "_norm", h, g)
    q, k, v, zf, zc, zg = _inproj(tag + "_inproj", u, w_in_p)
    c = _fgate_fwd(tag + "_fgate", zf, bf)
    ct = c[:, :N_HEADS].T.reshape(N_HEADS, m // t, t).transpose(1, 0, 2)
    o, lse = _attn_fwd(tag + "_attn", q, k, v, c, ct, t)
    cv, y = _conv_fwd(tag + "_conv", zc, conv_w, conv_b, g_conv, t)
    h2, mg, ya, yc = _mixout_fwd(tag + "_mixout", o, cv, zg, h, w_ao, w_co, w_out, t)
    return h2, (h, u, q, k, v, zf, zc, zg, c, ct, o, lse, cv, y, mg, ya, yc)


def _mixer_bwd(tag, dh, saved, g, w_in_p, bf, conv_w, g_conv, w_ao, w_co, w_out):
    h, u, q, k, v, zf, zc, zg, c, ct, o, lse, cv, y, mg, ya, yc = saved
    m, d = h.shape
    t = _row_tile(m, 256)
    tk = _row_tile(m, 512)
    do, dcv, dzg, dya, dyc = _mixout_bwd(tag + "_dmixout", dh, zg, ya, yc, w_ao, w_co, w_out, t)
    d_wout = _wgrad(tag + "_dwout", mg, dh, tk)
    d_wao = _wgrad(tag + "_dwao", o, dya, tk)
    d_wco = _wgrad(tag + "_dwco", cv, dyc, tk)
    dq, dk, dv, dcs, drs = _attn_bwd(tag + "_dattn", q, k, v, c, ct, o, lse, do, t)
    dc = drs[:, :, :2].transpose(1, 0, 2).reshape(m, N_HEADS) - dcs[:, :, :2, :].transpose(0, 2, 1, 3).reshape(N_HEADS, m).T
    dc = jnp.pad(dc, ((0, 0), (0, F_PAD - N_HEADS)))
    dzf, dbf = _fgate_bwd(tag + "_dfgate", dc, zf, bf)
    dzc, dconv_w, dconv_b, dg_conv = _conv_bwd(tag + "_dconv", dcv, y, zc, conv_w, g_conv, t)
    dz = jnp.concatenate([dq.astype(BF16), dk.astype(BF16), dv.astype(BF16), dzf, dzc, dzg], axis=1)
    tm = _row_tile(m, 256)
    row = pl.BlockSpec((tm, d), lambda i: (i, 0))
    one = pl.BlockSpec((1, d), lambda i: (0, 0))
    dh2, dg = _mm(tag + "_du", dz, w_in_p, grid=(m // tm,), dims=NT,
                  a_spec=pl.BlockSpec((tm, IN_PAD), lambda i: (i, 0)), b_spec=pl.BlockSpec((d, IN_PAD), lambda i: (0, 0)),
                  out=_sds((m, d), F32), o_spec=row, rms=(h, g, dh, row, one))
    rt = 256
    d_win_p = _mm(tag + "_dwin", u, dz, grid=(d // rt, m // tk), dims=TN, nk=m // tk, k_axis=1,
                  a_spec=pl.BlockSpec((tk, rt), lambda r, kk: (kk, r)), b_spec=pl.BlockSpec((tk, IN_PAD), lambda r, kk: (kk, 0)),
                  out=_sds((d, IN_PAD), BF16), o_spec=pl.BlockSpec((rt, IN_PAD), lambda r, kk: (r, 0)), acc_shape=(rt, IN_PAD))
    return dh2, dg, d_win_p, dbf, d_wao, dconv_w, dconv_b, dg_conv, d_wco, d_wout


def _col_shards(full, n_cols):
    r = full.shape[0]
    return full.reshape(r, N_DEV, n_cols).transpose(1, 0, 2)


def _from_col_shards(stacked):
    _, r, n = stacked.shape
    return stacked.transpose(1, 0, 2).reshape(r, N_DEV * n)


SMALL_ROWS = 24


def _pack_small(g_ff1, g_mix, g_ff2, g_ple, g_final, conv_b, g_conv, b_f, loss_row):
    n_layers = g_ff1.shape[0]
    bf_row = jnp.pad(b_f.reshape(1, n_layers * N_HEADS), ((0, 0), (0, D_MODEL - n_layers * N_HEADS)))
    parts = [g_ff1, g_mix, g_ff2, g_ple, g_final.reshape(1, D_MODEL), conv_b.reshape(-1, D_MODEL),
             g_conv.reshape(-1, D_MODEL), bf_row, loss_row]
    packed = jnp.concatenate(parts, axis=0)
    return jnp.pad(packed, ((0, SMALL_ROWS - packed.shape[0]), (0, 0)))


def _unpack_small(packed, n_layers):
    ln = n_layers
    cr = n_layers * CONV_CH // D_MODEL
    pos = 4 * ln + 1
    return dict(
        g_ff1=packed[0:ln], g_mix=packed[ln:2 * ln], g_ff2=packed[2 * ln:3 * ln], g_ple=packed[3 * ln:4 * ln],
        g_final=packed[4 * ln], conv_b=packed[pos:pos + cr].reshape(ln, CONV_CH),
        g_conv=packed[pos + cr:pos + 2 * cr].reshape(ln, CONV_CH),
        b_f=packed[pos + 2 * cr, :ln * N_HEADS].reshape(ln, N_HEADS), loss=packed[pos + 2 * cr + 1, 0])


BIG = ("w_ff1_in", "w_ff1_out", "w_in", "w_attn_out", "conv_w", "w_conv_out", "w_out", "w_ff2_in", "w_ff2_out",
       "w_ple_gate", "w_ple_proj")
ROW_BAND = dict(w_ff1_in=None, w_ff1_out=D_FF // N_DEV, w_in=None, w_attn_out=None, conv_w=None, w_conv_out=None,
                w_out=D_MODEL // N_DEV, w_ff2_in=None, w_ff2_out=D_FF // N_DEV, w_ple_gate=D_MODEL // N_DEV, w_ple_proj=None)
SMALL = ("g_ff1", "g_mix", "g_ff2", "g_ple", "g_final", "conv_b", "g_conv", "b_f")
ORDER = ("g_ff1", "w_ff1_in", "w_ff1_out", "g_mix", "w_in", "b_f", "w_attn_out", "conv_w", "conv_b", "g_conv", "w_conv_out",
         "w_out", "g_ff2", "w_ff2_in", "w_ff2_out", "g_ple", "w_ple_gate", "w_ple_proj", "g_final")


def kernel(x, p, g_ff1, w_ff1_in, w_ff1_out, g_mix, w_in, b_f, w_attn_out, conv_w, conv_b, g_conv, w_conv_out, w_out, g_ff2, w_ff2_in, w_ff2_out, g_ple, w_ple_gate, w_ple_proj, g_final, loss_target, m_g_ff1, m_w_ff1_in, m_w_ff1_out, m_g_mix, m_w_in, m_b_f, m_w_attn_out, m_conv_w, m_conv_b, m_g_conv, m_w_conv_out, m_w_out, m_g_ff2, m_w_ff2_in, m_w_ff2_out, m_g_ple, m_w_ple_gate, m_w_ple_proj, m_g_final, v_g_ff1, v_w_ff1_in, v_w_ff1_out, v_g_mix, v_w_in, v_b_f, v_w_attn_out, v_conv_w, v_conv_b, v_g_conv, v_w_conv_out, v_w_out, v_g_ff2, v_w_ff2_in, v_w_ff2_out, v_g_ple, v_w_ple_gate, v_w_ple_proj, v_g_final):
    local = dict(locals())
    W = {n: local[n] for n in ORDER}
    M1 = {n: local["m_" + n] for n in ORDER}
    V1 = {n: local["v_" + n] for n in ORDER}
    n_layers = g_ff1.shape[0]
    m_rows = x.shape[1]
    t = _row_tile(m_rows, 256)

    full = []
    for l in range(n_layers):
        shards = [W[n][l].astype(F32 if n == "conv_w" else BF16) for n in BIG]
        got = dict(zip(BIG, _all_gather("gather_weights", shards, [ROW_BAND[n] for n in BIG])))
        w_in_full = _from_col_shards(got["w_in"])
        zeros = jnp.zeros((D_MODEL, F_PAD - N_HEADS), BF16)
        fw = dict(
            wi1=got["w_ff1_in"], wo1=got["w_ff1_out"], wi2=got["w_ff2_in"], wo2=got["w_ff2_out"],
            w_in_p=jnp.concatenate([w_in_full[:, :P_F + N_HEADS], zeros, w_in_full[:, P_F + N_HEADS:]], axis=1),
            w_ao=_from_col_shards(got["w_attn_out"]), w_co=_from_col_shards(got["w_conv_out"]),
            w_out=got["w_out"], w_pg=got["w_ple_gate"], w_pp=_from_col_shards(got["w_ple_proj"]),
            conv_w=jnp.pad(_from_col_shards(got["conv_w"]), ((0, CONV_HALO - CONV_K), (0, 0))),
            bf=jnp.pad(b_f[l].reshape(1, N_HEADS), ((0, 0), (0, F_PAD - N_HEADS))),
            g1=g_ff1[l].reshape(1, -1), gm=g_mix[l].reshape(1, -1), g2=g_ff2[l].reshape(1, -1), gp=g_ple[l].reshape(1, -1),
            conv_b=conv_b[l].reshape(1, -1), g_conv=g_conv[l].reshape(1, -1))
        full.append(fw)

    h = x[0]
    saved = []
    for l in range(n_layers):
        fw = full[l]
        h, s1 = _ffn_fwd("ff1", h, fw["g1"], fw["wi1"], fw["wo1"])
        h, s2 = _mixer_fwd("mix", h, fw["gm"], fw["w_in_p"], fw["bf"], fw["conv_w"], fw["conv_b"], fw["g_conv"],
                           fw["w_ao"], fw["w_co"], fw["w_out"])
        h, s3 = _ffn_fwd("ff2", h, fw["g2"], fw["wi2"], fw["wo2"])
        h_in = h
        pl_in = p[l, 0]
        h, n_ple, gl, pp = _ple_fwd("ple", h, pl_in, fw["gp"], fw["w_pg"], fw["w_pp"], t)
        saved.append((s1, s2, s3, (h_in, pl_in, n_ple, gl, pp)))

    loss_row, dh, dg_final = _loss_head("loss_head", h, g_final.reshape(1, -1), loss_target[0], t)

    small_grads = {n: [None] * n_layers for n in ("g_ff1", "g_mix", "g_ff2", "g_ple", "conv_b", "g_conv", "b_f")}
    stacked = {n: None for n in BIG}
    tk = _row_tile(m_rows, 512)
    for l in reversed(range(n_layers)):
        fw = full[l]
        s1, s2, s3, (h_in, pl_in, n_ple, gl, pp) = saved[l]
        dh, dgl, dpp, dgp = _ple_bwd("ple_bwd", dh, h_in, fw["gp"], gl, pp, fw["w_pg"], t)
        d_wpg = _wgrad("ple_dwgate", n_ple, dgl, tk)
        d_wpp = _wgrad("ple_dwproj", pl_in, dpp, tk)
        dh, dg2, d_wi2, d_wo2 = _ffn_bwd("ff2", dh, s3, fw["g2"], fw["wi2"], fw["wo2"])
        dh, dgm, d_win_p, dbf, d_wao, dconv_w, dconv_b, dg_conv, d_wco, d_wout = _mixer_bwd(
            "mix", dh, s2, fw["gm"], fw["w_in_p"], fw["bf"], fw["conv_w"], fw["g_conv"], fw["w_ao"], fw["w_co"], fw["w_out"])
        dh, dg1, d_wi1, d_wo1 = _ffn_bwd("ff1", dh, s1, fw["g1"], fw["wi1"], fw["wo1"])
        small_grads["g_ff1"][l], small_grads["g_mix"][l], small_grads["g_ff2"][l], small_grads["g_ple"][l] = dg1, dgm, dg2, dgp
        small_grads["conv_b"][l], small_grads["g_conv"][l] = dconv_b, dg_conv
        small_grads["b_f"][l] = dbf[:, :N_HEADS]

        d_win = jnp.concatenate([d_win_p[:, :P_F + N_HEADS], d_win_p[:, P_C:]], axis=1)
        grads = dict(
            w_ff1_in=d_wi1, w_ff1_out=d_wo1, w_in=_col_shards(d_win, IN_SHARD), w_attn_out=_col_shards(d_wao, D_MODEL // N_DEV),
            conv_w=_col_shards(dconv_w[:CONV_K], CONV_CH // N_DEV), w_conv_out=_col_shards(d_wco, D_MODEL // N_DEV),
            w_out=d_wout, w_ff2_in=d_wi2, w_ff2_out=d_wo2, w_ple_gate=d_wpg, w_ple_proj=_col_shards(d_wpp, D_MODEL // N_DEV))
        recv = _exchange("scatter_grads", [grads[n] for n in BIG], [ROW_BAND[n] for n in BIG], [False] * len(BIG))
        for n, rc in zip(BIG, recv):
            stacked[n] = _adamw("adamw_" + n, rc, W[n], M1[n], V1[n], l, stacked[n])

    cat = {n: jnp.concatenate(small_grads[n], axis=0) for n in small_grads}
    g_pack = _pack_small(cat["g_ff1"], cat["g_mix"], cat["g_ff2"], cat["g_ple"], dg_final, cat["conv_b"], cat["g_conv"],
                         cat["b_f"], loss_row[:, :1] * jnp.ones((1, D_MODEL), F32))
    zero_row = jnp.zeros((1, D_MODEL), F32)
    packs = [_pack_small(*[src[n] for n in ("g_ff1", "g_mix", "g_ff2", "g_ple", "g_final", "conv_b", "g_conv", "b_f")], zero_row)
             for src in (W, M1, V1)]
    (recv_small,) = _exchange("gather_small", [g_pack], [None], [True])
    outs_small = [_unpack_small(a, n_layers) for a in _adamw_small("adamw_small", recv_small, *packs)]

    def pick(kind, n):
        return outs_small[kind][n] if n in SMALL else stacked[n][kind]

    result = [outs_small[0]["loss"], dh.reshape(x.shape)]
    for kind in range(4):
        result += [pick(kind, n) for n in ORDER]
    return tuple(result)
```

```python
import functools
import math

import jax
import jax.numpy as jnp
from jax import lax
from jax.experimental import pallas as pl
from jax.experimental.pallas import tpu as pltpu

F32 = jnp.float32
BF16 = jnp.bfloat16

N_DEV = 8
D_MODEL = 1024
N_HEADS = 8
HEAD_DIM = 64
ATTN_W = N_HEADS * HEAD_DIM
CONV_CH = 512
CONV_K = 31
CONV_HALO = 32
D_FF = 2816
FF_SHARD = 2 * D_FF // N_DEV
D_PLE = 256
EPS = 1e-6
FFN_RES = 0.5
LANES = 128
IN_COLS = 3 * ATTN_W + N_HEADS + 2 * CONV_CH + 2 * D_MODEL
IN_SHARD = IN_COLS // N_DEV
F_PAD = LANES
P_Q, P_K, P_V = 0, ATTN_W, 2 * ATTN_W
P_F = 3 * ATTN_W
P_C = P_F + F_PAD
P_G = P_C + 2 * CONV_CH
IN_PAD = P_G + 2 * D_MODEL
NEG_BIG = -1e30

ADAM_LR, ADAM_B1, ADAM_B2, ADAM_EPS, ADAM_WD, ADAM_STEP = 0.001, 0.9, 0.999, 1e-08, 0.01, 10

VMEM_CAP = 60 * 1024 * 1024
VMEM_SLACK = 12 * 1024 * 1024

NN = (((1,), (0,)), ((), ()))
NT = (((1,), (1,)), ((), ()))
TN = (((0,), (0,)), ((), ()))

MESH = pl.DeviceIdType.MESH
ANY = pl.BlockSpec(memory_space=pl.ANY)


def _nbytes(shape, dtype):
    return math.prod(d for d in shape if d is not None) * jnp.dtype(dtype).itemsize


def _params(block_bytes, n_axes):
    limit = min(VMEM_CAP, 2 * block_bytes + VMEM_SLACK)
    return pltpu.CompilerParams(dimension_semantics=("arbitrary",) * n_axes, vmem_limit_bytes=limit)


def _call(body, name, grid, in_arrays, in_specs, out_shapes, out_specs, scratch=(), aliases=None, extra_bytes=0,
          after=(), prefetch=None):
    total = extra_bytes
    for a, s in zip(in_arrays, in_specs):
        if s.block_shape is not None:
            total += _nbytes(s.block_shape, a.dtype)
    for o, s in zip(out_shapes, out_specs):
        if s.block_shape is not None:
            total += _nbytes(s.block_shape, o.dtype)
    n_in, n_after = len(in_arrays), len(after)
    lead = 0 if prefetch is None else 1

    def with_after(*refs):
        return body(*refs[:lead + n_in], *refs[lead + n_in + n_after:])

    operands = [pltpu.with_memory_space_constraint(a, pltpu.HBM) for a in (*in_arrays, *after)]
    specs = list(in_specs) + [ANY] * n_after
    common = dict(name=name, out_shape=list(out_shapes), input_output_aliases=aliases or {},
                  compiler_params=_params(total, len(grid)))
    if prefetch is None:
        return pl.pallas_call(with_after if n_after else body, grid=grid, in_specs=specs, out_specs=list(out_specs),
                              scratch_shapes=list(scratch), **common)(*operands)
    grid_spec = pltpu.PrefetchScalarGridSpec(num_scalar_prefetch=1, grid=grid, in_specs=specs, out_specs=list(out_specs),
                                             scratch_shapes=list(scratch))
    return pl.pallas_call(with_after if n_after else body, grid_spec=grid_spec, **common)(prefetch, *operands)


def _sig(x):
    return 1.0 / (1.0 + jnp.exp(-x))


def _dot(a, b, dims):
    return lax.dot_general(a.astype(BF16), b.astype(BF16), dims, preferred_element_type=F32)


def _rms_stats(x):
    r = lax.rsqrt(jnp.mean(x * x, axis=-1, keepdims=True) + EPS)
    return r, x * r


def _rms_bwd(dn, x, g):
    r, xh = _rms_stats(x)
    t = dn * g
    dx = r * (t - xh * jnp.mean(t * xh, axis=-1, keepdims=True))
    return dx, jnp.sum(dn * xh, axis=0, keepdims=True)


def _row_tile(m, want):
    t = min(m, want)
    assert m % t == 0
    return t


def _mm(name, a, b, *, grid, a_spec, b_spec, out, o_spec, dims, nk=None, k_axis=None, alpha=1.0,
        res=None, res_spec=None, rms=None, acc_shape=None):
    first_axes = len(grid)

    def body(*refs):
        refs = list(refs)
        a_ref, b_ref = refs[:2]
        pos = 2
        if res is not None:
            res_ref = refs[pos]
            pos += 1
        if rms is not None:
            h_ref, g_ref, dres_ref = refs[pos:pos + 3]
            pos += 3
        o_ref = refs[pos]
        pos += 1
        if rms is not None:
            dg_ref = refs[pos]
            pos += 1
        acc_ref = refs[pos] if k_axis is not None else None
        ids = [pl.program_id(ax) for ax in range(first_axes)]
        row_axes = [ids[ax] == 0 for ax in range(first_axes) if ax != k_axis]
        is_first = functools.reduce(jnp.logical_and, row_axes) if row_axes else None

        part = _dot(a_ref[...], b_ref[...], dims)

        def finish(acc):
            val = acc if alpha == 1.0 else acc * alpha
            if res is not None:
                val = val + res_ref[...]
            if rms is not None:
                dx, dg = _rms_bwd(val, h_ref[...], g_ref[...])
                o_ref[...] = (dres_ref[...] + dx).astype(o_ref.dtype)

                @pl.when(is_first)
                def _():
                    dg_ref[...] = dg

                @pl.when(jnp.logical_not(is_first))
                def _():
                    dg_ref[...] += dg
            else:
                o_ref[...] = val.astype(o_ref.dtype)

        if k_axis is None:
            finish(part)
        else:
            k = ids[k_axis]

            @pl.when(k == 0)
            def _():
                acc_ref[...] = part

            @pl.when(k > 0)
            def _():
                acc_ref[...] += part

            @pl.when(k == nk - 1)
            def _():
                finish(acc_ref[...])

    in_arrays, in_specs = [a, b], [a_spec, b_spec]
    if res is not None:
        in_arrays.append(res)
        in_specs.append(res_spec)
    outs, o_specs = [out], [o_spec]
    if rms is not None:
        h, g, dres, row_spec, g_spec = rms
        in_arrays += [h, g, dres]
        in_specs += [row_spec, g_spec, row_spec]
        outs.append(jax.ShapeDtypeStruct(g.shape, F32))
        o_specs.append(g_spec)
    scratch, extra = [], 0
    if k_axis is not None:
        scratch = [pltpu.VMEM(acc_shape, F32)]
        extra = _nbytes(acc_shape, F32)
    res_out = _call(body, name, grid, in_arrays, in_specs, outs, o_specs, scratch, extra_bytes=extra)
    return res_out if rms is not None else res_out[0]


def _sds(shape, dtype):
    return jax.ShapeDtypeStruct(shape, dtype)


def _rmsnorm_fwd(name, h, g, after=()):
    m, d = h.shape
    tm = _row_tile(m, 512)

    def body(h_ref, g_ref, o_ref):
        _, xh = _rms_stats(h_ref[...])
        o_ref[...] = (xh * g_ref[...]).astype(BF16)

    row = pl.BlockSpec((tm, d), lambda i: (i, 0))
    return _call(body, name, (m // tm,), [h, g], [row, pl.BlockSpec((1, d), lambda i: (0, 0))],
                 [_sds((m, d), BF16)], [row], after=after)[0]


def _swiglu_fwd(name, ab):
    _, m, w = ab.shape
    half = N_DEV // 2
    tm = _row_tile(m, 512)
    ab4 = ab.reshape(2, half, m, w)

    def body(ab_ref, o_ref):
        a, b = ab_ref[0], ab_ref[1]
        o_ref[...] = (a * _sig(a) * b).astype(BF16)

    return _call(body, name, (half, m // tm), [ab4],
                 [pl.BlockSpec((2, None, tm, w), lambda j, i: (0, j, i, 0))],
                 [_sds((half, m, w), BF16)], [pl.BlockSpec((None, tm, w), lambda j, i: (j, i, 0))])[0]


def _swiglu_bwd(name, ab, ds):
    _, m, w = ab.shape
    half = N_DEV // 2
    tm = _row_tile(m, 512)
    ab4 = ab.reshape(2, half, m, w)

    def body(ab_ref, ds_ref, o_ref):
        a, b, d = ab_ref[0], ab_ref[1], ds_ref[...]
        sg = _sig(a)
        o_ref[0] = (d * b * sg * (1.0 + a * (1.0 - sg))).astype(BF16)
        o_ref[1] = (d * a * sg).astype(BF16)

    blk = pl.BlockSpec((2, None, tm, w), lambda j, i: (0, j, i, 0))
    out = _call(body, name, (half, m // tm), [ab4, ds],
                [blk, pl.BlockSpec((None, tm, w), lambda j, i: (j, i, 0))],
                [_sds((2, half, m, w), BF16)], [blk])[0]
    return out.reshape(N_DEV, m, w)


def _inproj(name, u, w_in_p):
    m, d = u.shape
    tm = _row_tile(m, 256)

    def body(u_ref, w_ref, q_ref, k_ref, v_ref, f_ref, c_ref, g_ref):
        z = _dot(u_ref[...], w_ref[...], NN)
        q_ref[...] = z[:, P_Q:P_K].astype(BF16)
        k_ref[...] = z[:, P_K:P_V].astype(BF16)
        v_ref[...] = z[:, P_V:P_F].astype(BF16)
        f_ref[...] = z[:, P_F:P_C]
        c_ref[...] = z[:, P_C:P_G]
        g_ref[...] = z[:, P_G:IN_PAD]

    def rows(width):
        return pl.BlockSpec((tm, width), lambda i: (i, 0))

    widths = (ATTN_W, ATTN_W, ATTN_W, F_PAD, 2 * CONV_CH, 2 * D_MODEL)
    dtypes = (BF16, BF16, BF16, F32, F32, F32)
    return _call(body, name, (m // tm,), [u, w_in_p], [rows(d), pl.BlockSpec((d, IN_PAD), lambda i: (0, 0))],
                 [_sds((m, wd), dt) for wd, dt in zip(widths, dtypes)], [rows(wd) for wd in widths])


def _cumsum_rows(x, reverse):
    m = x.shape[0]
    row = lax.broadcasted_iota(jnp.int32, x.shape, 0)
    sh = 1
    while sh < m:
        if reverse:
            x = x + jnp.where(row < m - sh, pltpu.roll(x, m - sh, axis=0), 0.0)
        else:
            x = x + jnp.where(row >= sh, pltpu.roll(x, sh, axis=0), 0.0)
        sh *= 2
    return x


def _log_sigmoid(x):
    return jnp.minimum(x, 0.0) - jnp.log(1.0 + jnp.exp(-jnp.abs(x)))


def _fgate_fwd(name, zf, bf):
    m, w = zf.shape

    def body(z_ref, b_ref, c_ref):
        c_ref[...] = _cumsum_rows(_log_sigmoid(z_ref[...] + b_ref[...]), reverse=False)

    full = pl.BlockSpec((m, w), lambda i: (0, 0))
    return _call(body, name, (1,), [zf, bf], [full, pl.BlockSpec((1, w), lambda i: (0, 0))], [_sds((m, w), F32)], [full])[0]


def _fgate_bwd(name, dc, zf, bf):
    m, w = zf.shape

    def body(dc_ref, z_ref, b_ref, dz_ref, db_ref):
        dls = _cumsum_rows(dc_ref[...], reverse=True)
        dz = dls * _sig(-(z_ref[...] + b_ref[...]))
        lane = lax.broadcasted_iota(jnp.int32, dz.shape, 1)
        dz = jnp.where(lane < N_HEADS, dz, 0.0)
        dz_ref[...] = dz.astype(BF16)
        db_ref[...] = jnp.sum(dz, axis=0, keepdims=True)

    full = pl.BlockSpec((m, w), lambda i: (0, 0))
    one = pl.BlockSpec((1, w), lambda i: (0, 0))
    return _call(body, name, (1,), [dc, zf, bf], [full, full, one], [_sds((m, w), BF16), _sds((1, w), F32)], [full, one])


def _lane_pick(x, idx):
    lane = lax.broadcasted_iota(jnp.int32, x.shape, 1)
    return jnp.sum(jnp.where(lane == idx, x, 0.0), axis=1, keepdims=True)


def _row_pick(x, idx):
    sub = lax.broadcasted_iota(jnp.int32, x.shape, 0)
    return jnp.sum(jnp.where(sub == idx, x, 0.0), axis=0, keepdims=True)


def _attn_fwd(name, q, k, v, c, ct, t):
    m = q.shape[0]
    n_chunks = m // t
    pairs = N_HEADS // 2
    scale = 1.0 / math.sqrt(HEAD_DIM)

    def body(q_ref, k_ref, v_ref, c_ref, ct_ref, o_ref, lse_ref):
        p = pl.program_id(0)
        i = pl.program_id(1)
        lane = lax.broadcasted_iota(jnp.int32, (t, LANES), 1)
        first = lane < HEAD_DIM
        q2 = q_ref[...]
        zero = jnp.zeros_like(q2)
        q_heads = (jnp.where(first, q2, zero), jnp.where(first, zero, q2))
        cblk = c_ref[...]
        c_cols = (_lane_pick(cblk, 2 * p), _lane_pick(cblk, 2 * p + 1))
        rows = i * t + lax.broadcasted_iota(jnp.int32, (t, t), 0)

        def step(j, carry):
            off = pl.multiple_of(j * t, t)
            kj = k_ref[pl.ds(off, t), :]
            vj = v_ref[pl.ds(off, t), :]
            ctj = ct_ref[j]
            visible = rows >= off + lax.broadcasted_iota(jnp.int32, (t, t), 1)
            new = []
            for e in range(2):
                mx, den, acc = carry[3 * e:3 * e + 3]
                s = _dot(q_heads[e], kj, NT) * scale + (c_cols[e] - _row_pick(ctj, 2 * p + e))
                s = jnp.where(visible, s, NEG_BIG)
                mx2 = jnp.maximum(mx, jnp.max(s, axis=1, keepdims=True))
                corr = jnp.exp(mx - mx2)
                pe = jnp.exp(s - mx2)
                new += [mx2, corr * den + jnp.sum(pe, axis=1, keepdims=True), corr * acc + _dot(pe, vj, NN)]
            return tuple(new)

        col = jnp.full((t, 1), NEG_BIG, F32), jnp.zeros((t, 1), F32), jnp.zeros((t, LANES), F32)
        fin = lax.fori_loop(0, i + 1, step, col + col)
        o_ref[...] = jnp.where(first, fin[2] / fin[1], fin[5] / fin[4]).astype(BF16)
        lse_a = fin[0] + jnp.log(fin[1])
        lse_b = fin[3] + jnp.log(fin[4])
        lse_ref[...] = jnp.where(lane == 0, lse_a, jnp.where(lane == 1, lse_b, 0.0))

    seq = pl.BlockSpec((m, LANES), lambda p, i: (0, p))
    blk = pl.BlockSpec((t, LANES), lambda p, i: (i, p))
    return _call(body, name, (pairs, n_chunks), [q, k, v, c, ct],
                 [blk, seq, seq, pl.BlockSpec((t, LANES), lambda p, i: (i, 0)),
                  pl.BlockSpec((n_chunks, 8, t), lambda p, i: (0, 0, 0))],
                 [_sds((m, ATTN_W), BF16), _sds((pairs, m, LANES), F32)],
                 [blk, pl.BlockSpec((None, t, LANES), lambda p, i: (p, i, 0))])


def _attn_bwd(name, q, k, v, c, ct, o, lse, do, t):
    m = q.shape[0]
    n_chunks = m // t
    pairs = N_HEADS // 2
    scale = 1.0 / math.sqrt(HEAD_DIM)

    def body(q_ref, do_ref, o_ref, lse_ref, c_ref, ct_ref, k_ref, v_ref, dq_ref, dk_ref, dv_ref, dcs_ref, drs_ref):
        p = pl.program_id(0)
        j = pl.program_id(1)

        @pl.when(j == 0)
        def _():
            dq_ref[...] = jnp.zeros_like(dq_ref)
            drs_ref[...] = jnp.zeros_like(drs_ref)

        lane = lax.broadcasted_iota(jnp.int32, (t, LANES), 1)
        first = lane < HEAD_DIM
        kj, vj = k_ref[...], v_ref[...]
        zero = jnp.zeros_like(kj)
        k_heads = (jnp.where(first, kj, zero), jnp.where(first, zero, kj))
        v_heads = (jnp.where(first, vj, zero), jnp.where(first, zero, vj))
        ctj = ct_ref[j]
        c_rows = (_row_pick(ctj, 2 * p), _row_pick(ctj, 2 * p + 1))
        cols = j * t + lax.broadcasted_iota(jnp.int32, (t, t), 1)

        def step(i, carry):
            off = pl.multiple_of(i * t, t)
            qi = q_ref[pl.ds(off, t), :]
            doi = do_ref[pl.ds(off, t), :]
            prod = doi * o_ref[pl.ds(off, t), :].astype(F32)
            lsei = lse_ref[pl.ds(off, t), :]
            ci = c_ref[pl.ds(off, t), :]
            dob = doi.astype(BF16)
            visible = off + lax.broadcasted_iota(jnp.int32, (t, t), 0) >= cols
            new = []
            dq = jnp.zeros((t, LANES), F32)
            drow = jnp.zeros((t, LANES), F32)
            for e in range(2):
                dk, dv, dcol = carry[3 * e:3 * e + 3]
                delta = jnp.sum(jnp.where(first if e == 0 else jnp.logical_not(first), prod, 0.0), axis=1, keepdims=True)
                s = _dot(qi, k_heads[e], NT) * scale + (_lane_pick(ci, 2 * p + e) - c_rows[e])
                pe = jnp.where(visible, jnp.exp(s - _lane_pick(lsei, e)), 0.0)
                dsc = pe * (_dot(dob, v_heads[e], NT) - delta)
                dsb = dsc.astype(BF16)
                dq = dq + _dot(dsb, k_heads[e], NN)
                drow = jnp.where(lane == e, jnp.sum(dsc, axis=1, keepdims=True), drow)
                new += [dk + _dot(dsb, qi, TN), dv + _dot(pe, dob, TN), dcol + jnp.sum(dsc, axis=0, keepdims=True)]
            dq_ref[pl.ds(off, t), :] += dq * scale
            drs_ref[pl.ds(off, t), :] += drow
            return tuple(new)

        z = jnp.zeros((t, LANES), F32), jnp.zeros((t, LANES), F32), jnp.zeros((1, t), F32)
        fin = lax.fori_loop(j, n_chunks, step, z + z)
        dk_ref[...] = jnp.where(first, fin[0], fin[3]) * scale
        dv_ref[...] = jnp.where(first, fin[1], fin[4])
        sub = lax.broadcasted_iota(jnp.int32, (8, t), 0)
        dcs_ref[...] = jnp.where(sub == 0, fin[2], jnp.where(sub == 1, fin[5], 0.0))

    seq = pl.BlockSpec((m, LANES), lambda p, j: (0, p))
    blk = pl.BlockSpec((t, LANES), lambda p, j: (j, p))
    return _call(body, name, (pairs, n_chunks), [q, do, o, lse, c, ct, k, v],
                 [seq, seq, seq, pl.BlockSpec((None, m, LANES), lambda p, j: (p, 0, 0)),
                  pl.BlockSpec((m, LANES), lambda p, j: (0, 0)),
                  pl.BlockSpec((n_chunks, 8, t), lambda p, j: (0, 0, 0)), blk, blk],
                 [_sds((m, ATTN_W), F32), _sds((m, ATTN_W), F32), _sds((m, ATTN_W), F32),
                  _sds((pairs, n_chunks, 8, t), F32), _sds((pairs, m, LANES), F32)],
                 [seq, blk, blk, pl.BlockSpec((None, None, 8, t), lambda p, j: (p, j, 0, 0)),
                  pl.BlockSpec((None, m, LANES), lambda p, j: (p, 0, 0))])


def _glu(z):
    return z[:, :CONV_CH] * _sig(z[:, CONV_CH:])


def _shifted(x, lead, tm):
    n = x.shape[0]
    return pltpu.roll(x, (n - lead) % n, axis=0)[:tm]


def _conv_fwd(name, zc, w, b, g, tm):
    m = zc.shape[0]
    hb = tm // CONV_HALO

    def body(cur_ref, prev_ref, w_ref, b_ref, g_ref, cv_ref, y_ref):
        i = pl.program_id(0)
        a_prev = jnp.where(i > 0, _glu(prev_ref[...]), 0.0)
        af = jnp.concatenate([a_prev, _glu(cur_ref[...])], axis=0)
        y = jnp.zeros((tm, CONV_CH), F32)
        for tap in range(CONV_K):
            y = y + w_ref[pl.ds(tap, 1), :] * _shifted(af, CONV_HALO - (CONV_K - 1) + tap, tm)
        y = y + b_ref[...]
        y_ref[...] = y
        _, xh = _rms_stats(y)
        rn = xh * g_ref[...]
        cv_ref[...] = (rn * _sig(rn)).astype(BF16)

    one = pl.BlockSpec((1, CONV_CH), lambda i: (0, 0))
    out = pl.BlockSpec((tm, CONV_CH), lambda i: (i, 0))
    return _call(body, name, (m // tm,), [zc, zc, w, b, g],
                 [pl.BlockSpec((tm, 2 * CONV_CH), lambda i: (i, 0)),
                  pl.BlockSpec((CONV_HALO, 2 * CONV_CH), lambda i: (jnp.maximum(i * hb - 1, 0), 0)),
                  pl.BlockSpec((CONV_HALO, CONV_CH), lambda i: (0, 0)), one, one],
                 [_sds((m, CONV_CH), BF16), _sds((m, CONV_CH), F32)], [out, out])


def _conv_bwd(name, dcv, y, zc, w, g, tm):
    m = zc.shape[0]
    hb = tm // CONV_HALO
    n_blocks = m // tm

    def body(dcv_ref, dcvn_ref, y_ref, yn_ref, cur_ref, prev_ref, w_ref, g_ref, dz_ref, dw_ref, db_ref, dg_ref):
        i = pl.program_id(0)
        gv = g_ref[...]

        def dy_of(d, yv):
            r, xh = _rms_stats(yv)
            rn = xh * gv
            sg = _sig(rn)
            drn = d * sg * (1.0 + rn * (1.0 - sg))
            tt = drn * gv
            return r * (tt - xh * jnp.mean(tt * xh, axis=-1, keepdims=True)), drn * xh

        dy, dgt = dy_of(dcv_ref[...], y_ref[...])
        dy_next, _ = dy_of(dcvn_ref[...], yn_ref[...])
        dyf = jnp.concatenate([dy, jnp.where(i < n_blocks - 1, dy_next, 0.0)], axis=0)
        cur = cur_ref[...]
        af = jnp.concatenate([jnp.where(i > 0, _glu(prev_ref[...]), 0.0), _glu(cur)], axis=0)

        @pl.when(i == 0)
        def _():
            dw_ref[...] = jnp.zeros_like(dw_ref)
            db_ref[...] = jnp.zeros_like(db_ref)
            dg_ref[...] = jnp.zeros_like(dg_ref)

        da = jnp.zeros((tm, CONV_CH), F32)
        for tap in range(CONV_K):
            da = da + w_ref[pl.ds(tap, 1), :] * _shifted(dyf, CONV_K - 1 - tap, tm)
            a_tap = _shifted(af, CONV_HALO - (CONV_K - 1) + tap, tm)
            dw_ref[pl.ds(tap, 1), :] += jnp.sum(dy * a_tap, axis=0, keepdims=True)
        db_ref[...] += jnp.sum(dy, axis=0, keepdims=True)
        dg_ref[...] += jnp.sum(dgt, axis=0, keepdims=True)
        c1, sg2 = cur[:, :CONV_CH], _sig(cur[:, CONV_CH:])
        dz_ref[:, :CONV_CH] = (da * sg2).astype(BF16)
        dz_ref[:, CONV_CH:] = (da * c1 * sg2 * (1.0 - sg2)).astype(BF16)

    one = pl.BlockSpec((1, CONV_CH), lambda i: (0, 0))
    taps = pl.BlockSpec((CONV_HALO, CONV_CH), lambda i: (0, 0))
    row = pl.BlockSpec((tm, CONV_CH), lambda i: (i, 0))
    nxt = pl.BlockSpec((CONV_HALO, CONV_CH), lambda i: (jnp.minimum((i + 1) * hb, m // CONV_HALO - 1), 0))
    row2 = pl.BlockSpec((tm, 2 * CONV_CH), lambda i: (i, 0))
    return _call(body, name, (n_blocks,), [dcv, dcv, y, y, zc, zc, w, g],
                 [row, nxt, row, nxt, row2,
                  pl.BlockSpec((CONV_HALO, 2 * CONV_CH), lambda i: (jnp.maximum(i * hb - 1, 0), 0)), taps, one],
                 [_sds((m, 2 * CONV_CH), BF16), _sds((CONV_HALO, CONV_CH), F32), _sds((1, CONV_CH), F32), _sds((1, CONV_CH), F32)],
                 [row2, taps, one, one])


def _mixout_fwd(name, o, cv, zg, h, w_ao, w_co, w_out, tm):
    m, d = h.shape

    def body(o_ref, cv_ref, zg_ref, h_ref, wa_ref, wc_ref, wo_ref, hn_ref, mg_ref, ya_ref, yc_ref):
        ya = _dot(o_ref[...], wa_ref[...], NN)
        yc = _dot(cv_ref[...], wc_ref[...], NN)
        zg_v = zg_ref[...]
        mg = (_sig(zg_v[:, :d]) * ya + _sig(zg_v[:, d:]) * yc).astype(BF16)
        ya_ref[...] = ya
        yc_ref[...] = yc
        mg_ref[...] = mg
        hn_ref[...] = h_ref[...] + _dot(mg, wo_ref[...], NN)

    def rows(wd):
        return pl.BlockSpec((tm, wd), lambda i: (i, 0))

    def whole(a):
        return pl.BlockSpec(a.shape, lambda i: (0, 0))

    return _call(body, name, (m // tm,), [o, cv, zg, h, w_ao, w_co, w_out],
                 [rows(ATTN_W), rows(CONV_CH), rows(2 * d), rows(d), whole(w_ao), whole(w_co), whole(w_out)],
                 [_sds((m, d), F32), _sds((m, d), BF16), _sds((m, d), F32), _sds((m, d), F32)],
                 [rows(d), rows(d), rows(d), rows(d)])


def _mixout_bwd(name, dh, zg, ya, yc, w_ao, w_co, w_out, tm):
    m, d = dh.shape

    def body(dh_ref, zg_ref, ya_ref, yc_ref, wa_ref, wc_ref, wo_ref, do_ref, dcv_ref, dzg_ref, dya_ref, dyc_ref):
        dm = _dot(dh_ref[...], wo_ref[...], NT)
        zg_v = zg_ref[...]
        sa, sc = _sig(zg_v[:, :d]), _sig(zg_v[:, d:])
        dya = (dm * sa).astype(BF16)
        dyc = (dm * sc).astype(BF16)
        dzg_ref[:, :d] = (dm * ya_ref[...] * sa * (1.0 - sa)).astype(BF16)
        dzg_ref[:, d:] = (dm * yc_ref[...] * sc * (1.0 - sc)).astype(BF16)
        dya_ref[...] = dya
        dyc_ref[...] = dyc
        do_ref[...] = _dot(dya, wa_ref[...], NT)
        dcv_ref[...] = _dot(dyc, wc_ref[...], NT)

    def rows(wd):
        return pl.BlockSpec((tm, wd), lambda i: (i, 0))

    def whole(a):
        return pl.BlockSpec(a.shape, lambda i: (0, 0))

    return _call(body, name, (m // tm,), [dh, zg, ya, yc, w_ao, w_co, w_out],
                 [rows(d), rows(2 * d), rows(d), rows(d), whole(w_ao), whole(w_co), whole(w_out)],
                 [_sds((m, ATTN_W), F32), _sds((m, CONV_CH), F32), _sds((m, 2 * d), BF16), _sds((m, d), BF16), _sds((m, d), BF16)],
                 [rows(ATTN_W), rows(CONV_CH), rows(2 * d), rows(d), rows(d)])


def _ple_fwd(name, h, p, g, w_gate, w_proj, tm):
    m, d = h.shape

    def body(h_ref, p_ref, g_ref, wg_ref, wp_ref, hn_ref, n_ref, gl_ref, pp_ref):
        hv = h_ref[...]
        _, xh = _rms_stats(hv)
        n = (xh * g_ref[...]).astype(BF16)
        gl = _dot(n, wg_ref[...], NN)
        pp = _dot(p_ref[...], wp_ref[...], NN)
        n_ref[...] = n
        gl_ref[...] = gl
        pp_ref[...] = pp
        hn_ref[...] = hv + _sig(gl) * pp

    def rows(wd):
        return pl.BlockSpec((tm, wd), lambda i: (i, 0))

    return _call(body, name, (m // tm,), [h, p, g, w_gate, w_proj],
                 [rows(d), rows(D_PLE), pl.BlockSpec((1, d), lambda i: (0, 0)),
                  pl.BlockSpec((d, d), lambda i: (0, 0)), pl.BlockSpec((D_PLE, d), lambda i: (0, 0))],
                 [_sds((m, d), F32), _sds((m, d), BF16), _sds((m, d), F32), _sds((m, d), F32)],
                 [rows(d)] * 4)


def _ple_bwd(name, dh, h, g, gl, pp, w_gate, tm, after=()):
    m, d = h.shape

    def body(dh_ref, h_ref, g_ref, gl_ref, pp_ref, wg_ref, dhn_ref, dgl_ref, dpp_ref, dg_ref):
        dhv = dh_ref[...]
        sg = _sig(gl_ref[...])
        dgl = (dhv * pp_ref[...] * sg * (1.0 - sg)).astype(BF16)
        dgl_ref[...] = dgl
        dpp_ref[...] = (dhv * sg).astype(BF16)
        dx, dg = _rms_bwd(_dot(dgl, wg_ref[...], NT), h_ref[...], g_ref[...])
        dhn_ref[...] = dhv + dx

        @pl.when(pl.program_id(0) == 0)
        def _():
            dg_ref[...] = dg

        @pl.when(pl.program_id(0) > 0)
        def _():
            dg_ref[...] += dg

    rows = pl.BlockSpec((tm, d), lambda i: (i, 0))
    one = pl.BlockSpec((1, d), lambda i: (0, 0))
    return _call(body, name, (m // tm,), [dh, h, g, gl, pp, w_gate],
                 [rows, rows, one, rows, rows, pl.BlockSpec((d, d), lambda i: (0, 0))],
                 [_sds((m, d), F32), _sds((m, d), BF16), _sds((m, d), BF16), _sds((1, d), F32)],
                 [rows, rows, rows, one], after=after)


def _loss_head(name, h, g, target, tm):
    m, d = h.shape

    def body(h_ref, g_ref, t_ref, loss_ref, dh_ref, dg_ref):
        hv, gv = h_ref[...], g_ref[...]
        r, xh = _rms_stats(hv)
        err = xh * gv - t_ref[...]
        dy = err * (1.0 / d)
        tt = dy * gv
        dh_ref[...] = r * (tt - xh * jnp.mean(tt * xh, axis=-1, keepdims=True))
        dg = jnp.sum(dy * xh, axis=0, keepdims=True)
        part = jnp.zeros((1, LANES), F32) + 0.5 * jnp.sum(jnp.sum(err * err, axis=1, keepdims=True), axis=0, keepdims=True) * (1.0 / d)

        @pl.when(pl.program_id(0) == 0)
        def _():
            dg_ref[...] = dg
            loss_ref[...] = part

        @pl.when(pl.program_id(0) > 0)
        def _():
            dg_ref[...] += dg
            loss_ref[...] += part

    rows = pl.BlockSpec((tm, d), lambda i: (i, 0))
    one = pl.BlockSpec((1, d), lambda i: (0, 0))
    return _call(body, name, (m // tm,), [h, g, target], [rows, one, rows],
                 [_sds((1, LANES), F32), _sds((m, d), F32), _sds((1, d), F32)],
                 [pl.BlockSpec((1, LANES), lambda i: (0, 0)), rows, one])


def _place():
    return lax.axis_index("x"), lax.axis_index("y"), lax.axis_index("c")


def _linear(px, py, pc):
    return 4 * px + 2 * py + pc


def _block_of(ref, idx, rows_per_block):
    if rows_per_block is None:
        return ref.at[idx]
    return ref.at[pl.ds(pl.multiple_of(idx * rows_per_block, 16), rows_per_block), :]


FLIPS = tuple((dx, dy, dc) for dx in (0, 1) for dy in (0, 1) for dc in (0, 1))[1:]
N_PEERS = len(FLIPS)
HBM = pl.BlockSpec(memory_space=pltpu.HBM)
SEM = pl.BlockSpec(memory_space=pltpu.SEMAPHORE)
SIDE_EFFECT = pltpu.SideEffectType.DATAFLOW_SIDE_EFFECTING


def _peer(x, y, c, flip):
    return (1 - x if flip[0] else x, 1 - y if flip[1] else y, 1 - c if flip[2] else c)


def _split_copies(scatter, srcs, lands, bands, send_sems, recv_sems):
    x, y, c = _place()
    my_idx = _linear(x, y, c)
    copies = []
    for t in range(len(srcs)):
        for k, flip in enumerate(FLIPS):
            peer = _peer(x, y, c, flip)
            if scatter:
                src, dst = _block_of(srcs[t], _linear(*peer), bands[t]), lands[t].at[k]
            else:
                src, dst = srcs[t], _block_of(lands[t], my_idx, bands[t])
            copies.append(pltpu.make_async_remote_copy(
                src_ref=src, dst_ref=dst, send_sem=send_sems.at[t * N_PEERS + k], recv_sem=recv_sems.at[t * N_PEERS + k],
                device_id=peer, device_id_type=MESH))
    return copies


def _split_start(name, scatter, srcs, lands, bands, follows):
    n = len(srcs)

    def body(*refs):
        send_sems, recv_sems = refs[2 * n + 1], refs[2 * n + 2]
        for cp in _split_copies(scatter, refs[:n], refs[n:2 * n], bands, send_sems, recv_sems):
            cp.start()
        token = refs[-1]
        token[...] = jnp.zeros_like(token)

    pinned = [pltpu.with_memory_space_constraint(a, pltpu.HBM) for a in (*srcs, *lands)]
    outs = pl.pallas_call(
        body, name=name,
        out_shape=(pltpu.SemaphoreType.DMA((n * N_PEERS,)), pltpu.SemaphoreType.DMA((n * N_PEERS,)),
                   *[pltpu.HBM(a.shape, a.dtype) for a in pinned], _sds((8, LANES), F32)),
        in_specs=[HBM] * (2 * n) + [ANY], out_specs=(SEM, SEM, *[HBM] * (2 * n), pl.BlockSpec(memory_space=pltpu.VMEM)),
        input_output_aliases={i: 2 + i for i in range(2 * n)},
        compiler_params=pltpu.CompilerParams(has_side_effects=SIDE_EFFECT),
    )(*pinned, follows)
    return outs[0], outs[1], outs[2:2 + n], outs[2 + n:2 + 2 * n], outs[-1]


def _split_wait(name, scatter, started, bands, follows):
    send_sems, recv_sems, srcs, lands, _ = started
    n = len(srcs)

    def body(*refs):
        for cp in _split_copies(scatter, refs[:n], refs[n:2 * n], bands, refs[2 * n], refs[2 * n + 1]):
            cp.wait_send()
            cp.wait_recv()

    outs = pl.pallas_call(
        body, name=name, out_shape=tuple(pltpu.HBM(a.shape, a.dtype) for a in (*srcs, *lands)),
        in_specs=[HBM] * (2 * n) + [SEM, SEM, ANY], out_specs=[HBM] * (2 * n),
        input_output_aliases={i: i for i in range(2 * n)},
        compiler_params=pltpu.CompilerParams(has_side_effects=SIDE_EFFECT),
    )(*srcs, *lands, send_sems, recv_sems, follows)
    return outs[:n], outs[n:]


def _own_block_filled(shard, band, my_idx):
    if band is None:
        zone = lax.empty((N_DEV,) + shard.shape, shard.dtype)
        return lax.dynamic_update_slice(zone, shard[None], (my_idx,) + (0,) * shard.ndim)
    zone = lax.empty((N_DEV * band,) + shard.shape[1:], shard.dtype)
    return lax.dynamic_update_slice(zone, shard, (my_idx * band,) + (0,) * (shard.ndim - 1))


def _exchange(name, grads, row_bands, gathered):
    n_items = len(grads)
    outs = []
    for gr, band, whole in zip(grads, row_bands, gathered):
        if whole:
            blk = gr.shape
        elif band is None:
            blk = gr.shape[1:]
        else:
            blk = (band,) + gr.shape[1:]
        outs.append(_sds((N_DEV,) + blk, gr.dtype))
    flips = [(dx, dy, dc) for dx in (0, 1) for dy in (0, 1) for dc in (0, 1)][1:]

    def body(*refs):
        srcs, dsts = refs[:n_items], refs[n_items:2 * n_items]
        send_sems, recv_sems, local_sems = refs[2 * n_items:]
        x, y, c = _place()
        my_idx = _linear(x, y, c)

        def src_of(t, idx):
            return srcs[t] if gathered[t] else _block_of(srcs[t], idx, row_bands[t])

        copies = []
        for t in range(n_items):
            local = pltpu.make_async_copy(src_of(t, my_idx), dsts[t].at[my_idx], local_sems.at[t])
            local.start()
            copies.append(local)
            for k, (dx, dy, dc) in enumerate(flips):
                peer = (1 - x if dx else x, 1 - y if dy else y, 1 - c if dc else c)
                cp = pltpu.make_async_remote_copy(
                    src_ref=src_of(t, _linear(*peer)), dst_ref=dsts[t].at[my_idx],
                    send_sem=send_sems.at[t, k], recv_sem=recv_sems.at[t, k], device_id=peer, device_id_type=MESH)
                cp.start()
                copies.append(cp)
        for cp in copies:
            cp.wait()

    return pl.pallas_call(
        body, name=name, in_specs=[ANY] * n_items, out_specs=[ANY] * n_items, out_shape=outs,
        scratch_shapes=[pltpu.SemaphoreType.DMA((n_items, 7)), pltpu.SemaphoreType.DMA((n_items, 7)),
                        pltpu.SemaphoreType.DMA((n_items,))],
    )(*grads)


def _adam_math(g, w, m, v):
    m2 = ADAM_B1 * m + (1.0 - ADAM_B1) * g
    v2 = ADAM_B2 * v + (1.0 - ADAM_B2) * (g * g)
    m_hat = m2 / (1.0 - ADAM_B1 ** ADAM_STEP)
    v_hat = v2 / (1.0 - ADAM_B2 ** ADAM_STEP)
    return -ADAM_LR * (m_hat / (jnp.sqrt(v_hat) + ADAM_EPS) + ADAM_WD * w), m2, v2


def _adamw(name, recv, grads, band, my_idx, w, m, v, layer, prev):
    n_layers, r, c = w.shape
    tr = r
    for cand in (256, 176, 128):
        if r > cand and r % cand == 0:
            tr = cand
            break
    steps = r // tr

    def body(idx_ref, recv_ref, own_ref, w_ref, m_ref, v_ref, *rest):
        g_ref, d_ref, m2_ref, v2_ref = rest[-4:]
        g = own_ref[...].astype(F32)
        for s in range(N_PEERS):
            g = g + recv_ref[s].astype(F32)
        delta, m2, v2 = _adam_math(g, w_ref[...], m_ref[...], v_ref[...])
        g_ref[...] = g
        d_ref[...] = delta
        m2_ref[...] = m2
        v2_ref[...] = v2

    blk = pl.BlockSpec((None, tr, c), lambda i, idx: (layer, i, 0))
    if band is None:
        own = pl.BlockSpec((None, tr, c), lambda i, idx: (idx[0], i, 0))
    else:
        own = pl.BlockSpec((tr, c), lambda i, idx: (idx[0] * steps + i, 0))
    ins = [recv, grads, w, m, v]
    specs = [pl.BlockSpec((N_PEERS, tr, c), lambda i, idx: (0, i, 0)), own, blk, blk, blk]
    aliases = {}
    if prev is not None:
        ins += list(prev)
        specs += [ANY] * 4
        aliases = {1 + len(ins) - 4 + n: n for n in range(4)}
    return _call(body, name, (steps,), ins, specs, [_sds(w.shape, F32)] * 4, [blk] * 4, aliases=aliases, prefetch=my_idx)


def _adamw_small(name, recv, w, m, v):
    r, c = w.shape

    def body(recv_ref, w_ref, m_ref, v_ref, g_ref, d_ref, m2_ref, v2_ref):
        g = recv_ref[0]
        for s in range(1, N_DEV):
            g = g + recv_ref[s]
        delta, m2, v2 = _adam_math(g, w_ref[...], m_ref[...], v_ref[...])
        g_ref[...] = g
        d_ref[...] = delta
        m2_ref[...] = m2
        v2_ref[...] = v2

    full = pl.BlockSpec((r, c), lambda i: (0, 0))
    return _call(body, name, (1,), [recv, w, m, v], [pl.BlockSpec((N_DEV, r, c), lambda i: (0, 0, 0)), full, full, full],
                 [_sds((r, c), F32)] * 4, [full] * 4)


def _ffn_fwd(tag, h, g, wi, wo, after=()):
    m, d = h.shape
    tm = _row_tile(m, 512)
    n = _rmsnorm_fwd(tag + "_norm", h, g, after)
    ab = _mm(tag + "_in", n, wi, grid=(N_DEV, m // tm), dims=NT,
             a_spec=pl.BlockSpec((tm, d), lambda s, i: (i, 0)), b_spec=pl.BlockSpec((None, FF_SHARD, d), lambda s, i: (s, 0, 0)),
             out=_sds((N_DEV, m, FF_SHARD), F32), o_spec=pl.BlockSpec((None, tm, FF_SHARD), lambda s, i: (s, i, 0)))
    act = _swiglu_fwd(tag + "_act", ab)
    nk = N_DEV // 2
    row = pl.BlockSpec((tm, d), lambda i, k: (i, 0))
    h2 = _mm(tag + "_out", act, wo, grid=(m // tm, nk), dims=NN, nk=nk, k_axis=1, alpha=FFN_RES,
             a_spec=pl.BlockSpec((None, tm, FF_SHARD), lambda i, k: (k, i, 0)), b_spec=pl.BlockSpec((FF_SHARD, d), lambda i, k: (k, 0)),
             out=_sds((m, d), F32), o_spec=row, res=h, res_spec=row, acc_shape=(tm, d))
    return h2, (h, n, ab, act)


def _ffn_bwd(tag, dh, saved, g, wi, wo):
    h, n, ab, act = saved
    m, d = h.shape
    tm = _row_tile(m, 512)
    nk = N_DEV // 2
    ds = _mm(tag + "_dact", dh, wo, grid=(nk, m // tm), dims=NT, alpha=FFN_RES,
             a_spec=pl.BlockSpec((tm, d), lambda j, i: (i, 0)), b_spec=pl.BlockSpec((FF_SHARD, d), lambda j, i: (j, 0)),
             out=_sds((nk, m, FF_SHARD), F32), o_spec=pl.BlockSpec((None, tm, FF_SHARD), lambda j, i: (j, i, 0)))
    dwo = _mm(tag + "_dwo", act, dh, grid=(nk, m // tm), dims=TN, nk=m // tm, k_axis=1, alpha=FFN_RES,
              a_spec=pl.BlockSpec((None, tm, FF_SHARD), lambda j, k: (j, k, 0)), b_spec=pl.BlockSpec((tm, d), lambda j, k: (k, 0)),
              out=_sds((D_FF, d), BF16), o_spec=pl.BlockSpec((FF_SHARD, d), lambda j, k: (j, 0)), acc_shape=(FF_SHARD, d))
    dab = _swiglu_bwd(tag + "_dab", ab, ds)
    row = pl.BlockSpec((tm, d), lambda i, k: (i, 0))
    one = pl.BlockSpec((1, d), lambda i, k: (0, 0))
    dh2, dg = _mm(tag + "_dn", dab, wi, grid=(m // tm, N_DEV), dims=NN, nk=N_DEV, k_axis=1,
                  a_spec=pl.BlockSpec((None, tm, FF_SHARD), lambda i, k: (k, i, 0)),
                  b_spec=pl.BlockSpec((None, FF_SHARD, d), lambda i, k: (k, 0, 0)),
                  out=_sds((m, d), F32), o_spec=row, rms=(h, g, dh, row, one), acc_shape=(tm, d))
    dwi = _mm(tag + "_dwi", dab, n, grid=(N_DEV, m // tm), dims=TN, nk=m // tm, k_axis=1,
              a_spec=pl.BlockSpec((None, tm, FF_SHARD), lambda s, k: (s, k, 0)), b_spec=pl.BlockSpec((tm, d), lambda s, k: (k, 0)),
              out=_sds((N_DEV, FF_SHARD, d), BF16), o_spec=pl.BlockSpec((None, FF_SHARD, d), lambda s, k: (s, 0, 0)),
              acc_shape=(FF_SHARD, d))
    return dh2, dg, dwi, dwo


def _wgrad(name, a, b, tk):
    m = a.shape[0]
    ka, kb = a.shape[1], b.shape[1]
    return _mm(name, a, b, grid=(m // tk,), dims=TN, nk=m // tk, k_axis=0,
               a_spec=pl.BlockSpec((tk, ka), lambda k: (k, 0)), b_spec=pl.BlockSpec((tk, kb), lambda k: (k, 0)),
               out=_sds((ka, kb), BF16), o_spec=pl.BlockSpec((ka, kb), lambda k: (0, 0)), acc_shape=(ka, kb))


def _mixer_fwd(tag, h, g, w_in_p, bf, conv_w, conv_b, g_conv, w_ao, w_co, w_out):
    m, d = h.shape
    t = _row_tile(m, 256)
    u = _rmsnorm_fwd(tag + "_norm", h, g)
    q, k, v, zf, zc, zg = _inproj(tag + "_inproj", u, w_in_p)
    c = _fgate_fwd(tag + "_fgate", zf, bf)
    ct = c[:, :N_HEADS].T.reshape(N_HEADS, m // t, t).transpose(1, 0, 2)
    o, lse = _attn_fwd(tag + "_attn", q, k, v, c, ct, t)
    cv, y = _conv_fwd(tag + "_conv", zc, conv_w, conv_b, g_conv, t)
    h2, mg, ya, yc = _mixout_fwd(tag + "_mixout", o, cv, zg, h, w_ao, w_co, w_out, t)
    return h2, (h, u, q, k, v, zf, zc, zg, c, ct, o, lse, cv, y, mg, ya, yc)


def _mixer_bwd(tag, dh, saved, g, w_in_p, bf, conv_w, g_conv, w_ao, w_co, w_out):
    h, u, q, k, v, zf, zc, zg, c, ct, o, lse, cv, y, mg, ya, yc = saved
    m, d = h.shape
    t = _row_tile(m, 256)
    tk = _row_tile(m, 512)
    do, dcv, dzg, dya, dyc = _mixout_bwd(tag + "_dmixout", dh, zg, ya, yc, w_ao, w_co, w_out, t)
    d_wout = _wgrad(tag + "_dwout", mg, dh, tk)
    d_wao = _wgrad(tag + "_dwao", o, dya, tk)
    d_wco = _wgrad(tag + "_dwco", cv, dyc, tk)
    dq, dk, dv, dcs, drs = _attn_bwd(tag + "_dattn", q, k, v, c, ct, o, lse, do, t)
    dc = drs[:, :, :2].transpose(1, 0, 2).reshape(m, N_HEADS) - dcs[:, :, :2, :].transpose(0, 2, 1, 3).reshape(N_HEADS, m).T
    dc = jnp.pad(dc, ((0, 0), (0, F_PAD - N_HEADS)))
    dzf, dbf = _fgate_bwd(tag + "_dfgate", dc, zf, bf)
    dzc, dconv_w, dconv_b, dg_conv = _conv_bwd(tag + "_dconv", dcv, y, zc, conv_w, g_conv, t)
    dz = jnp.concatenate([dq.astype(BF16), dk.astype(BF16), dv.astype(BF16), dzf, dzc, dzg], axis=1)
    tm = _row_tile(m, 256)
    row = pl.BlockSpec((tm, d), lambda i: (i, 0))
    one = pl.BlockSpec((1, d), lambda i: (0, 0))
    dh2, dg = _mm(tag + "_du", dz, w_in_p, grid=(m // tm,), dims=NT,
                  a_spec=pl.BlockSpec((tm, IN_PAD), lambda i: (i, 0)), b_spec=pl.BlockSpec((d, IN_PAD), lambda i: (0, 0)),
                  out=_sds((m, d), F32), o_spec=row, rms=(h, g, dh, row, one))
    rt = 256
    d_win_p = _mm(tag + "_dwin", u, dz, grid=(d // rt, m // tk), dims=TN, nk=m // tk, k_axis=1,
                  a_spec=pl.BlockSpec((tk, rt), lambda r, kk: (kk, r)), b_spec=pl.BlockSpec((tk, IN_PAD), lambda r, kk: (kk, 0)),
                  out=_sds((d, IN_PAD), BF16), o_spec=pl.BlockSpec((rt, IN_PAD), lambda r, kk: (r, 0)), acc_shape=(rt, IN_PAD))
    return dh2, dg, d_win_p, dbf, d_wao, dconv_w, dconv_b, dg_conv, d_wco, d_wout


def _col_shards(full, n_cols):
    r = full.shape[0]
    return full.reshape(r, N_DEV, n_cols).transpose(1, 0, 2)


def _from_col_shards(stacked):
    _, r, n = stacked.shape
    return stacked.transpose(1, 0, 2).reshape(r, N_DEV * n)


SMALL_ROWS = 24


def _pack_small(g_ff1, g_mix, g_ff2, g_ple, g_final, conv_b, g_conv, b_f, loss_row):
    n_layers = g_ff1.shape[0]
    bf_row = jnp.pad(b_f.reshape(1, n_layers * N_HEADS), ((0, 0), (0, D_MODEL - n_layers * N_HEADS)))
    parts = [g_ff1, g_mix, g_ff2, g_ple, g_final.reshape(1, D_MODEL), conv_b.reshape(-1, D_MODEL),
             g_conv.reshape(-1, D_MODEL), bf_row, loss_row]
    packed = jnp.concatenate(parts, axis=0)
    return jnp.pad(packed, ((0, SMALL_ROWS - packed.shape[0]), (0, 0)))


def _unpack_small(packed, n_layers):
    ln = n_layers
    cr = n_layers * CONV_CH // D_MODEL
    pos = 4 * ln + 1
    return dict(
        g_ff1=packed[0:ln], g_mix=packed[ln:2 * ln], g_ff2=packed[2 * ln:3 * ln], g_ple=packed[3 * ln:4 * ln],
        g_final=packed[4 * ln], conv_b=packed[pos:pos + cr].reshape(ln, CONV_CH),
        g_conv=packed[pos + cr:pos + 2 * cr].reshape(ln, CONV_CH),
        b_f=packed[pos + 2 * cr, :ln * N_HEADS].reshape(ln, N_HEADS), loss=packed[pos + 2 * cr + 1, 0])


BIG = ("w_ff1_in", "w_ff1_out", "w_in", "w_attn_out", "conv_w", "w_conv_out", "w_out", "w_ff2_in", "w_ff2_out",
       "w_ple_gate", "w_ple_proj")
ROW_BAND = dict(w_ff1_in=None, w_ff1_out=D_FF // N_DEV, w_in=None, w_attn_out=None, conv_w=None, w_conv_out=None,
                w_out=D_MODEL // N_DEV, w_ff2_in=None, w_ff2_out=D_FF // N_DEV, w_ple_gate=D_MODEL // N_DEV, w_ple_proj=None)
FF_IN = ("w_ff1_in", "w_ff2_in")
SMALL = ("g_ff1", "g_mix", "g_ff2", "g_ple", "g_final", "conv_b", "g_conv", "b_f")
ORDER = ("g_ff1", "w_ff1_in", "w_ff1_out", "g_mix", "w_in", "b_f", "w_attn_out", "conv_w", "conv_b", "g_conv", "w_conv_out",
         "w_out", "g_ff2", "w_ff2_in", "w_ff2_out", "g_ple", "w_ple_gate", "w_ple_proj", "g_final")


def kernel(x, p, g_ff1, w_ff1_in, w_ff1_out, g_mix, w_in, b_f, w_attn_out, conv_w, conv_b, g_conv, w_conv_out, w_out, g_ff2, w_ff2_in, w_ff2_out, g_ple, w_ple_gate, w_ple_proj, g_final, loss_target, m_g_ff1, m_w_ff1_in, m_w_ff1_out, m_g_mix, m_w_in, m_b_f, m_w_attn_out, m_conv_w, m_conv_b, m_g_conv, m_w_conv_out, m_w_out, m_g_ff2, m_w_ff2_in, m_w_ff2_out, m_g_ple, m_w_ple_gate, m_w_ple_proj, m_g_final, v_g_ff1, v_w_ff1_in, v_w_ff1_out, v_g_mix, v_w_in, v_b_f, v_w_attn_out, v_conv_w, v_conv_b, v_g_conv, v_w_conv_out, v_w_out, v_g_ff2, v_w_ff2_in, v_w_ff2_out, v_g_ple, v_w_ple_gate, v_w_ple_proj, v_g_final):
    local = dict(locals())
    W = {n: local[n] for n in ORDER}
    M1 = {n: local["m_" + n] for n in ORDER}
    V1 = {n: local["v_" + n] for n in ORDER}
    for group in (W, M1, V1):
        for n in FF_IN:
            group[n] = group[n].transpose(0, 2, 1)
    n_layers = g_ff1.shape[0]
    m_rows = x.shape[1]
    t = _row_tile(m_rows, 256)
    my_idx = _linear(*_place()).astype(jnp.int32)
    idx_arr = my_idx.reshape(1)
    bands = [ROW_BAND[n] for n in BIG]

    def gather_start(l, follows):
        shards = [W[n][l].astype(F32 if n == "conv_w" else BF16) for n in BIG]
        zones = [_own_block_filled(s, band, my_idx) for s, band in zip(shards, bands)]
        return _split_start(f"gather_start_{l}", False, shards, zones, bands, follows)

    def weights_of(l, got):
        got = dict(zip(BIG, got))
        w_in_full = _from_col_shards(got["w_in"])
        zeros = jnp.zeros((D_MODEL, F_PAD - N_HEADS), BF16)
        return dict(
            wi1=got["w_ff1_in"], wo1=got["w_ff1_out"], wi2=got["w_ff2_in"], wo2=got["w_ff2_out"],
            w_in_p=jnp.concatenate([w_in_full[:, :P_F + N_HEADS], zeros, w_in_full[:, P_F + N_HEADS:]], axis=1),
            w_ao=_from_col_shards(got["w_attn_out"]), w_co=_from_col_shards(got["w_conv_out"]),
            w_out=got["w_out"], w_pg=got["w_ple_gate"], w_pp=_from_col_shards(got["w_ple_proj"]),
            conv_w=jnp.pad(_from_col_shards(got["conv_w"]), ((0, CONV_HALO - CONV_K), (0, 0))),
            bf=jnp.pad(b_f[l].reshape(1, N_HEADS), ((0, 0), (0, F_PAD - N_HEADS))),
            g1=g_ff1[l].reshape(1, -1), gm=g_mix[l].reshape(1, -1), g2=g_ff2[l].reshape(1, -1), gp=g_ple[l].reshape(1, -1),
            conv_b=conv_b[l].reshape(1, -1), g_conv=g_conv[l].reshape(1, -1))

    h = x[0]
    saved, full = [], []
    flight = gather_start(0, h)
    for l in range(n_layers):
        got = _split_wait(f"gather_wait_{l}", False, flight, bands, h)[1]
        after = ()
        if l + 1 < n_layers:
            flight = gather_start(l + 1, got[0])
            after = (flight[4],)
        fw = weights_of(l, got)
        full.append(fw)
        h, s1 = _ffn_fwd("ff1", h, fw["g1"], fw["wi1"], fw["wo1"], after)
        h, s2 = _mixer_fwd("mix", h, fw["gm"], fw["w_in_p"], fw["bf"], fw["conv_w"], fw["conv_b"], fw["g_conv"],
                           fw["w_ao"], fw["w_co"], fw["w_out"])
        h, s3 = _ffn_fwd("ff2", h, fw["g2"], fw["wi2"], fw["wo2"])
        h_in = h
        pl_in = p[l, 0]
        h, n_ple, gl, pp = _ple_fwd("ple", h, pl_in, fw["gp"], fw["w_pg"], fw["w_pp"], t)
        saved.append((s1, s2, s3, (h_in, pl_in, n_ple, gl, pp)))

    loss_row, dh, dg_final = _loss_head("loss_head", h, g_final.reshape(1, -1), loss_target[0], t)

    small_grads = {n: [None] * n_layers for n in ("g_ff1", "g_mix", "g_ff2", "g_ple", "conv_b", "g_conv", "b_f")}
    stacked = {n: None for n in BIG}
    tk = _row_tile(m_rows, 512)

    def apply(l, flight, follows):
        terms, recv = _split_wait(f"scatter_wait_{l}", True, flight, bands, follows)
        for n, own, rc, band in zip(BIG, terms, recv, bands):
            stacked[n] = _adamw("adamw_" + n, rc, own, band, idx_arr, W[n], M1[n], V1[n], l, stacked[n])

    flight, after = None, ()
    for l in reversed(range(n_layers)):
        fw = full[l]
        s1, s2, s3, (h_in, pl_in, n_ple, gl, pp) = saved[l]
        dh, dgl, dpp, dgp = _ple_bwd("ple_bwd", dh, h_in, fw["gp"], gl, pp, fw["w_pg"], t, after)
        d_wpg = _wgrad("ple_dwgate", n_ple, dgl, tk)
        d_wpp = _wgrad("ple_dwproj", pl_in, dpp, tk)
        dh, dg2, d_wi2, d_wo2 = _ffn_bwd("ff2", dh, s3, fw["g2"], fw["wi2"], fw["wo2"])
        dh, dgm, d_win_p, dbf, d_wao, dconv_w, dconv_b, dg_conv, d_wco, d_wout = _mixer_bwd(
            "mix", dh, s2, fw["gm"], fw["w_in_p"], fw["bf"], fw["conv_w"], fw["g_conv"], fw["w_ao"], fw["w_co"], fw["w_out"])
        dh, dg1, d_wi1, d_wo1 = _ffn_bwd("ff1", dh, s1, fw["g1"], fw["wi1"], fw["wo1"])
        small_grads["g_ff1"][l], small_grads["g_mix"][l], small_grads["g_ff2"][l], small_grads["g_ple"][l] = dg1, dgm, dg2, dgp
        small_grads["conv_b"][l], small_grads["g_conv"][l] = dconv_b, dg_conv
        small_grads["b_f"][l] = dbf[:, :N_HEADS]

        d_win = jnp.concatenate([d_win_p[:, :P_F + N_HEADS], d_win_p[:, P_C:]], axis=1)
        grads = dict(
            w_ff1_in=d_wi1, w_ff1_out=d_wo1, w_in=_col_shards(d_win, IN_SHARD), w_attn_out=_col_shards(d_wao, D_MODEL // N_DEV),
            conv_w=_col_shards(dconv_w[:CONV_K], CONV_CH // N_DEV), w_conv_out=_col_shards(d_wco, D_MODEL // N_DEV),
            w_out=d_wout, w_ff2_in=d_wi2, w_ff2_out=d_wo2, w_ple_gate=d_wpg, w_ple_proj=_col_shards(d_wpp, D_MODEL // N_DEV))
        terms = [grads[n] for n in BIG]
        zones = [lax.empty((N_PEERS,) + (g.shape[1:] if band is None else (band,) + g.shape[1:]), g.dtype)
                 for g, band in zip(terms, bands)]
        if flight is not None:
            apply(l + 1, flight, dh)
        flight = _split_start(f"scatter_start_{l}", True, terms, zones, bands, dh)
        after = (flight[4],)
    apply(0, flight, dh)

    cat = {n: jnp.concatenate(small_grads[n], axis=0) for n in small_grads}
    g_pack = _pack_small(cat["g_ff1"], cat["g_mix"], cat["g_ff2"], cat["g_ple"], dg_final, cat["conv_b"], cat["g_conv"],
                         cat["b_f"], loss_row[:, :1] * jnp.ones((1, D_MODEL), F32))
    zero_row = jnp.zeros((1, D_MODEL), F32)
    packs = [_pack_small(*[src[n] for n in ("g_ff1", "g_mix", "g_ff2", "g_ple", "g_final", "conv_b", "g_conv", "b_f")], zero_row)
             for src in (W, M1, V1)]
    (recv_small,) = _exchange("gather_small", [g_pack], [None], [True])
    outs_small = [_unpack_small(a, n_layers) for a in _adamw_small("adamw_small", recv_small, *packs)]

    def pick(kind, n):
        if n in SMALL:
            return outs_small[kind][n]
        return stacked[n][kind].transpose(0, 2, 1) if n in FF_IN else stacked[n][kind]

    result = [outs_small[0]["loss"], dh.reshape(x.shape)]
    for kind in range(4):
        result += [pick(kind, n) for n in ORDER]
    return tuple(result)
```

```python
import functools
import math

import jax
import jax.numpy as jnp
from jax import lax
from jax.experimental import pallas as pl
from jax.experimental.pallas import tpu as pltpu

F32 = jnp.float32
BF16 = jnp.bfloat16

N_DEV = 8
D_MODEL = 1024
N_HEADS = 8
HEAD_DIM = 64
ATTN_W = N_HEADS * HEAD_DIM
CONV_CH = 512
CONV_K = 31
CONV_HALO = 32
D_FF = 2816
FF_SHARD = 2 * D_FF // N_DEV
D_PLE = 256
EPS = 1e-6
FFN_RES = 0.5
LANES = 128
IN_COLS = 3 * ATTN_W + N_HEADS + 2 * CONV_CH + 2 * D_MODEL
IN_SHARD = IN_COLS // N_DEV
F_PAD = LANES
P_Q, P_K, P_V = 0, ATTN_W, 2 * ATTN_W
P_F = 3 * ATTN_W
P_C = P_F + F_PAD
P_G = P_C + 2 * CONV_CH
IN_PAD = P_G + 2 * D_MODEL
NEG_BIG = -1e30

ADAM_LR, ADAM_B1, ADAM_B2, ADAM_EPS, ADAM_WD, ADAM_STEP = 0.001, 0.9, 0.999, 1e-08, 0.01, 10

VMEM_CAP = 60 * 1024 * 1024
VMEM_SLACK = 12 * 1024 * 1024

NN = (((1,), (0,)), ((), ()))
NT = (((1,), (1,)), ((), ()))
TN = (((0,), (0,)), ((), ()))

MESH = pl.DeviceIdType.MESH
ANY = pl.BlockSpec(memory_space=pl.ANY)


def _nbytes(shape, dtype):
    return math.prod(d for d in shape if d is not None) * jnp.dtype(dtype).itemsize


def _params(block_bytes, n_axes):
    limit = min(VMEM_CAP, 2 * block_bytes + VMEM_SLACK)
    return pltpu.CompilerParams(dimension_semantics=("arbitrary",) * n_axes, vmem_limit_bytes=limit)


def _call(body, name, grid, in_arrays, in_specs, out_shapes, out_specs, scratch=(), aliases=None, extra_bytes=0,
          after=(), prefetch=None):
    total = extra_bytes
    for a, s in zip(in_arrays, in_specs):
        if s.block_shape is not None:
            total += _nbytes(s.block_shape, a.dtype)
    for o, s in zip(out_shapes, out_specs):
        if s.block_shape is not None:
            total += _nbytes(s.block_shape, o.dtype)
    n_in, n_after = len(in_arrays), len(after)
    lead = 0 if prefetch is None else 1

    def with_after(*refs):
        return body(*refs[:lead + n_in], *refs[lead + n_in + n_after:])

    operands = [pltpu.with_memory_space_constraint(a, pltpu.HBM) for a in (*in_arrays, *after)]
    specs = list(in_specs) + [ANY] * n_after
    common = dict(name=name, out_shape=[pltpu.HBM(o.shape, o.dtype) for o in out_shapes], input_output_aliases=aliases or {},
                  compiler_params=_params(total, len(grid)))
    if prefetch is None:
        return pl.pallas_call(with_after if n_after else body, grid=grid, in_specs=specs, out_specs=list(out_specs),
                              scratch_shapes=list(scratch), **common)(*operands)
    grid_spec = pltpu.PrefetchScalarGridSpec(num_scalar_prefetch=1, grid=grid, in_specs=specs, out_specs=list(out_specs),
                                             scratch_shapes=list(scratch))
    return pl.pallas_call(with_after if n_after else body, grid_spec=grid_spec, **common)(prefetch, *operands)


def _sig(x):
    return 1.0 / (1.0 + jnp.exp(-x))


def _dot(a, b, dims):
    return lax.dot_general(a.astype(BF16), b.astype(BF16), dims, preferred_element_type=F32)


def _rms_stats(x):
    r = lax.rsqrt(jnp.mean(x * x, axis=-1, keepdims=True) + EPS)
    return r, x * r


def _rms_bwd(dn, x, g):
    r, xh = _rms_stats(x)
    t = dn * g
    dx = r * (t - xh * jnp.mean(t * xh, axis=-1, keepdims=True))
    return dx, jnp.sum(dn * xh, axis=0, keepdims=True)


def _row_tile(m, want):
    t = min(m, want)
    assert m % t == 0
    return t


def _mm(name, a, b, *, grid, a_spec, b_spec, out, o_spec, dims, nk=None, k_axis=None, alpha=1.0,
        res=None, res_spec=None, rms=None, acc_shape=None, after=()):
    first_axes = len(grid)

    def body(*refs):
        refs = list(refs)
        a_ref, b_ref = refs[:2]
        pos = 2
        if res is not None:
            res_ref = refs[pos]
            pos += 1
        if rms is not None:
            h_ref, g_ref, dres_ref = refs[pos:pos + 3]
            pos += 3
        o_ref = refs[pos]
        pos += 1
        if rms is not None:
            dg_ref = refs[pos]
            pos += 1
        acc_ref = refs[pos] if k_axis is not None else None
        ids = [pl.program_id(ax) for ax in range(first_axes)]
        row_axes = [ids[ax] == 0 for ax in range(first_axes) if ax != k_axis]
        is_first = functools.reduce(jnp.logical_and, row_axes) if row_axes else None

        part = _dot(a_ref[...], b_ref[...], dims)

        def finish(acc):
            val = acc if alpha == 1.0 else acc * alpha
            if res is not None:
                val = val + res_ref[...]
            if rms is not None:
                dx, dg = _rms_bwd(val, h_ref[...], g_ref[...])
                o_ref[...] = (dres_ref[...] + dx).astype(o_ref.dtype)

                @pl.when(is_first)
                def _():
                    dg_ref[...] = dg

                @pl.when(jnp.logical_not(is_first))
                def _():
                    dg_ref[...] += dg
            else:
                o_ref[...] = val.astype(o_ref.dtype)

        if k_axis is None:
            finish(part)
        else:
            k = ids[k_axis]

            @pl.when(k == 0)
            def _():
                acc_ref[...] = part

            @pl.when(k > 0)
            def _():
                acc_ref[...] += part

            @pl.when(k == nk - 1)
            def _():
                finish(acc_ref[...])

    in_arrays, in_specs = [a, b], [a_spec, b_spec]
    if res is not None:
        in_arrays.append(res)
        in_specs.append(res_spec)
    outs, o_specs = [out], [o_spec]
    if rms is not None:
        h, g, dres, row_spec, g_spec = rms
        in_arrays += [h, g, dres]
        in_specs += [row_spec, g_spec, row_spec]
        outs.append(jax.ShapeDtypeStruct(g.shape, F32))
        o_specs.append(g_spec)
    scratch, extra = [], 0
    if k_axis is not None:
        scratch = [pltpu.VMEM(acc_shape, F32)]
        extra = _nbytes(acc_shape, F32)
    res_out = _call(body, name, grid, in_arrays, in_specs, outs, o_specs, scratch, extra_bytes=extra, after=after)
    return res_out if rms is not None else res_out[0]


def _sds(shape, dtype):
    return jax.ShapeDtypeStruct(shape, dtype)


def _rmsnorm_fwd(name, h, g, after=()):
    m, d = h.shape
    tm = _row_tile(m, 512)

    def body(h_ref, g_ref, o_ref):
        _, xh = _rms_stats(h_ref[...])
        o_ref[...] = (xh * g_ref[...]).astype(BF16)

    row = pl.BlockSpec((tm, d), lambda i: (i, 0))
    return _call(body, name, (m // tm,), [h, g], [row, pl.BlockSpec((1, d), lambda i: (0, 0))],
                 [_sds((m, d), BF16)], [row], after=after)[0]


def _swiglu_fwd(name, ab):
    _, m, w = ab.shape
    half = N_DEV // 2
    tm = _row_tile(m, 512)
    ab4 = ab.reshape(2, half, m, w)

    def body(ab_ref, o_ref):
        a, b = ab_ref[0], ab_ref[1]
        o_ref[...] = (a * _sig(a) * b).astype(BF16)

    return _call(body, name, (half, m // tm), [ab4],
                 [pl.BlockSpec((2, None, tm, w), lambda j, i: (0, j, i, 0))],
                 [_sds((half, m, w), BF16)], [pl.BlockSpec((None, tm, w), lambda j, i: (j, i, 0))])[0]


def _swiglu_bwd(name, ab, ds):
    _, m, w = ab.shape
    half = N_DEV // 2
    tm = _row_tile(m, 512)
    ab4 = ab.reshape(2, half, m, w)

    def body(ab_ref, ds_ref, o_ref):
        a, b, d = ab_ref[0], ab_ref[1], ds_ref[...]
        sg = _sig(a)
        o_ref[0] = (d * b * sg * (1.0 + a * (1.0 - sg))).astype(BF16)
        o_ref[1] = (d * a * sg).astype(BF16)

    blk = pl.BlockSpec((2, None, tm, w), lambda j, i: (0, j, i, 0))
    out = _call(body, name, (half, m // tm), [ab4, ds],
                [blk, pl.BlockSpec((None, tm, w), lambda j, i: (j, i, 0))],
                [_sds((2, half, m, w), BF16)], [blk])[0]
    return out.reshape(N_DEV, m, w)


def _inproj(name, u, w_in_p):
    m, d = u.shape
    tm = _row_tile(m, 256)

    def body(u_ref, w_ref, q_ref, k_ref, v_ref, f_ref, c_ref, g_ref):
        z = _dot(u_ref[...], w_ref[...], NN)
        q_ref[...] = z[:, P_Q:P_K].astype(BF16)
        k_ref[...] = z[:, P_K:P_V].astype(BF16)
        v_ref[...] = z[:, P_V:P_F].astype(BF16)
        f_ref[...] = z[:, P_F:P_C]
        c_ref[...] = z[:, P_C:P_G]
        g_ref[...] = z[:, P_G:IN_PAD]

    def rows(width):
        return pl.BlockSpec((tm, width), lambda i: (i, 0))

    widths = (ATTN_W, ATTN_W, ATTN_W, F_PAD, 2 * CONV_CH, 2 * D_MODEL)
    dtypes = (BF16, BF16, BF16, F32, F32, F32)
    return _call(body, name, (m // tm,), [u, w_in_p], [rows(d), pl.BlockSpec((d, IN_PAD), lambda i: (0, 0))],
                 [_sds((m, wd), dt) for wd, dt in zip(widths, dtypes)], [rows(wd) for wd in widths])


def _cumsum_rows(x, reverse):
    m = x.shape[0]
    row = lax.broadcasted_iota(jnp.int32, x.shape, 0)
    sh = 1
    while sh < m:
        if reverse:
            x = x + jnp.where(row < m - sh, pltpu.roll(x, m - sh, axis=0), 0.0)
        else:
            x = x + jnp.where(row >= sh, pltpu.roll(x, sh, axis=0), 0.0)
        sh *= 2
    return x


def _log_sigmoid(x):
    return jnp.minimum(x, 0.0) - jnp.log(1.0 + jnp.exp(-jnp.abs(x)))


def _fgate_fwd(name, zf, bf):
    m, w = zf.shape

    def body(z_ref, b_ref, c_ref):
        c_ref[...] = _cumsum_rows(_log_sigmoid(z_ref[...] + b_ref[...]), reverse=False)

    full = pl.BlockSpec((m, w), lambda i: (0, 0))
    return _call(body, name, (1,), [zf, bf], [full, pl.BlockSpec((1, w), lambda i: (0, 0))], [_sds((m, w), F32)], [full])[0]


def _fgate_bwd(name, dc, zf, bf):
    m, w = zf.shape

    def body(dc_ref, z_ref, b_ref, dz_ref, db_ref):
        dls = _cumsum_rows(dc_ref[...], reverse=True)
        dz = dls * _sig(-(z_ref[...] + b_ref[...]))
        lane = lax.broadcasted_iota(jnp.int32, dz.shape, 1)
        dz = jnp.where(lane < N_HEADS, dz, 0.0)
        dz_ref[...] = dz.astype(BF16)
        db_ref[...] = jnp.sum(dz, axis=0, keepdims=True)

    full = pl.BlockSpec((m, w), lambda i: (0, 0))
    one = pl.BlockSpec((1, w), lambda i: (0, 0))
    return _call(body, name, (1,), [dc, zf, bf], [full, full, one], [_sds((m, w), BF16), _sds((1, w), F32)], [full, one])


def _lane_pick(x, idx):
    lane = lax.broadcasted_iota(jnp.int32, x.shape, 1)
    return jnp.sum(jnp.where(lane == idx, x, 0.0), axis=1, keepdims=True)


def _row_pick(x, idx):
    sub = lax.broadcasted_iota(jnp.int32, x.shape, 0)
    return jnp.sum(jnp.where(sub == idx, x, 0.0), axis=0, keepdims=True)


def _attn_fwd(name, q, k, v, c, ct, t):
    m = q.shape[0]
    n_chunks = m // t
    pairs = N_HEADS // 2
    scale = 1.0 / math.sqrt(HEAD_DIM)

    def body(q_ref, k_ref, v_ref, c_ref, ct_ref, o_ref, lse_ref):
        p = pl.program_id(0)
        i = pl.program_id(1)
        lane = lax.broadcasted_iota(jnp.int32, (t, LANES), 1)
        first = lane < HEAD_DIM
        q2 = q_ref[...]
        zero = jnp.zeros_like(q2)
        q_heads = (jnp.where(first, q2, zero), jnp.where(first, zero, q2))
        cblk = c_ref[...]
        c_cols = (_lane_pick(cblk, 2 * p), _lane_pick(cblk, 2 * p + 1))
        rows = i * t + lax.broadcasted_iota(jnp.int32, (t, t), 0)

        def step(j, carry):
            off = pl.multiple_of(j * t, t)
            kj = k_ref[pl.ds(off, t), :]
            vj = v_ref[pl.ds(off, t), :]
            ctj = ct_ref[j]
            visible = rows >= off + lax.broadcasted_iota(jnp.int32, (t, t), 1)
            new = []
            for e in range(2):
                mx, den, acc = carry[3 * e:3 * e + 3]
                s = _dot(q_heads[e], kj, NT) * scale + (c_cols[e] - _row_pick(ctj, 2 * p + e))
                s = jnp.where(visible, s, NEG_BIG)
                mx2 = jnp.maximum(mx, jnp.max(s, axis=1, keepdims=True))
                corr = jnp.exp(mx - mx2)
                pe = jnp.exp(s - mx2)
                new += [mx2, corr * den + jnp.sum(pe, axis=1, keepdims=True), corr * acc + _dot(pe, vj, NN)]
            return tuple(new)

        col = jnp.full((t, 1), NEG_BIG, F32), jnp.zeros((t, 1), F32), jnp.zeros((t, LANES), F32)
        fin = lax.fori_loop(0, i + 1, step, col + col)
        o_ref[...] = jnp.where(first, fin[2] / fin[1], fin[5] / fin[4]).astype(BF16)
        lse_a = fin[0] + jnp.log(fin[1])
        lse_b = fin[3] + jnp.log(fin[4])
        lse_ref[...] = jnp.where(lane == 0, lse_a, jnp.where(lane == 1, lse_b, 0.0))

    seq = pl.BlockSpec((m, LANES), lambda p, i: (0, p))
    blk = pl.BlockSpec((t, LANES), lambda p, i: (i, p))
    return _call(body, name, (pairs, n_chunks), [q, k, v, c, ct],
                 [blk, seq, seq, pl.BlockSpec((t, LANES), lambda p, i: (i, 0)),
                  pl.BlockSpec((n_chunks, 8, t), lambda p, i: (0, 0, 0))],
                 [_sds((m, ATTN_W), BF16), _sds((pairs, m, LANES), F32)],
                 [blk, pl.BlockSpec((None, t, LANES), lambda p, i: (p, i, 0))])


def _attn_bwd(name, q, k, v, c, ct, o, lse, do, t):
    m = q.shape[0]
    n_chunks = m // t
    pairs = N_HEADS // 2
    scale = 1.0 / math.sqrt(HEAD_DIM)

    def body(q_ref, do_ref, o_ref, lse_ref, c_ref, ct_ref, k_ref, v_ref, dq_ref, dk_ref, dv_ref, dcs_ref, drs_ref):
        p = pl.program_id(0)
        j = pl.program_id(1)

        @pl.when(j == 0)
        def _():
            dq_ref[...] = jnp.zeros_like(dq_ref)
            drs_ref[...] = jnp.zeros_like(drs_ref)

        lane = lax.broadcasted_iota(jnp.int32, (t, LANES), 1)
        first = lane < HEAD_DIM
        kj, vj = k_ref[...], v_ref[...]
        zero = jnp.zeros_like(kj)
        k_heads = (jnp.where(first, kj, zero), jnp.where(first, zero, kj))
        v_heads = (jnp.where(first, vj, zero), jnp.where(first, zero, vj))
        ctj = ct_ref[j]
        c_rows = (_row_pick(ctj, 2 * p), _row_pick(ctj, 2 * p + 1))
        cols = j * t + lax.broadcasted_iota(jnp.int32, (t, t), 1)

        def step(i, carry):
            off = pl.multiple_of(i * t, t)
            qi = q_ref[pl.ds(off, t), :]
            doi = do_ref[pl.ds(off, t), :]
            prod = doi * o_ref[pl.ds(off, t), :].astype(F32)
            lsei = lse_ref[pl.ds(off, t), :]
            ci = c_ref[pl.ds(off, t), :]
            dob = doi.astype(BF16)
            visible = off + lax.broadcasted_iota(jnp.int32, (t, t), 0) >= cols
            new = []
            dq = jnp.zeros((t, LANES), F32)
            drow = jnp.zeros((t, LANES), F32)
            for e in range(2):
                dk, dv, dcol = carry[3 * e:3 * e + 3]
                delta = jnp.sum(jnp.where(first if e == 0 else jnp.logical_not(first), prod, 0.0), axis=1, keepdims=True)
                s = _dot(qi, k_heads[e], NT) * scale + (_lane_pick(ci, 2 * p + e) - c_rows[e])
                pe = jnp.where(visible, jnp.exp(s - _lane_pick(lsei, e)), 0.0)
                dsc = pe * (_dot(dob, v_heads[e], NT) - delta)
                dsb = dsc.astype(BF16)
                dq = dq + _dot(dsb, k_heads[e], NN)
                drow = jnp.where(lane == e, jnp.sum(dsc, axis=1, keepdims=True), drow)
                new += [dk + _dot(dsb, qi, TN), dv + _dot(pe, dob, TN), dcol + jnp.sum(dsc, axis=0, keepdims=True)]
            dq_ref[pl.ds(off, t), :] += dq * scale
            drs_ref[pl.ds(off, t), :] += drow
            return tuple(new)

        z = jnp.zeros((t, LANES), F32), jnp.zeros((t, LANES), F32), jnp.zeros((1, t), F32)
        fin = lax.fori_loop(j, n_chunks, step, z + z)
        dk_ref[...] = jnp.where(first, fin[0], fin[3]) * scale
        dv_ref[...] = jnp.where(first, fin[1], fin[4])
        sub = lax.broadcasted_iota(jnp.int32, (8, t), 0)
        dcs_ref[...] = jnp.where(sub == 0, fin[2], jnp.where(sub == 1, fin[5], 0.0))

    seq = pl.BlockSpec((m, LANES), lambda p, j: (0, p))
    blk = pl.BlockSpec((t, LANES), lambda p, j: (j, p))
    return _call(body, name, (pairs, n_chunks), [q, do, o, lse, c, ct, k, v],
                 [seq, seq, seq, pl.BlockSpec((None, m, LANES), lambda p, j: (p, 0, 0)),
                  pl.BlockSpec((m, LANES), lambda p, j: (0, 0)),
                  pl.BlockSpec((n_chunks, 8, t), lambda p, j: (0, 0, 0)), blk, blk],
                 [_sds((m, ATTN_W), F32), _sds((m, ATTN_W), F32), _sds((m, ATTN_W), F32),
                  _sds((pairs, n_chunks, 8, t), F32), _sds((pairs, m, LANES), F32)],
                 [seq, blk, blk, pl.BlockSpec((None, None, 8, t), lambda p, j: (p, j, 0, 0)),
                  pl.BlockSpec((None, m, LANES), lambda p, j: (p, 0, 0))])


def _glu(z):
    return z[:, :CONV_CH] * _sig(z[:, CONV_CH:])


def _shifted(x, lead, tm):
    n = x.shape[0]
    return pltpu.roll(x, (n - lead) % n, axis=0)[:tm]


def _conv_fwd(name, zc, w, b, g, tm):
    m = zc.shape[0]
    hb = tm // CONV_HALO

    def body(cur_ref, prev_ref, w_ref, b_ref, g_ref, cv_ref, y_ref):
        i = pl.program_id(0)
        a_prev = jnp.where(i > 0, _glu(prev_ref[...]), 0.0)
        af = jnp.concatenate([a_prev, _glu(cur_ref[...])], axis=0)
        y = jnp.zeros((tm, CONV_CH), F32)
        for tap in range(CONV_K):
            y = y + w_ref[pl.ds(tap, 1), :] * _shifted(af, CONV_HALO - (CONV_K - 1) + tap, tm)
        y = y + b_ref[...]
        y_ref[...] = y
        _, xh = _rms_stats(y)
        rn = xh * g_ref[...]
        cv_ref[...] = (rn * _sig(rn)).astype(BF16)

    one = pl.BlockSpec((1, CONV_CH), lambda i: (0, 0))
    out = pl.BlockSpec((tm, CONV_CH), lambda i: (i, 0))
    return _call(body, name, (m // tm,), [zc, zc, w, b, g],
                 [pl.BlockSpec((tm, 2 * CONV_CH), lambda i: (i, 0)),
                  pl.BlockSpec((CONV_HALO, 2 * CONV_CH), lambda i: (jnp.maximum(i * hb - 1, 0), 0)),
                  pl.BlockSpec((CONV_HALO, CONV_CH), lambda i: (0, 0)), one, one],
                 [_sds((m, CONV_CH), BF16), _sds((m, CONV_CH), F32)], [out, out])


def _conv_bwd(name, dcv, y, zc, w, g, tm):
    m = zc.shape[0]
    hb = tm // CONV_HALO
    n_blocks = m // tm

    def body(dcv_ref, dcvn_ref, y_ref, yn_ref, cur_ref, prev_ref, w_ref, g_ref, dz_ref, dw_ref, db_ref, dg_ref):
        i = pl.program_id(0)
        gv = g_ref[...]

        def dy_of(d, yv):
            r, xh = _rms_stats(yv)
            rn = xh * gv
            sg = _sig(rn)
            drn = d * sg * (1.0 + rn * (1.0 - sg))
            tt = drn * gv
            return r * (tt - xh * jnp.mean(tt * xh, axis=-1, keepdims=True)), drn * xh

        dy, dgt = dy_of(dcv_ref[...], y_ref[...])
        dy_next, _ = dy_of(dcvn_ref[...], yn_ref[...])
        dyf = jnp.concatenate([dy, jnp.where(i < n_blocks - 1, dy_next, 0.0)], axis=0)
        cur = cur_ref[...]
        af = jnp.concatenate([jnp.where(i > 0, _glu(prev_ref[...]), 0.0), _glu(cur)], axis=0)

        @pl.when(i == 0)
        def _():
            dw_ref[...] = jnp.zeros_like(dw_ref)
            db_ref[...] = jnp.zeros_like(db_ref)
            dg_ref[...] = jnp.zeros_like(dg_ref)

        da = jnp.zeros((tm, CONV_CH), F32)
        for tap in range(CONV_K):
            da = da + w_ref[pl.ds(tap, 1), :] * _shifted(dyf, CONV_K - 1 - tap, tm)
            a_tap = _shifted(af, CONV_HALO - (CONV_K - 1) + tap, tm)
            dw_ref[pl.ds(tap, 1), :] += jnp.sum(dy * a_tap, axis=0, keepdims=True)
        db_ref[...] += jnp.sum(dy, axis=0, keepdims=True)
        dg_ref[...] += jnp.sum(dgt, axis=0, keepdims=True)
        c1, sg2 = cur[:, :CONV_CH], _sig(cur[:, CONV_CH:])
        dz_ref[:, :CONV_CH] = (da * sg2).astype(BF16)
        dz_ref[:, CONV_CH:] = (da * c1 * sg2 * (1.0 - sg2)).astype(BF16)

    one = pl.BlockSpec((1, CONV_CH), lambda i: (0, 0))
    taps = pl.BlockSpec((CONV_HALO, CONV_CH), lambda i: (0, 0))
    row = pl.BlockSpec((tm, CONV_CH), lambda i: (i, 0))
    nxt = pl.BlockSpec((CONV_HALO, CONV_CH), lambda i: (jnp.minimum((i + 1) * hb, m // CONV_HALO - 1), 0))
    row2 = pl.BlockSpec((tm, 2 * CONV_CH), lambda i: (i, 0))
    return _call(body, name, (n_blocks,), [dcv, dcv, y, y, zc, zc, w, g],
                 [row, nxt, row, nxt, row2,
                  pl.BlockSpec((CONV_HALO, 2 * CONV_CH), lambda i: (jnp.maximum(i * hb - 1, 0), 0)), taps, one],
                 [_sds((m, 2 * CONV_CH), BF16), _sds((CONV_HALO, CONV_CH), F32), _sds((1, CONV_CH), F32), _sds((1, CONV_CH), F32)],
                 [row2, taps, one, one])


def _mixout_fwd(name, o, cv, zg, h, w_ao, w_co, w_out, tm):
    m, d = h.shape

    def body(o_ref, cv_ref, zg_ref, h_ref, wa_ref, wc_ref, wo_ref, hn_ref, mg_ref, ya_ref, yc_ref):
        ya = _dot(o_ref[...], wa_ref[...], NN)
        yc = _dot(cv_ref[...], wc_ref[...], NN)
        zg_v = zg_ref[...]
        mg = (_sig(zg_v[:, :d]) * ya + _sig(zg_v[:, d:]) * yc).astype(BF16)
        ya_ref[...] = ya
        yc_ref[...] = yc
        mg_ref[...] = mg
        hn_ref[...] = h_ref[...] + _dot(mg, wo_ref[...], NN)

    def rows(wd):
        return pl.BlockSpec((tm, wd), lambda i: (i, 0))

    def whole(a):
        return pl.BlockSpec(a.shape, lambda i: (0, 0))

    return _call(body, name, (m // tm,), [o, cv, zg, h, w_ao, w_co, w_out],
                 [rows(ATTN_W), rows(CONV_CH), rows(2 * d), rows(d), whole(w_ao), whole(w_co), whole(w_out)],
                 [_sds((m, d), F32), _sds((m, d), BF16), _sds((m, d), F32), _sds((m, d), F32)],
                 [rows(d), rows(d), rows(d), rows(d)])


def _mixout_bwd(name, dh, zg, ya, yc, w_ao, w_co, w_out, tm, after=()):
    m, d = dh.shape

    def body(dh_ref, zg_ref, ya_ref, yc_ref, wa_ref, wc_ref, wo_ref, do_ref, dcv_ref, dzg_ref, dya_ref, dyc_ref):
        dm = _dot(dh_ref[...], wo_ref[...], NT)
        zg_v = zg_ref[...]
        sa, sc = _sig(zg_v[:, :d]), _sig(zg_v[:, d:])
        dya = (dm * sa).astype(BF16)
        dyc = (dm * sc).astype(BF16)
        dzg_ref[:, :d] = (dm * ya_ref[...] * sa * (1.0 - sa)).astype(BF16)
        dzg_ref[:, d:] = (dm * yc_ref[...] * sc * (1.0 - sc)).astype(BF16)
        dya_ref[...] = dya
        dyc_ref[...] = dyc
        do_ref[...] = _dot(dya, wa_ref[...], NT)
        dcv_ref[...] = _dot(dyc, wc_ref[...], NT)

    def rows(wd):
        return pl.BlockSpec((tm, wd), lambda i: (i, 0))

    def whole(a):
        return pl.BlockSpec(a.shape, lambda i: (0, 0))

    return _call(body, name, (m // tm,), [dh, zg, ya, yc, w_ao, w_co, w_out],
                 [rows(d), rows(2 * d), rows(d), rows(d), whole(w_ao), whole(w_co), whole(w_out)],
                 [_sds((m, ATTN_W), F32), _sds((m, CONV_CH), F32), _sds((m, 2 * d), BF16), _sds((m, d), BF16), _sds((m, d), BF16)],
                 [rows(ATTN_W), rows(CONV_CH), rows(2 * d), rows(d), rows(d)], after=after)


def _ple_fwd(name, h, p, g, w_gate, w_proj, tm):
    m, d = h.shape

    def body(h_ref, p_ref, g_ref, wg_ref, wp_ref, hn_ref, n_ref, gl_ref, pp_ref):
        hv = h_ref[...]
        _, xh = _rms_stats(hv)
        n = (xh * g_ref[...]).astype(BF16)
        gl = _dot(n, wg_ref[...], NN)
        pp = _dot(p_ref[...], wp_ref[...], NN)
        n_ref[...] = n
        gl_ref[...] = gl
        pp_ref[...] = pp
        hn_ref[...] = hv + _sig(gl) * pp

    def rows(wd):
        return pl.BlockSpec((tm, wd), lambda i: (i, 0))

    return _call(body, name, (m // tm,), [h, p, g, w_gate, w_proj],
                 [rows(d), rows(D_PLE), pl.BlockSpec((1, d), lambda i: (0, 0)),
                  pl.BlockSpec((d, d), lambda i: (0, 0)), pl.BlockSpec((D_PLE, d), lambda i: (0, 0))],
                 [_sds((m, d), F32), _sds((m, d), BF16), _sds((m, d), F32), _sds((m, d), F32)],
                 [rows(d)] * 4)


def _ple_bwd(name, dh, h, g, gl, pp, w_gate, tm, after=()):
    m, d = h.shape

    def body(dh_ref, h_ref, g_ref, gl_ref, pp_ref, wg_ref, dhn_ref, dgl_ref, dpp_ref, dg_ref):
        dhv = dh_ref[...]
        sg = _sig(gl_ref[...])
        dgl = (dhv * pp_ref[...] * sg * (1.0 - sg)).astype(BF16)
        dgl_ref[...] = dgl
        dpp_ref[...] = (dhv * sg).astype(BF16)
        dx, dg = _rms_bwd(_dot(dgl, wg_ref[...], NT), h_ref[...], g_ref[...])
        dhn_ref[...] = dhv + dx

        @pl.when(pl.program_id(0) == 0)
        def _():
            dg_ref[...] = dg

        @pl.when(pl.program_id(0) > 0)
        def _():
            dg_ref[...] += dg

    rows = pl.BlockSpec((tm, d), lambda i: (i, 0))
    one = pl.BlockSpec((1, d), lambda i: (0, 0))
    return _call(body, name, (m // tm,), [dh, h, g, gl, pp, w_gate],
                 [rows, rows, one, rows, rows, pl.BlockSpec((d, d), lambda i: (0, 0))],
                 [_sds((m, d), F32), _sds((m, d), BF16), _sds((m, d), BF16), _sds((1, d), F32)],
                 [rows, rows, rows, one], after=after)


def _loss_head(name, h, g, target, tm):
    m, d = h.shape

    def body(h_ref, g_ref, t_ref, loss_ref, dh_ref, dg_ref):
        hv, gv = h_ref[...], g_ref[...]
        r, xh = _rms_stats(hv)
        err = xh * gv - t_ref[...]
        dy = err * (1.0 / d)
        tt = dy * gv
        dh_ref[...] = r * (tt - xh * jnp.mean(tt * xh, axis=-1, keepdims=True))
        dg = jnp.sum(dy * xh, axis=0, keepdims=True)
        part = jnp.zeros((1, LANES), F32) + 0.5 * jnp.sum(jnp.sum(err * err, axis=1, keepdims=True), axis=0, keepdims=True) * (1.0 / d)

        @pl.when(pl.program_id(0) == 0)
        def _():
            dg_ref[...] = dg
            loss_ref[...] = part

        @pl.when(pl.program_id(0) > 0)
        def _():
            dg_ref[...] += dg
            loss_ref[...] += part

    rows = pl.BlockSpec((tm, d), lambda i: (i, 0))
    one = pl.BlockSpec((1, d), lambda i: (0, 0))
    return _call(body, name, (m // tm,), [h, g, target], [rows, one, rows],
                 [_sds((1, LANES), F32), _sds((m, d), F32), _sds((1, d), F32)],
                 [pl.BlockSpec((1, LANES), lambda i: (0, 0)), rows, one])


def _place():
    return lax.axis_index("x"), lax.axis_index("y"), lax.axis_index("c")


def _linear(px, py, pc):
    return 4 * px + 2 * py + pc


def _block_of(ref, idx, rows_per_block):
    if rows_per_block is None:
        return ref.at[idx]
    return ref.at[pl.ds(pl.multiple_of(idx * rows_per_block, 16), rows_per_block), :]


FLIPS = tuple((dx, dy, dc) for dx in (0, 1) for dy in (0, 1) for dc in (0, 1))[1:]
N_PEERS = len(FLIPS)
GATHER_FLIPS = ((0, 0, 1), (1, 0, 0), (0, 1, 0), (1, 1, 0))
HBM = pl.BlockSpec(memory_space=pltpu.HBM)
SEM = pl.BlockSpec(memory_space=pltpu.SEMAPHORE)
SIDE_EFFECT = pltpu.SideEffectType.DATAFLOW_SIDE_EFFECTING


def _peer(x, y, c, flip):
    return (1 - x if flip[0] else x, 1 - y if flip[1] else y, 1 - c if flip[2] else c)


def _split_copies(scatter, srcs, lands, bands, send_sems, recv_sems):
    x, y, c = _place()
    my_idx = _linear(x, y, c)
    flips = FLIPS if scatter else GATHER_FLIPS
    copies = []
    for t in range(len(srcs)):
        for k, flip in enumerate(flips):
            peer = _peer(x, y, c, flip)
            if scatter:
                src, dst = _block_of(srcs[t], _linear(*peer), bands[t]), lands[t].at[k]
            else:
                src, dst = srcs[t], _block_of(lands[t], my_idx, bands[t])
            sem = t * len(flips) + k
            copies.append(pltpu.make_async_remote_copy(
                src_ref=src, dst_ref=dst, send_sem=send_sems.at[sem], recv_sem=recv_sems.at[sem],
                device_id=peer, device_id_type=MESH))
    return copies


def _pass_to_sibling(name, lands, bands):
    n = len(lands)

    def body(*refs):
        zones = refs[n:2 * n]
        send_sems, recv_sems = refs[2 * n:]
        x, y, c = _place()
        chips = [(1 - x, y), (x, 1 - y), (1 - x, 1 - y)]

        def copy(t, j, core):
            blk = _block_of(zones[t], _linear(*chips[j], core), bands[t])
            return pltpu.make_async_remote_copy(
                src_ref=blk, dst_ref=blk, send_sem=send_sems.at[t * 3 + j], recv_sem=recv_sems.at[t * 3 + j],
                device_id=(x, y, 1 - c), device_id_type=MESH)

        sent = [copy(t, j, c) for t in range(n) for j in range(3)]
        for cp in sent:
            cp.start()
        for t in range(n):
            for j in range(3):
                copy(t, j, 1 - c).wait_recv()
        for cp in sent:
            cp.wait_send()

    return pl.pallas_call(
        body, name=name, in_specs=[ANY] * n, out_specs=[ANY] * n, out_shape=[_sds(a.shape, a.dtype) for a in lands],
        input_output_aliases={i: i for i in range(n)},
        scratch_shapes=[pltpu.SemaphoreType.DMA((3 * n,)), pltpu.SemaphoreType.DMA((3 * n,))],
    )(*lands)


def _split_start(name, scatter, srcs, lands, bands, follows):
    n = len(srcs)

    def body(*refs):
        send_sems, recv_sems = refs[2 * n + 1], refs[2 * n + 2]
        for cp in _split_copies(scatter, refs[:n], refs[n:2 * n], bands, send_sems, recv_sems):
            cp.start()
        token = refs[-1]
        token[...] = jnp.zeros_like(token)

    pinned = [pltpu.with_memory_space_constraint(a, pltpu.HBM) for a in (*srcs, *lands)]
    n_sems = n * len(FLIPS if scatter else GATHER_FLIPS)
    outs = pl.pallas_call(
        body, name=name,
        out_shape=(pltpu.SemaphoreType.DMA((n_sems,)), pltpu.SemaphoreType.DMA((n_sems,)),
                   *[pltpu.HBM(a.shape, a.dtype) for a in pinned], _sds((8, LANES), F32)),
        in_specs=[HBM] * (2 * n) + [ANY], out_specs=(SEM, SEM, *[HBM] * (2 * n), pl.BlockSpec(memory_space=pltpu.VMEM)),
        input_output_aliases={i: 2 + i for i in range(2 * n)},
        compiler_params=pltpu.CompilerParams(has_side_effects=SIDE_EFFECT),
    )(*pinned, follows)
    return outs[0], outs[1], outs[2:2 + n], outs[2 + n:2 + 2 * n], outs[-1]


def _split_wait(name, scatter, started, bands, follows):
    send_sems, recv_sems, srcs, lands, _ = started
    n = len(srcs)

    def body(*refs):
        for cp in _split_copies(scatter, refs[:n], refs[n:2 * n], bands, refs[2 * n], refs[2 * n + 1]):
            cp.wait_send()
            cp.wait_recv()

    outs = pl.pallas_call(
        body, name=name, out_shape=tuple(pltpu.HBM(a.shape, a.dtype) for a in (*srcs, *lands)),
        in_specs=[HBM] * (2 * n) + [SEM, SEM, ANY], out_specs=[HBM] * (2 * n),
        input_output_aliases={i: i for i in range(2 * n)},
        compiler_params=pltpu.CompilerParams(has_side_effects=SIDE_EFFECT),
    )(*srcs, *lands, send_sems, recv_sems, follows)
    return outs[:n], outs[n:]


def _own_block_filled(shard, band, my_idx):
    if band is None:
        zone = lax.empty((N_DEV,) + shard.shape, shard.dtype)
        return lax.dynamic_update_slice(zone, shard[None], (my_idx,) + (0,) * shard.ndim)
    zone = lax.empty((N_DEV * band,) + shard.shape[1:], shard.dtype)
    return lax.dynamic_update_slice(zone, shard, (my_idx * band,) + (0,) * (shard.ndim - 1))


def _exchange(name, grads, row_bands, gathered):
    n_items = len(grads)
    outs = []
    for gr, band, whole in zip(grads, row_bands, gathered):
        if whole:
            blk = gr.shape
        elif band is None:
            blk = gr.shape[1:]
        else:
            blk = (band,) + gr.shape[1:]
        outs.append(_sds((N_DEV,) + blk, gr.dtype))
    flips = [(dx, dy, dc) for dx in (0, 1) for dy in (0, 1) for dc in (0, 1)][1:]

    def body(*refs):
        srcs, dsts = refs[:n_items], refs[n_items:2 * n_items]
        send_sems, recv_sems, local_sems = refs[2 * n_items:]
        x, y, c = _place()
        my_idx = _linear(x, y, c)

        def src_of(t, idx):
            return srcs[t] if gathered[t] else _block_of(srcs[t], idx, row_bands[t])

        copies = []
        for t in range(n_items):
            local = pltpu.make_async_copy(src_of(t, my_idx), dsts[t].at[my_idx], local_sems.at[t])
            local.start()
            copies.append(local)
            for k, (dx, dy, dc) in enumerate(flips):
                peer = (1 - x if dx else x, 1 - y if dy else y, 1 - c if dc else c)
                cp = pltpu.make_async_remote_copy(
                    src_ref=src_of(t, _linear(*peer)), dst_ref=dsts[t].at[my_idx],
                    send_sem=send_sems.at[t, k], recv_sem=recv_sems.at[t, k], device_id=peer, device_id_type=MESH)
                cp.start()
                copies.append(cp)
        for cp in copies:
            cp.wait()

    return pl.pallas_call(
        body, name=name, in_specs=[ANY] * n_items, out_specs=[ANY] * n_items, out_shape=outs,
        scratch_shapes=[pltpu.SemaphoreType.DMA((n_items, 7)), pltpu.SemaphoreType.DMA((n_items, 7)),
                        pltpu.SemaphoreType.DMA((n_items,))],
    )(*grads)


def _adam_math(g, w, m, v):
    m2 = ADAM_B1 * m + (1.0 - ADAM_B1) * g
    v2 = ADAM_B2 * v + (1.0 - ADAM_B2) * (g * g)
    m_hat = m2 / (1.0 - ADAM_B1 ** ADAM_STEP)
    v_hat = v2 / (1.0 - ADAM_B2 ** ADAM_STEP)
    return -ADAM_LR * (m_hat / (jnp.sqrt(v_hat) + ADAM_EPS) + ADAM_WD * w), m2, v2


def _adamw(name, recv, grads, band, my_idx, w, m, v, layer, prev):
    n_layers, r, c = w.shape
    tr = r
    for cand in (256, 176, 128):
        if r > cand and r % cand == 0:
            tr = cand
            break
    steps = r // tr

    def body(idx_ref, recv_ref, own_ref, w_ref, m_ref, v_ref, *rest):
        g_ref, d_ref, m2_ref, v2_ref = rest[-4:]
        g = own_ref[...].astype(F32)
        for s in range(N_PEERS):
            g = g + recv_ref[s].astype(F32)
        delta, m2, v2 = _adam_math(g, w_ref[...], m_ref[...], v_ref[...])
        g_ref[...] = g
        d_ref[...] = delta
        m2_ref[...] = m2
        v2_ref[...] = v2

    blk = pl.BlockSpec((None, tr, c), lambda i, idx: (layer, i, 0))
    if band is None:
        own = pl.BlockSpec((None, tr, c), lambda i, idx: (idx[0], i, 0))
    else:
        own = pl.BlockSpec((tr, c), lambda i, idx: (idx[0] * steps + i, 0))
    ins = [recv, grads, w, m, v]
    specs = [pl.BlockSpec((N_PEERS, tr, c), lambda i, idx: (0, i, 0)), own, blk, blk, blk]
    aliases = {}
    if prev is not None:
        ins += list(prev)
        specs += [ANY] * 4
        aliases = {1 + len(ins) - 4 + n: n for n in range(4)}
    return _call(body, name, (steps,), ins, specs, [_sds(w.shape, F32)] * 4, [blk] * 4, aliases=aliases, prefetch=my_idx)


def _adamw_small(name, recv, w, m, v):
    r, c = w.shape

    def body(recv_ref, w_ref, m_ref, v_ref, g_ref, d_ref, m2_ref, v2_ref):
        g = recv_ref[0]
        for s in range(1, N_DEV):
            g = g + recv_ref[s]
        delta, m2, v2 = _adam_math(g, w_ref[...], m_ref[...], v_ref[...])
        g_ref[...] = g
        d_ref[...] = delta
        m2_ref[...] = m2
        v2_ref[...] = v2

    full = pl.BlockSpec((r, c), lambda i: (0, 0))
    return _call(body, name, (1,), [recv, w, m, v], [pl.BlockSpec((N_DEV, r, c), lambda i: (0, 0, 0)), full, full, full],
                 [_sds((r, c), F32)] * 4, [full] * 4)


def _ffn_fwd(tag, h, g, wi, wo, after=()):
    m, d = h.shape
    tm = _row_tile(m, 512)
    n = _rmsnorm_fwd(tag + "_norm", h, g, after)
    tall = _row_tile(m, 1024)
    ab = _mm(tag + "_in", n, wi, grid=(N_DEV, m // tall), dims=NT,
             a_spec=pl.BlockSpec((tall, d), lambda s, i: (i, 0)), b_spec=pl.BlockSpec((None, FF_SHARD, d), lambda s, i: (s, 0, 0)),
             out=_sds((N_DEV, m, FF_SHARD), F32), o_spec=pl.BlockSpec((None, tall, FF_SHARD), lambda s, i: (s, i, 0)))
    act = _swiglu_fwd(tag + "_act", ab)
    nk = N_DEV // 2
    row = pl.BlockSpec((tm, d), lambda i, k: (i, 0))
    h2 = _mm(tag + "_out", act, wo, grid=(m // tm, nk), dims=NN, nk=nk, k_axis=1, alpha=FFN_RES,
             a_spec=pl.BlockSpec((None, tm, FF_SHARD), lambda i, k: (k, i, 0)), b_spec=pl.BlockSpec((FF_SHARD, d), lambda i, k: (k, 0)),
             out=_sds((m, d), F32), o_spec=row, res=h, res_spec=row, acc_shape=(tm, d))
    return h2, (h, n, ab, act)


def _ffn_bwd(tag, dh, saved, g, wi, wo, after=()):
    h, n, ab, act = saved
    m, d = h.shape
    tm = _row_tile(m, 512)
    nk = N_DEV // 2
    ds = _mm(tag + "_dact", dh, wo, grid=(nk, m // tm), dims=NT, alpha=FFN_RES, after=after,
             a_spec=pl.BlockSpec((tm, d), lambda j, i: (i, 0)), b_spec=pl.BlockSpec((FF_SHARD, d), lambda j, i: (j, 0)),
             out=_sds((nk, m, FF_SHARD), F32), o_spec=pl.BlockSpec((None, tm, FF_SHARD), lambda j, i: (j, i, 0)))
    dwo = _mm(tag + "_dwo", act, dh, grid=(nk, m // tm), dims=TN, nk=m // tm, k_axis=1, alpha=FFN_RES,
              a_spec=pl.BlockSpec((None, tm, FF_SHARD), lambda j, k: (j, k, 0)), b_spec=pl.BlockSpec((tm, d), lambda j, k: (k, 0)),
              out=_sds((D_FF, d), BF16), o_spec=pl.BlockSpec((FF_SHARD, d), lambda j, k: (j, 0)), acc_shape=(FF_SHARD, d))
    dab = _swiglu_bwd(tag + "_dab", ab, ds)
    row = pl.BlockSpec((tm, d), lambda i, k: (i, 0))
    one = pl.BlockSpec((1, d), lambda i, k: (0, 0))
    dh2, dg = _mm(tag + "_dn", dab, wi, grid=(m // tm, N_DEV), dims=NN, nk=N_DEV, k_axis=1,
                  a_spec=pl.BlockSpec((None, tm, FF_SHARD), lambda i, k: (k, i, 0)),
                  b_spec=pl.BlockSpec((None, FF_SHARD, d), lambda i, k: (k, 0, 0)),
                  out=_sds((m, d), F32), o_spec=row, rms=(h, g, dh, row, one), acc_shape=(tm, d))
    dwi = _mm(tag + "_dwi", dab, n, grid=(N_DEV, m // tm), dims=TN, nk=m // tm, k_axis=1,
              a_spec=pl.BlockSpec((None, tm, FF_SHARD), lambda s, k: (s, k, 0)), b_spec=pl.BlockSpec((tm, d), lambda s, k: (k, 0)),
              out=_sds((N_DEV, FF_SHARD, d), BF16), o_spec=pl.BlockSpec((None, FF_SHARD, d), lambda s, k: (s, 0, 0)),
              acc_shape=(FF_SHARD, d))
    return dh2, dg, dwi, dwo


def _wgrad(name, a, b, tk):
    m = a.shape[0]
    ka, kb = a.shape[1], b.shape[1]
    return _mm(name, a, b, grid=(m // tk,), dims=TN, nk=m // tk, k_axis=0,
               a_spec=pl.BlockSpec((tk, ka), lambda k: (k, 0)), b_spec=pl.BlockSpec((tk, kb), lambda k: (k, 0)),
               out=_sds((ka, kb), BF16), o_spec=pl.BlockSpec((ka, kb), lambda k: (0, 0)), acc_shape=(ka, kb))


def _mixer_fwd(tag, h, g, w_in_p, bf, conv_w, conv_b, g_conv, w_ao, w_co, w_out):
    m, d = h.shape
    t = _row_tile(m, 256)
    u = _rmsnorm_fwd(tag + "_norm", h, g)
    q, k, v, zf, zc, zg = _inproj(tag + "_inproj", u, w_in_p)
    c = _fgate_fwd(tag + "_fgate", zf, bf)
    ct = c[:, :N_HEADS].T.reshape(N_HEADS, m // t, t).transpose(1, 0, 2)
    o, lse = _attn_fwd(tag + "_attn", q, k, v, c, ct, t)
    cv, y = _conv_fwd(tag + "_conv", zc, conv_w, conv_b, g_conv, t)
    h2, mg, ya, yc = _mixout_fwd(tag + "_mixout", o, cv, zg, h, w_ao, w_co, w_out, t)
    return h2, (h, u, q, k, v, zf, zc, zg, c, ct, o, lse, cv, y, mg, ya, yc)


def _mixer_bwd(tag, dh, saved, g, w_in_p, bf, conv_w, g_conv, w_ao, w_co, w_out, after=()):
    h, u, q, k, v, zf, zc, zg, c, ct, o, lse, cv, y, mg, ya, yc = saved
    m, d = h.shape
    t = _row_tile(m, 256)
    tk = _row_tile(m, 512)
    do, dcv, dzg, dya, dyc = _mixout_bwd(tag + "_dmixout", dh, zg, ya, yc, w_ao, w_co, w_out, t, after)
    d_wout = _wgrad(tag + "_dwout", mg, dh, tk)
    d_wao = _wgrad(tag + "_dwao", o, dya, tk)
    d_wco = _wgrad(tag + "_dwco", cv, dyc, tk)
    dq, dk, dv, dcs, drs = _attn_bwd(tag + "_dattn", q, k, v, c, ct, o, lse, do, t)
    dc = drs[:, :, :2].transpose(1, 0, 2).reshape(m, N_HEADS) - dcs[:, :, :2, :].transpose(0, 2, 1, 3).reshape(N_HEADS, m).T
    dc = jnp.pad(dc, ((0, 0), (0, F_PAD - N_HEADS)))
    dzf, dbf = _fgate_bwd(tag + "_dfgate", dc, zf, bf)
    dzc, dconv_w, dconv_b, dg_conv = _conv_bwd(tag + "_dconv", dcv, y, zc, conv_w, g_conv, t)
    dz = jnp.concatenate([dq.astype(BF16), dk.astype(BF16), dv.astype(BF16), dzf, dzc, dzg], axis=1)
    tm = _row_tile(m, 256)
    row = pl.BlockSpec((tm, d), lambda i: (i, 0))
    one = pl.BlockSpec((1, d), lambda i: (0, 0))
    dh2, dg = _mm(tag + "_du", dz, w_in_p, grid=(m // tm,), dims=NT,
                  a_spec=pl.BlockSpec((tm, IN_PAD), lambda i: (i, 0)), b_spec=pl.BlockSpec((d, IN_PAD), lambda i: (0, 0)),
                  out=_sds((m, d), F32), o_spec=row, rms=(h, g, dh, row, one))
    rt = 256
    d_win_p = _mm(tag + "_dwin", u, dz, grid=(d // rt, m // tk), dims=TN, nk=m // tk, k_axis=1,
                  a_spec=pl.BlockSpec((tk, rt), lambda r, kk: (kk, r)), b_spec=pl.BlockSpec((tk, IN_PAD), lambda r, kk: (kk, 0)),
                  out=_sds((d, IN_PAD), BF16), o_spec=pl.BlockSpec((rt, IN_PAD), lambda r, kk: (r, 0)), acc_shape=(rt, IN_PAD))
    return dh2, dg, d_win_p, dbf, d_wao, dconv_w, dconv_b, dg_conv, d_wco, d_wout


def _col_shards(full, n_cols):
    r = full.shape[0]
    return full.reshape(r, N_DEV, n_cols).transpose(1, 0, 2)


def _from_col_shards(stacked):
    _, r, n = stacked.shape
    return stacked.transpose(1, 0, 2).reshape(r, N_DEV * n)


SMALL_ROWS = 24


def _pack_small(g_ff1, g_mix, g_ff2, g_ple, g_final, conv_b, g_conv, b_f, loss_row):
    n_layers = g_ff1.shape[0]
    bf_row = jnp.pad(b_f.reshape(1, n_layers * N_HEADS), ((0, 0), (0, D_MODEL - n_layers * N_HEADS)))
    parts = [g_ff1, g_mix, g_ff2, g_ple, g_final.reshape(1, D_MODEL), conv_b.reshape(-1, D_MODEL),
             g_conv.reshape(-1, D_MODEL), bf_row, loss_row]
    packed = jnp.concatenate(parts, axis=0)
    return jnp.pad(packed, ((0, SMALL_ROWS - packed.shape[0]), (0, 0)))


def _unpack_small(packed, n_layers):
    ln = n_layers
    cr = n_layers * CONV_CH // D_MODEL
    pos = 4 * ln + 1
    return dict(
        g_ff1=packed[0:ln], g_mix=packed[ln:2 * ln], g_ff2=packed[2 * ln:3 * ln], g_ple=packed[3 * ln:4 * ln],
        g_final=packed[4 * ln], conv_b=packed[pos:pos + cr].reshape(ln, CONV_CH),
        g_conv=packed[pos + cr:pos + 2 * cr].reshape(ln, CONV_CH),
        b_f=packed[pos + 2 * cr, :ln * N_HEADS].reshape(ln, N_HEADS), loss=packed[pos + 2 * cr + 1, 0])


BIG = ("w_ff1_in", "w_ff1_out", "w_in", "w_attn_out", "conv_w", "w_conv_out", "w_out", "w_ff2_in", "w_ff2_out",
       "w_ple_gate", "w_ple_proj")
ROW_BAND = dict(w_ff1_in=None, w_ff1_out=D_FF // N_DEV, w_in=None, w_attn_out=None, conv_w=None, w_conv_out=None,
                w_out=D_MODEL // N_DEV, w_ff2_in=None, w_ff2_out=D_FF // N_DEV, w_ple_gate=D_MODEL // N_DEV, w_ple_proj=None)
FF_IN = ("w_ff1_in", "w_ff2_in")
LAST_PARTS = (("w_ple_gate", "w_ple_proj", "w_ff2_in", "w_ff2_out"),
              ("w_in", "w_attn_out", "conv_w", "w_conv_out", "w_out"), ("w_ff1_in", "w_ff1_out"))
SMALL = ("g_ff1", "g_mix", "g_ff2", "g_ple", "g_final", "conv_b", "g_conv", "b_f")
ORDER = ("g_ff1", "w_ff1_in", "w_ff1_out", "g_mix", "w_in", "b_f", "w_attn_out", "conv_w", "conv_b", "g_conv", "w_conv_out",
         "w_out", "g_ff2", "w_ff2_in", "w_ff2_out", "g_ple", "w_ple_gate", "w_ple_proj", "g_final")


def kernel(x, p, g_ff1, w_ff1_in, w_ff1_out, g_mix, w_in, b_f, w_attn_out, conv_w, conv_b, g_conv, w_conv_out, w_out, g_ff2, w_ff2_in, w_ff2_out, g_ple, w_ple_gate, w_ple_proj, g_final, loss_target, m_g_ff1, m_w_ff1_in, m_w_ff1_out, m_g_mix, m_w_in, m_b_f, m_w_attn_out, m_conv_w, m_conv_b, m_g_conv, m_w_conv_out, m_w_out, m_g_ff2, m_w_ff2_in, m_w_ff2_out, m_g_ple, m_w_ple_gate, m_w_ple_proj, m_g_final, v_g_ff1, v_w_ff1_in, v_w_ff1_out, v_g_mix, v_w_in, v_b_f, v_w_attn_out, v_conv_w, v_conv_b, v_g_conv, v_w_conv_out, v_w_out, v_g_ff2, v_w_ff2_in, v_w_ff2_out, v_g_ple, v_w_ple_gate, v_w_ple_proj, v_g_final):
    local = dict(locals())
    W = {n: local[n] for n in ORDER}
    M1 = {n: local["m_" + n] for n in ORDER}
    V1 = {n: local["v_" + n] for n in ORDER}
    for group in (W, M1, V1):
        for n in FF_IN:
            group[n] = group[n].transpose(0, 2, 1)
    n_layers = g_ff1.shape[0]
    m_rows = x.shape[1]
    t = _row_tile(m_rows, 256)
    my_idx = _linear(*_place()).astype(jnp.int32)
    idx_arr = my_idx.reshape(1)
    bands = [ROW_BAND[n] for n in BIG]

    def gather_start(l, follows):
        shards = [W[n][l].astype(F32 if n == "conv_w" else BF16) for n in BIG]
        zones = [_own_block_filled(s, band, my_idx) for s, band in zip(shards, bands)]
        return _split_start(f"gather_start_{l}", False, shards, zones, bands, follows)

    def weights_of(l, got):
        got = dict(zip(BIG, got))
        w_in_full = _from_col_shards(got["w_in"])
        zeros = jnp.zeros((D_MODEL, F_PAD - N_HEADS), BF16)
        return dict(
            wi1=got["w_ff1_in"], wo1=got["w_ff1_out"], wi2=got["w_ff2_in"], wo2=got["w_ff2_out"],
            w_in_p=jnp.concatenate([w_in_full[:, :P_F + N_HEADS], zeros, w_in_full[:, P_F + N_HEADS:]], axis=1),
            w_ao=_from_col_shards(got["w_attn_out"]), w_co=_from_col_shards(got["w_conv_out"]),
            w_out=got["w_out"], w_pg=got["w_ple_gate"], w_pp=_from_col_shards(got["w_ple_proj"]),
            conv_w=jnp.pad(_from_col_shards(got["conv_w"]), ((0, CONV_HALO - CONV_K), (0, 0))),
            bf=jnp.pad(b_f[l].reshape(1, N_HEADS), ((0, 0), (0, F_PAD - N_HEADS))),
            g1=g_ff1[l].reshape(1, -1), gm=g_mix[l].reshape(1, -1), g2=g_ff2[l].reshape(1, -1), gp=g_ple[l].reshape(1, -1),
            conv_b=conv_b[l].reshape(1, -1), g_conv=g_conv[l].reshape(1, -1))

    h = x[0]
    saved, full = [], []
    flight = gather_start(0, h)
    for l in range(n_layers):
        got = _pass_to_sibling("gather_pass", _split_wait(f"gather_wait_{l}", False, flight, bands, h)[1], bands)
        after = ()
        if l + 1 < n_layers:
            flight = gather_start(l + 1, got[0])
            after = (flight[4],)
        fw = weights_of(l, got)
        full.append(fw)
        h, s1 = _ffn_fwd("ff1", h, fw["g1"], fw["wi1"], fw["wo1"], after)
        h, s2 = _mixer_fwd("mix", h, fw["gm"], fw["w_in_p"], fw["bf"], fw["conv_w"], fw["conv_b"], fw["g_conv"],
                           fw["w_ao"], fw["w_co"], fw["w_out"])
        h, s3 = _ffn_fwd("ff2", h, fw["g2"], fw["wi2"], fw["wo2"])
        h_in = h
        pl_in = p[l, 0]
        h, n_ple, gl, pp = _ple_fwd("ple", h, pl_in, fw["gp"], fw["w_pg"], fw["w_pp"], t)
        saved.append((s1, s2, s3, (h_in, pl_in, n_ple, gl, pp)))

    loss_row, dh, dg_final = _loss_head("loss_head", h, g_final.reshape(1, -1), loss_target[0], t)

    small_grads = {n: [None] * n_layers for n in ("g_ff1", "g_mix", "g_ff2", "g_ple", "conv_b", "g_conv", "b_f")}
    stacked = {n: None for n in BIG}
    tk = _row_tile(m_rows, 512)

    def start_ready(l, grads, parts_left, follows):
        started, token = [], ()
        for part in [names for names in parts_left if all(n in grads for n in names)]:
            parts_left.remove(part)
            part_bands = [ROW_BAND[n] for n in part]
            terms = [grads[n] for n in part]
            zones = [lax.empty((N_PEERS,) + (g.shape[1:] if band is None else (band,) + g.shape[1:]), g.dtype)
                     for g, band in zip(terms, part_bands)]
            flight = _split_start(f"scatter_start_{l}_{part[0]}", True, terms, zones, part_bands, follows)
            started.append((l, part, part_bands, flight))
            token = (flight[4],)
        return started, token

    def apply(item, follows):
        l, part, part_bands, flight = item
        terms, recv = _split_wait(f"scatter_wait_{l}_{part[0]}", True, flight, part_bands, follows)
        for n, own, rc, band in zip(part, terms, recv, part_bands):
            stacked[n] = _adamw("adamw_" + n, rc, own, band, idx_arr, W[n], M1[n], V1[n], l, stacked[n])

    pending, after = [], ()
    for l in reversed(range(n_layers)):
        parts_left = list(LAST_PARTS if l == 0 else (BIG,))
        mine, grads = [], {}
        fw = full[l]
        s1, s2, s3, (h_in, pl_in, n_ple, gl, pp) = saved[l]
        dh, dgl, dpp, dgp = _ple_bwd("ple_bwd", dh, h_in, fw["gp"], gl, pp, fw["w_pg"], t, after)
        d_wpg = _wgrad("ple_dwgate", n_ple, dgl, tk)
        d_wpp = _wgrad("ple_dwproj", pl_in, dpp, tk)
        dh, dg2, d_wi2, d_wo2 = _ffn_bwd("ff2", dh, s3, fw["g2"], fw["wi2"], fw["wo2"])
        grads.update(w_ple_gate=d_wpg, w_ple_proj=_col_shards(d_wpp, D_MODEL // N_DEV), w_ff2_in=d_wi2, w_ff2_out=d_wo2)
        new, token = start_ready(l, grads, parts_left, dh)
        mine += new
        dh, dgm, d_win_p, dbf, d_wao, dconv_w, dconv_b, dg_conv, d_wco, d_wout = _mixer_bwd(
            "mix", dh, s2, fw["gm"], fw["w_in_p"], fw["bf"], fw["conv_w"], fw["g_conv"], fw["w_ao"], fw["w_co"], fw["w_out"],
            token)
        d_win = jnp.concatenate([d_win_p[:, :P_F + N_HEADS], d_win_p[:, P_C:]], axis=1)
        grads.update(w_in=_col_shards(d_win, IN_SHARD), w_attn_out=_col_shards(d_wao, D_MODEL // N_DEV),
                     conv_w=_col_shards(dconv_w[:CONV_K], CONV_CH // N_DEV), w_conv_out=_col_shards(d_wco, D_MODEL // N_DEV),
                     w_out=d_wout)
        new, token = start_ready(l, grads, parts_left, dh)
        mine += new
        dh, dg1, d_wi1, d_wo1 = _ffn_bwd("ff1", dh, s1, fw["g1"], fw["wi1"], fw["wo1"], token)
        grads.update(w_ff1_in=d_wi1, w_ff1_out=d_wo1)
        small_grads["g_ff1"][l], small_grads["g_mix"][l], small_grads["g_ff2"][l], small_grads["g_ple"][l] = dg1, dgm, dg2, dgp
        small_grads["conv_b"][l], small_grads["g_conv"][l] = dconv_b, dg_conv
        small_grads["b_f"][l] = dbf[:, :N_HEADS]
        for item in pending:
            apply(item, dh)
        new, after = start_ready(l, grads, parts_left, dh)
        pending = mine + new
    for item in pending:
        apply(item, dh)

    cat = {n: jnp.concatenate(small_grads[n], axis=0) for n in small_grads}
    g_pack = _pack_small(cat["g_ff1"], cat["g_mix"], cat["g_ff2"], cat["g_ple"], dg_final, cat["conv_b"], cat["g_conv"],
                         cat["b_f"], loss_row[:, :1] * jnp.ones((1, D_MODEL), F32))
    zero_row = jnp.zeros((1, D_MODEL), F32)
    packs = [_pack_small(*[src[n] for n in ("g_ff1", "g_mix", "g_ff2", "g_ple", "g_final", "conv_b", "g_conv", "b_f")], zero_row)
             for src in (W, M1, V1)]
    (recv_small,) = _exchange("gather_small", [g_pack], [None], [True])
    outs_small = [_unpack_small(a, n_layers) for a in _adamw_small("adamw_small", recv_small, *packs)]

    def pick(kind, n):
        if n in SMALL:
            return outs_small[kind][n]
        return stacked[n][kind].transpose(0, 2, 1) if n in FF_IN else stacked[n][kind]

    result = [outs_small[0]["loss"], dh.reshape(x.shape)]
    for kind in range(4):
        result += [pick(kind, n) for n in ORDER]
    return tuple(result)
```

```python
import functools
import math

import jax
import jax.numpy as jnp
from jax import lax
from jax.experimental import pallas as pl
from jax.experimental.pallas import tpu as pltpu

F32 = jnp.float32
BF16 = jnp.bfloat16

N_DEV = 8
D_MODEL = 1024
N_HEADS = 8
HEAD_DIM = 64
ATTN_W = N_HEADS * HEAD_DIM
CONV_CH = 512
CONV_K = 31
CONV_HALO = 32
D_FF = 2816
FF_SHARD = 2 * D_FF // N_DEV
D_PLE = 256
EPS = 1e-6
FFN_RES = 0.5
LANES = 128
IN_COLS = 3 * ATTN_W + N_HEADS + 2 * CONV_CH + 2 * D_MODEL
IN_SHARD = IN_COLS // N_DEV
F_PAD = LANES
P_Q, P_K, P_V = 0, ATTN_W, 2 * ATTN_W
P_F = 3 * ATTN_W
P_C = P_F + F_PAD
P_G = P_C + 2 * CONV_CH
IN_PAD = P_G + 2 * D_MODEL
NEG_BIG = -1e30
ATTN_TILE = 256

ADAM_LR, ADAM_B1, ADAM_B2, ADAM_EPS, ADAM_WD, ADAM_STEP = 0.001, 0.9, 0.999, 1e-08, 0.01, 10

VMEM_CAP = 60 * 1024 * 1024
VMEM_SLACK = 12 * 1024 * 1024

NN = (((1,), (0,)), ((), ()))
NT = (((1,), (1,)), ((), ()))
TN = (((0,), (0,)), ((), ()))

MESH = pl.DeviceIdType.MESH
ANY = pl.BlockSpec(memory_space=pl.ANY)


def _nbytes(shape, dtype):
    return math.prod(d for d in shape if d is not None) * jnp.dtype(dtype).itemsize


def _params(block_bytes, n_axes):
    limit = min(VMEM_CAP, 2 * block_bytes + VMEM_SLACK)
    return pltpu.CompilerParams(dimension_semantics=("arbitrary",) * n_axes, vmem_limit_bytes=limit)


def _call(body, name, grid, in_arrays, in_specs, out_shapes, out_specs, scratch=(), aliases=None, extra_bytes=0,
          after=(), prefetch=None):
    total = extra_bytes
    for a, s in zip(in_arrays, in_specs):
        if s.block_shape is not None:
            total += _nbytes(s.block_shape, a.dtype)
    for o, s in zip(out_shapes, out_specs):
        if s.block_shape is not None:
            total += _nbytes(s.block_shape, o.dtype)
    n_in, n_after = len(in_arrays), len(after)
    lead = 0 if prefetch is None else 1

    def with_after(*refs):
        return body(*refs[:lead + n_in], *refs[lead + n_in + n_after:])

    operands = [pltpu.with_memory_space_constraint(a, pltpu.HBM) for a in (*in_arrays, *after)]
    specs = list(in_specs) + [ANY] * n_after
    common = dict(name=name, out_shape=[pltpu.HBM(o.shape, o.dtype) for o in out_shapes], input_output_aliases=aliases or {},
                  compiler_params=_params(total, len(grid)))
    if prefetch is None:
        return pl.pallas_call(with_after if n_after else body, grid=grid, in_specs=specs, out_specs=list(out_specs),
                              scratch_shapes=list(scratch), **common)(*operands)
    grid_spec = pltpu.PrefetchScalarGridSpec(num_scalar_prefetch=1, grid=grid, in_specs=specs, out_specs=list(out_specs),
                                             scratch_shapes=list(scratch))
    return pl.pallas_call(with_after if n_after else body, grid_spec=grid_spec, **common)(prefetch, *operands)


def _sig(x):
    return 1.0 / (1.0 + jnp.exp(-x))


def _dot(a, b, dims):
    return lax.dot_general(a.astype(BF16), b.astype(BF16), dims, preferred_element_type=F32)


def _rms_stats(x):
    r = lax.rsqrt(jnp.mean(x * x, axis=-1, keepdims=True) + EPS)
    return r, x * r


def _rms_bwd(dn, x, g):
    r, xh = _rms_stats(x)
    t = dn * g
    dx = r * (t - xh * jnp.mean(t * xh, axis=-1, keepdims=True))
    return dx, jnp.sum(dn * xh, axis=0, keepdims=True)


def _row_tile(m, want):
    t = min(m, want)
    assert m % t == 0
    return t


def _mm(name, a, b, *, grid, a_spec, b_spec, out, o_spec, dims, nk=None, k_axis=None, alpha=1.0,
        res=None, res_spec=None, rms=None, acc_shape=None, after=()):
    first_axes = len(grid)

    def body(*refs):
        refs = list(refs)
        a_ref, b_ref = refs[:2]
        pos = 2
        if res is not None:
            res_ref = refs[pos]
            pos += 1
        if rms is not None:
            h_ref, g_ref, dres_ref = refs[pos:pos + 3]
            pos += 3
        o_ref = refs[pos]
        pos += 1
        if rms is not None:
            dg_ref = refs[pos]
            pos += 1
        acc_ref = refs[pos] if k_axis is not None else None
        ids = [pl.program_id(ax) for ax in range(first_axes)]
        row_axes = [ids[ax] == 0 for ax in range(first_axes) if ax != k_axis]
        is_first = functools.reduce(jnp.logical_and, row_axes) if row_axes else None

        part = _dot(a_ref[...], b_ref[...], dims)

        def finish(acc):
            val = acc if alpha == 1.0 else acc * alpha
            if res is not None:
                val = val + res_ref[...]
            if rms is not None:
                dx, dg = _rms_bwd(val, h_ref[...], g_ref[...])
                o_ref[...] = (dres_ref[...] + dx).astype(o_ref.dtype)

                @pl.when(is_first)
                def _():
                    dg_ref[...] = dg

                @pl.when(jnp.logical_not(is_first))
                def _():
                    dg_ref[...] += dg
            else:
                o_ref[...] = val.astype(o_ref.dtype)

        if k_axis is None:
            finish(part)
        else:
            k = ids[k_axis]

            @pl.when(k == 0)
            def _():
                acc_ref[...] = part

            @pl.when(k > 0)
            def _():
                acc_ref[...] += part

            @pl.when(k == nk - 1)
            def _():
                finish(acc_ref[...])

    in_arrays, in_specs = [a, b], [a_spec, b_spec]
    if res is not None:
        in_arrays.append(res)
        in_specs.append(res_spec)
    outs, o_specs = [out], [o_spec]
    if rms is not None:
        h, g, dres, row_spec, g_spec = rms
        in_arrays += [h, g, dres]
        in_specs += [row_spec, g_spec, row_spec]
        outs.append(jax.ShapeDtypeStruct(g.shape, F32))
        o_specs.append(g_spec)
    scratch, extra = [], 0
    if k_axis is not None:
        scratch = [pltpu.VMEM(acc_shape, F32)]
        extra = _nbytes(acc_shape, F32)
    res_out = _call(body, name, grid, in_arrays, in_specs, outs, o_specs, scratch, extra_bytes=extra, after=after)
    return res_out if rms is not None else res_out[0]


def _sds(shape, dtype):
    return jax.ShapeDtypeStruct(shape, dtype)


def _rmsnorm_fwd(name, h, g, after=()):
    m, d = h.shape
    tm = _row_tile(m, 512)

    def body(h_ref, g_ref, o_ref):
        _, xh = _rms_stats(h_ref[...])
        o_ref[...] = (xh * g_ref[...]).astype(BF16)

    row = pl.BlockSpec((tm, d), lambda i: (i, 0))
    return _call(body, name, (m // tm,), [h, g], [row, pl.BlockSpec((1, d), lambda i: (0, 0))],
                 [_sds((m, d), BF16)], [row], after=after)[0]


def _swiglu_fwd(name, ab):
    _, m, w = ab.shape
    half = N_DEV // 2
    tm = _row_tile(m, 512)
    ab4 = ab.reshape(2, half, m, w)

    def body(ab_ref, o_ref):
        a, b = ab_ref[0], ab_ref[1]
        o_ref[...] = (a * _sig(a) * b).astype(BF16)

    return _call(body, name, (half, m // tm), [ab4],
                 [pl.BlockSpec((2, None, tm, w), lambda j, i: (0, j, i, 0))],
                 [_sds((half, m, w), BF16)], [pl.BlockSpec((None, tm, w), lambda j, i: (j, i, 0))])[0]


def _swiglu_bwd(name, ab, ds):
    _, m, w = ab.shape
    half = N_DEV // 2
    tm = _row_tile(m, 512)
    ab4 = ab.reshape(2, half, m, w)

    def body(ab_ref, ds_ref, o_ref):
        a, b, d = ab_ref[0], ab_ref[1], ds_ref[...]
        sg = _sig(a)
        o_ref[0] = (d * b * sg * (1.0 + a * (1.0 - sg))).astype(BF16)
        o_ref[1] = (d * a * sg).astype(BF16)

    blk = pl.BlockSpec((2, None, tm, w), lambda j, i: (0, j, i, 0))
    out = _call(body, name, (half, m // tm), [ab4, ds],
                [blk, pl.BlockSpec((None, tm, w), lambda j, i: (j, i, 0))],
                [_sds((2, half, m, w), BF16)], [blk])[0]
    return out.reshape(N_DEV, m, w)


def _inproj(name, u, w_in_p):
    m, d = u.shape
    tm = _row_tile(m, 256)

    def body(u_ref, w_ref, q_ref, k_ref, v_ref, f_ref, c_ref, g_ref):
        z = _dot(u_ref[...], w_ref[...], NN)
        q_ref[...] = z[:, P_Q:P_K].astype(BF16)
        k_ref[...] = z[:, P_K:P_V].astype(BF16)
        v_ref[...] = z[:, P_V:P_F].astype(BF16)
        f_ref[...] = z[:, P_F:P_C]
        c_ref[...] = z[:, P_C:P_G]
        g_ref[...] = z[:, P_G:IN_PAD]

    def rows(width):
        return pl.BlockSpec((tm, width), lambda i: (i, 0))

    widths = (ATTN_W, ATTN_W, ATTN_W, F_PAD, 2 * CONV_CH, 2 * D_MODEL)
    dtypes = (BF16, BF16, BF16, F32, F32, F32)
    return _call(body, name, (m // tm,), [u, w_in_p], [rows(d), pl.BlockSpec((d, IN_PAD), lambda i: (0, 0))],
                 [_sds((m, wd), dt) for wd, dt in zip(widths, dtypes)], [rows(wd) for wd in widths])


def _cumsum_rows(x, reverse):
    m = x.shape[0]
    row = lax.broadcasted_iota(jnp.int32, x.shape, 0)
    sh = 1
    while sh < m:
        if reverse:
            x = x + jnp.where(row < m - sh, pltpu.roll(x, m - sh, axis=0), 0.0)
        else:
            x = x + jnp.where(row >= sh, pltpu.roll(x, sh, axis=0), 0.0)
        sh *= 2
    return x


def _log_sigmoid(x):
    return jnp.minimum(x, 0.0) - jnp.log(1.0 + jnp.exp(-jnp.abs(x)))


def _fgate_fwd(name, zf, bf):
    m, w = zf.shape

    def body(z_ref, b_ref, c_ref):
        c_ref[...] = _cumsum_rows(_log_sigmoid(z_ref[...] + b_ref[...]), reverse=False)

    full = pl.BlockSpec((m, w), lambda i: (0, 0))
    return _call(body, name, (1,), [zf, bf], [full, pl.BlockSpec((1, w), lambda i: (0, 0))], [_sds((m, w), F32)], [full])[0]


def _fgate_bwd(name, dc, zf, bf):
    m, w = zf.shape

    def body(dc_ref, z_ref, b_ref, dz_ref, db_ref):
        dls = _cumsum_rows(dc_ref[...], reverse=True)
        dz = dls * _sig(-(z_ref[...] + b_ref[...]))
        lane = lax.broadcasted_iota(jnp.int32, dz.shape, 1)
        dz = jnp.where(lane < N_HEADS, dz, 0.0)
        dz_ref[...] = dz.astype(BF16)
        db_ref[...] = jnp.sum(dz, axis=0, keepdims=True)

    full = pl.BlockSpec((m, w), lambda i: (0, 0))
    one = pl.BlockSpec((1, w), lambda i: (0, 0))
    return _call(body, name, (1,), [dc, zf, bf], [full, full, one], [_sds((m, w), BF16), _sds((1, w), F32)], [full, one])


def _lane_pick(x, idx):
    lane = lax.broadcasted_iota(jnp.int32, x.shape, 1)
    return jnp.sum(jnp.where(lane == idx, x, 0.0), axis=1, keepdims=True)


def _row_pick(x, idx):
    sub = lax.broadcasted_iota(jnp.int32, x.shape, 0)
    return jnp.sum(jnp.where(sub == idx, x, 0.0), axis=0, keepdims=True)


def _attn_fwd(name, q, k, v, ct, t):
    m = q.shape[0]
    n_chunks = m // t
    pairs = N_HEADS // 2
    scale = 1.0 / math.sqrt(HEAD_DIM)

    def body(q_ref, k_ref, v_ref, ct_ref, o_ref, lse_ref):
        p = pl.program_id(0)
        i = pl.program_id(1)
        lane = lax.broadcasted_iota(jnp.int32, (t, LANES), 1)
        first = lane < HEAD_DIM
        q2 = q_ref[...] * scale
        zero = jnp.zeros_like(q2)
        q_heads = (jnp.where(first, q2, zero), jnp.where(first, zero, q2))
        below = lax.broadcasted_iota(jnp.int32, (t, t), 0) >= lax.broadcasted_iota(jnp.int32, (t, t), 1)

        def step(j, carry, diagonal):
            off = pl.multiple_of(j * t, t)
            kj = k_ref[pl.ds(off, t), :]
            vj = v_ref[pl.ds(off, t), :]
            ctj = ct_ref[j]
            new = []
            for e in range(2):
                mx, den, acc = carry[3 * e:3 * e + 3]
                s = _dot(q_heads[e], kj, NT) - _row_pick(ctj, 2 * p + e)
                if diagonal:
                    s = jnp.where(below, s, NEG_BIG)
                mx2 = jnp.maximum(mx, jnp.max(s, axis=1, keepdims=True))
                corr = jnp.exp(mx - mx2)
                pe = jnp.exp(s - mx2)
                new += [mx2, corr * den + jnp.sum(pe, axis=1, keepdims=True), corr * acc + _dot(pe, vj, NN)]
            return tuple(new)

        col = jnp.full((t, 1), NEG_BIG, F32), jnp.zeros((t, 1), F32), jnp.zeros((t, LANES), F32)
        fin = step(i, lax.fori_loop(0, i, lambda j, carry: step(j, carry, False), col + col), True)
        o_ref[...] = jnp.where(first, fin[2] / fin[1], fin[5] / fin[4]).astype(BF16)
        lse_a = fin[0] + jnp.log(fin[1])
        lse_b = fin[3] + jnp.log(fin[4])
        lse_ref[...] = jnp.where(lane == 0, lse_a, jnp.where(lane == 1, lse_b, 0.0))

    seq = pl.BlockSpec((m, LANES), lambda p, i: (0, p))
    blk = pl.BlockSpec((t, LANES), lambda p, i: (i, p))
    return _call(body, name, (pairs, n_chunks), [q, k, v, ct],
                 [blk, seq, seq, pl.BlockSpec((n_chunks, 8, t), lambda p, i: (0, 0, 0))],
                 [_sds((m, ATTN_W), BF16), _sds((pairs, m, LANES), F32)],
                 [blk, pl.BlockSpec((None, t, LANES), lambda p, i: (p, i, 0))])


def _attn_bwd(name, q, k, v, ct, o, lse, do, t):
    m = q.shape[0]
    n_chunks = m // t
    pairs = N_HEADS // 2
    scale = 1.0 / math.sqrt(HEAD_DIM)

    def body(q_ref, do_ref, o_ref, lse_ref, ct_ref, k_ref, v_ref, dq_ref, dk_ref, dv_ref, dcs_ref, drs_ref):
        p = pl.program_id(0)
        j = pl.program_id(1)

        @pl.when(j == 0)
        def _():
            dq_ref[...] = jnp.zeros_like(dq_ref)
            drs_ref[...] = jnp.zeros_like(drs_ref)

        lane = lax.broadcasted_iota(jnp.int32, (t, LANES), 1)
        first = lane < HEAD_DIM
        kj, vj = k_ref[...] * scale, v_ref[...]
        zero = jnp.zeros_like(kj)
        k_heads = (jnp.where(first, kj, zero), jnp.where(first, zero, kj))
        v_heads = (jnp.where(first, vj, zero), jnp.where(first, zero, vj))
        ctj = ct_ref[j]
        c_rows = (_row_pick(ctj, 2 * p), _row_pick(ctj, 2 * p + 1))
        below = lax.broadcasted_iota(jnp.int32, (t, t), 0) >= lax.broadcasted_iota(jnp.int32, (t, t), 1)

        def step(i, carry, diagonal):
            off = pl.multiple_of(i * t, t)
            qi = q_ref[pl.ds(off, t), :]
            doi = do_ref[pl.ds(off, t), :]
            prod = doi * o_ref[pl.ds(off, t), :].astype(F32)
            lsei = lse_ref[pl.ds(off, t), :]
            dob = doi.astype(BF16)
            new = []
            dq = jnp.zeros((t, LANES), F32)
            drow = jnp.zeros((t, LANES), F32)
            for e in range(2):
                dk, dv, dcol = carry[3 * e:3 * e + 3]
                delta = jnp.sum(jnp.where(first if e == 0 else jnp.logical_not(first), prod, 0.0), axis=1, keepdims=True)
                pe = jnp.exp(_dot(qi, k_heads[e], NT) - c_rows[e] - _lane_pick(lsei, e))
                if diagonal:
                    pe = jnp.where(below, pe, 0.0)
                dsc = pe * (_dot(dob, v_heads[e], NT) - delta)
                dsb = dsc.astype(BF16)
                dq = dq + _dot(dsb, k_heads[e], NN)
                drow = jnp.where(lane == e, jnp.sum(dsc, axis=1, keepdims=True), drow)
                new += [dk + _dot(dsb, qi, TN), dv + _dot(pe, dob, TN), dcol + jnp.sum(dsc, axis=0, keepdims=True)]
            dq_ref[pl.ds(off, t), :] += dq
            drs_ref[pl.ds(off, t), :] += drow
            return tuple(new)

        z = jnp.zeros((t, LANES), F32), jnp.zeros((t, LANES), F32), jnp.zeros((1, t), F32)
        fin = lax.fori_loop(j + 1, n_chunks, lambda i, carry: step(i, carry, False), step(j, z + z, True))
        dk_ref[...] = jnp.where(first, fin[0], fin[3]) * scale
        dv_ref[...] = jnp.where(first, fin[1], fin[4])
        sub = lax.broadcasted_iota(jnp.int32, (8, t), 0)
        dcs_ref[...] = jnp.where(sub == 0, fin[2], jnp.where(sub == 1, fin[5], 0.0))

    seq = pl.BlockSpec((m, LANES), lambda p, j: (0, p))
    blk = pl.BlockSpec((t, LANES), lambda p, j: (j, p))
    return _call(body, name, (pairs, n_chunks), [q, do, o, lse, ct, k, v],
                 [seq, seq, seq, pl.BlockSpec((None, m, LANES), lambda p, j: (p, 0, 0)),
                  pl.BlockSpec((n_chunks, 8, t), lambda p, j: (0, 0, 0)), blk, blk],
                 [_sds((m, ATTN_W), F32), _sds((m, ATTN_W), F32), _sds((m, ATTN_W), F32),
                  _sds((pairs, n_chunks, 8, t), F32), _sds((pairs, m, LANES), F32)],
                 [seq, blk, blk, pl.BlockSpec((None, None, 8, t), lambda p, j: (p, j, 0, 0)),
                  pl.BlockSpec((None, m, LANES), lambda p, j: (p, 0, 0))])


def _glu(z):
    return z[:, :CONV_CH] * _sig(z[:, CONV_CH:])


def _shifted(x, lead, tm):
    n = x.shape[0]
    return pltpu.roll(x, (n - lead) % n, axis=0)[:tm]


def _conv_fwd(name, zc, w, b, g, tm):
    m = zc.shape[0]
    hb = tm // CONV_HALO

    def body(cur_ref, prev_ref, w_ref, b_ref, g_ref, cv_ref, y_ref):
        i = pl.program_id(0)
        a_prev = jnp.where(i > 0, _glu(prev_ref[...]), 0.0)
        af = jnp.concatenate([a_prev, _glu(cur_ref[...])], axis=0)
        y = jnp.zeros((tm, CONV_CH), F32)
        for tap in range(CONV_K):
            y = y + w_ref[pl.ds(tap, 1), :] * _shifted(af, CONV_HALO - (CONV_K - 1) + tap, tm)
        y = y + b_ref[...]
        y_ref[...] = y
        _, xh = _rms_stats(y)
        rn = xh * g_ref[...]
        cv_ref[...] = (rn * _sig(rn)).astype(BF16)

    one = pl.BlockSpec((1, CONV_CH), lambda i: (0, 0))
    out = pl.BlockSpec((tm, CONV_CH), lambda i: (i, 0))
    return _call(body, name, (m // tm,), [zc, zc, w, b, g],
                 [pl.BlockSpec((tm, 2 * CONV_CH), lambda i: (i, 0)),
                  pl.BlockSpec((CONV_HALO, 2 * CONV_CH), lambda i: (jnp.maximum(i * hb - 1, 0), 0)),
                  pl.BlockSpec((CONV_HALO, CONV_CH), lambda i: (0, 0)), one, one],
                 [_sds((m, CONV_CH), BF16), _sds((m, CONV_CH), F32)], [out, out])


def _conv_bwd(name, dcv, y, zc, w, g, tm):
    m = zc.shape[0]
    hb = tm // CONV_HALO
    n_blocks = m // tm

    def body(dcv_ref, dcvn_ref, y_ref, yn_ref, cur_ref, prev_ref, w_ref, g_ref, dz_ref, dw_ref, db_ref, dg_ref):
        i = pl.program_id(0)
        gv = g_ref[...]

        def dy_of(d, yv):
            r, xh = _rms_stats(yv)
            rn = xh * gv
            sg = _sig(rn)
            drn = d * sg * (1.0 + rn * (1.0 - sg))
            tt = drn * gv
            return r * (tt - xh * jnp.mean(tt * xh, axis=-1, keepdims=True)), drn * xh

        dy, dgt = dy_of(dcv_ref[...], y_ref[...])
        dy_next, _ = dy_of(dcvn_ref[...], yn_ref[...])
        dyf = jnp.concatenate([dy, jnp.where(i < n_blocks - 1, dy_next, 0.0)], axis=0)
        cur = cur_ref[...]
        af = jnp.concatenate([jnp.where(i > 0, _glu(prev_ref[...]), 0.0), _glu(cur)], axis=0)

        @pl.when(i == 0)
        def _():
            dw_ref[...] = jnp.zeros_like(dw_ref)
            db_ref[...] = jnp.zeros_like(db_ref)
            dg_ref[...] = jnp.zeros_like(dg_ref)

        da = jnp.zeros((tm, CONV_CH), F32)
        for tap in range(CONV_K):
            da = da + w_ref[pl.ds(tap, 1), :] * _shifted(dyf, CONV_K - 1 - tap, tm)
            a_tap = _shifted(af, CONV_HALO - (CONV_K - 1) + tap, tm)
            dw_ref[pl.ds(tap, 1), :] += jnp.sum(dy * a_tap, axis=0, keepdims=True)
        db_ref[...] += jnp.sum(dy, axis=0, keepdims=True)
        dg_ref[...] += jnp.sum(dgt, axis=0, keepdims=True)
        c1, sg2 = cur[:, :CONV_CH], _sig(cur[:, CONV_CH:])
        dz_ref[:, :CONV_CH] = (da * sg2).astype(BF16)
        dz_ref[:, CONV_CH:] = (da * c1 * sg2 * (1.0 - sg2)).astype(BF16)

    one = pl.BlockSpec((1, CONV_CH), lambda i: (0, 0))
    taps = pl.BlockSpec((CONV_HALO, CONV_CH), lambda i: (0, 0))
    row = pl.BlockSpec((tm, CONV_CH), lambda i: (i, 0))
    nxt = pl.BlockSpec((CONV_HALO, CONV_CH), lambda i: (jnp.minimum((i + 1) * hb, m // CONV_HALO - 1), 0))
    row2 = pl.BlockSpec((tm, 2 * CONV_CH), lambda i: (i, 0))
    return _call(body, name, (n_blocks,), [dcv, dcv, y, y, zc, zc, w, g],
                 [row, nxt, row, nxt, row2,
                  pl.BlockSpec((CONV_HALO, 2 * CONV_CH), lambda i: (jnp.maximum(i * hb - 1, 0), 0)), taps, one],
                 [_sds((m, 2 * CONV_CH), BF16), _sds((CONV_HALO, CONV_CH), F32), _sds((1, CONV_CH), F32), _sds((1, CONV_CH), F32)],
                 [row2, taps, one, one])


def _mixout_fwd(name, o, cv, zg, h, w_ao, w_co, w_out, tm):
    m, d = h.shape

    def body(o_ref, cv_ref, zg_ref, h_ref, wa_ref, wc_ref, wo_ref, hn_ref, mg_ref, ya_ref, yc_ref):
        ya = _dot(o_ref[...], wa_ref[...], NN)
        yc = _dot(cv_ref[...], wc_ref[...], NN)
        zg_v = zg_ref[...]
        mg = (_sig(zg_v[:, :d]) * ya + _sig(zg_v[:, d:]) * yc).astype(BF16)
        ya_ref[...] = ya
        yc_ref[...] = yc
        mg_ref[...] = mg
        hn_ref[...] = h_ref[...] + _dot(mg, wo_ref[...], NN)

    def rows(wd):
        return pl.BlockSpec((tm, wd), lambda i: (i, 0))

    def whole(a):
        return pl.BlockSpec(a.shape, lambda i: (0, 0))

    return _call(body, name, (m // tm,), [o, cv, zg, h, w_ao, w_co, w_out],
                 [rows(ATTN_W), rows(CONV_CH), rows(2 * d), rows(d), whole(w_ao), whole(w_co), whole(w_out)],
                 [_sds((m, d), F32), _sds((m, d), BF16), _sds((m, d), F32), _sds((m, d), F32)],
                 [rows(d), rows(d), rows(d), rows(d)])


def _mixout_bwd(name, dh, zg, ya, yc, w_ao, w_co, w_out, tm, after=()):
    m, d = dh.shape

    def body(dh_ref, zg_ref, ya_ref, yc_ref, wa_ref, wc_ref, wo_ref, do_ref, dcv_ref, dzg_ref, dya_ref, dyc_ref):
        dm = _dot(dh_ref[...], wo_ref[...], NT)
        zg_v = zg_ref[...]
        sa, sc = _sig(zg_v[:, :d]), _sig(zg_v[:, d:])
        dya = (dm * sa).astype(BF16)
        dyc = (dm * sc).astype(BF16)
        dzg_ref[:, :d] = (dm * ya_ref[...] * sa * (1.0 - sa)).astype(BF16)
        dzg_ref[:, d:] = (dm * yc_ref[...] * sc * (1.0 - sc)).astype(BF16)
        dya_ref[...] = dya
        dyc_ref[...] = dyc
        do_ref[...] = _dot(dya, wa_ref[...], NT)
        dcv_ref[...] = _dot(dyc, wc_ref[...], NT)

    def rows(wd):
        return pl.BlockSpec((tm, wd), lambda i: (i, 0))

    def whole(a):
        return pl.BlockSpec(a.shape, lambda i: (0, 0))

    return _call(body, name, (m // tm,), [dh, zg, ya, yc, w_ao, w_co, w_out],
                 [rows(d), rows(2 * d), rows(d), rows(d), whole(w_ao), whole(w_co), whole(w_out)],
                 [_sds((m, ATTN_W), F32), _sds((m, CONV_CH), F32), _sds((m, 2 * d), BF16), _sds((m, d), BF16), _sds((m, d), BF16)],
                 [rows(ATTN_W), rows(CONV_CH), rows(2 * d), rows(d), rows(d)], after=after)


def _ple_fwd(name, h, p, g, w_gate, w_proj, tm):
    m, d = h.shape

    def body(h_ref, p_ref, g_ref, wg_ref, wp_ref, hn_ref, n_ref, gl_ref, pp_ref):
        hv = h_ref[...]
        _, xh = _rms_stats(hv)
        n = (xh * g_ref[...]).astype(BF16)
        gl = _dot(n, wg_ref[...], NN)
        pp = _dot(p_ref[...], wp_ref[...], NN)
        n_ref[...] = n
        gl_ref[...] = gl
        pp_ref[...] = pp
        hn_ref[...] = hv + _sig(gl) * pp

    def rows(wd):
        return pl.BlockSpec((tm, wd), lambda i: (i, 0))

    return _call(body, name, (m // tm,), [h, p, g, w_gate, w_proj],
                 [rows(d), rows(D_PLE), pl.BlockSpec((1, d), lambda i: (0, 0)),
                  pl.BlockSpec((d, d), lambda i: (0, 0)), pl.BlockSpec((D_PLE, d), lambda i: (0, 0))],
                 [_sds((m, d), F32), _sds((m, d), BF16), _sds((m, d), F32), _sds((m, d), F32)],
                 [rows(d)] * 4)


def _ple_bwd(name, dh, h, g, gl, pp, w_gate, tm, after=()):
    m, d = h.shape

    def body(dh_ref, h_ref, g_ref, gl_ref, pp_ref, wg_ref, dhn_ref, dgl_ref, dpp_ref, dg_ref):
        dhv = dh_ref[...]
        sg = _sig(gl_ref[...])
        dgl = (dhv * pp_ref[...] * sg * (1.0 - sg)).astype(BF16)
        dgl_ref[...] = dgl
        dpp_ref[...] = (dhv * sg).astype(BF16)
        dx, dg = _rms_bwd(_dot(dgl, wg_ref[...], NT), h_ref[...], g_ref[...])
        dhn_ref[...] = dhv + dx

        @pl.when(pl.program_id(0) == 0)
        def _():
            dg_ref[...] = dg

        @pl.when(pl.program_id(0) > 0)
        def _():
            dg_ref[...] += dg

    rows = pl.BlockSpec((tm, d), lambda i: (i, 0))
    one = pl.BlockSpec((1, d), lambda i: (0, 0))
    return _call(body, name, (m // tm,), [dh, h, g, gl, pp, w_gate],
                 [rows, rows, one, rows, rows, pl.BlockSpec((d, d), lambda i: (0, 0))],
                 [_sds((m, d), F32), _sds((m, d), BF16), _sds((m, d), BF16), _sds((1, d), F32)],
                 [rows, rows, rows, one], after=after)


def _loss_head(name, h, g, target, tm):
    m, d = h.shape

    def body(h_ref, g_ref, t_ref, loss_ref, dh_ref, dg_ref):
        hv, gv = h_ref[...], g_ref[...]
        r, xh = _rms_stats(hv)
        err = xh * gv - t_ref[...]
        dy = err * (1.0 / d)
        tt = dy * gv
        dh_ref[...] = r * (tt - xh * jnp.mean(tt * xh, axis=-1, keepdims=True))
        dg = jnp.sum(dy * xh, axis=0, keepdims=True)
        part = jnp.zeros((1, LANES), F32) + 0.5 * jnp.sum(jnp.sum(err * err, axis=1, keepdims=True), axis=0, keepdims=True) * (1.0 / d)

        @pl.when(pl.program_id(0) == 0)
        def _():
            dg_ref[...] = dg
            loss_ref[...] = part

        @pl.when(pl.program_id(0) > 0)
        def _():
            dg_ref[...] += dg
            loss_ref[...] += part

    rows = pl.BlockSpec((tm, d), lambda i: (i, 0))
    one = pl.BlockSpec((1, d), lambda i: (0, 0))
    return _call(body, name, (m // tm,), [h, g, target], [rows, one, rows],
                 [_sds((1, LANES), F32), _sds((m, d), F32), _sds((1, d), F32)],
                 [pl.BlockSpec((1, LANES), lambda i: (0, 0)), rows, one])


def _place():
    return lax.axis_index("x"), lax.axis_index("y"), lax.axis_index("c")


def _linear(px, py, pc):
    return 4 * px + 2 * py + pc


def _block_of(ref, idx, rows_per_block):
    if rows_per_block is None:
        return ref.at[idx]
    return ref.at[pl.ds(pl.multiple_of(idx * rows_per_block, 16), rows_per_block), :]


FLIPS = tuple((dx, dy, dc) for dx in (0, 1) for dy in (0, 1) for dc in (0, 1))[1:]
N_PEERS = len(FLIPS)
GATHER_FLIPS = ((0, 0, 1), (1, 0, 0), (0, 1, 0), (1, 1, 0))
HBM = pl.BlockSpec(memory_space=pltpu.HBM)
SEM = pl.BlockSpec(memory_space=pltpu.SEMAPHORE)
SIDE_EFFECT = pltpu.SideEffectType.DATAFLOW_SIDE_EFFECTING


def _peer(x, y, c, flip):
    return (1 - x if flip[0] else x, 1 - y if flip[1] else y, 1 - c if flip[2] else c)


def _split_copies(scatter, srcs, lands, bands, send_sems, recv_sems):
    x, y, c = _place()
    my_idx = _linear(x, y, c)
    flips = FLIPS if scatter else GATHER_FLIPS
    copies = []
    for t in range(len(srcs)):
        for k, flip in enumerate(flips):
            peer = _peer(x, y, c, flip)
            if scatter:
                src, dst = _block_of(srcs[t], _linear(*peer), bands[t]), lands[t].at[k]
            else:
                src, dst = srcs[t], _block_of(lands[t], my_idx, bands[t])
            sem = t * len(flips) + k
            copies.append(pltpu.make_async_remote_copy(
                src_ref=src, dst_ref=dst, send_sem=send_sems.at[sem], recv_sem=recv_sems.at[sem],
                device_id=peer, device_id_type=MESH))
    return copies


def _pass_to_sibling(name, lands, bands):
    n = len(lands)

    def body(*refs):
        zones = refs[n:2 * n]
        send_sems, recv_sems = refs[2 * n:]
        x, y, c = _place()
        chips = [(1 - x, y), (x, 1 - y), (1 - x, 1 - y)]

        def copy(t, j, core):
            blk = _block_of(zones[t], _linear(*chips[j], core), bands[t])
            return pltpu.make_async_remote_copy(
                src_ref=blk, dst_ref=blk, send_sem=send_sems.at[t * 3 + j], recv_sem=recv_sems.at[t * 3 + j],
                device_id=(x, y, 1 - c), device_id_type=MESH)

        sent = [copy(t, j, c) for t in range(n) for j in range(3)]
        for cp in sent:
            cp.start()
        for t in range(n):
            for j in range(3):
                copy(t, j, 1 - c).wait_recv()
        for cp in sent:
            cp.wait_send()

    return pl.pallas_call(
        body, name=name, in_specs=[ANY] * n, out_specs=[ANY] * n, out_shape=[_sds(a.shape, a.dtype) for a in lands],
        input_output_aliases={i: i for i in range(n)},
        scratch_shapes=[pltpu.SemaphoreType.DMA((3 * n,)), pltpu.SemaphoreType.DMA((3 * n,))],
    )(*lands)


def _split_start(name, scatter, srcs, lands, bands, follows):
    n = len(srcs)

    def body(*refs):
        send_sems, recv_sems = refs[2 * n + 1], refs[2 * n + 2]
        for cp in _split_copies(scatter, refs[:n], refs[n:2 * n], bands, send_sems, recv_sems):
            cp.start()
        token = refs[-1]
        token[...] = jnp.zeros_like(token)

    pinned = [pltpu.with_memory_space_constraint(a, pltpu.HBM) for a in (*srcs, *lands)]
    n_sems = n * len(FLIPS if scatter else GATHER_FLIPS)
    outs = pl.pallas_call(
        body, name=name,
        out_shape=(pltpu.SemaphoreType.DMA((n_sems,)), pltpu.SemaphoreType.DMA((n_sems,)),
                   *[pltpu.HBM(a.shape, a.dtype) for a in pinned], _sds((8, LANES), F32)),
        in_specs=[HBM] * (2 * n) + [ANY], out_specs=(SEM, SEM, *[HBM] * (2 * n), pl.BlockSpec(memory_space=pltpu.VMEM)),
        input_output_aliases={i: 2 + i for i in range(2 * n)},
        compiler_params=pltpu.CompilerParams(has_side_effects=SIDE_EFFECT),
    )(*pinned, follows)
    return outs[0], outs[1], outs[2:2 + n], outs[2 + n:2 + 2 * n], outs[-1]


def _split_wait(name, scatter, started, bands, follows):
    send_sems, recv_sems, srcs, lands, _ = started
    n = len(srcs)

    def body(*refs):
        for cp in _split_copies(scatter, refs[:n], refs[n:2 * n], bands, refs[2 * n], refs[2 * n + 1]):
            cp.wait_send()
            cp.wait_recv()

    outs = pl.pallas_call(
        body, name=name, out_shape=tuple(pltpu.HBM(a.shape, a.dtype) for a in (*srcs, *lands)),
        in_specs=[HBM] * (2 * n) + [SEM, SEM, ANY], out_specs=[HBM] * (2 * n),
        input_output_aliases={i: i for i in range(2 * n)},
        compiler_params=pltpu.CompilerParams(has_side_effects=SIDE_EFFECT),
    )(*srcs, *lands, send_sems, recv_sems, follows)
    return outs[:n], outs[n:]


def _own_block_filled(shard, band, my_idx):
    if band is None:
        zone = lax.empty((N_DEV,) + shard.shape, shard.dtype)
        return lax.dynamic_update_slice(zone, shard[None], (my_idx,) + (0,) * shard.ndim)
    zone = lax.empty((N_DEV * band,) + shard.shape[1:], shard.dtype)
    return lax.dynamic_update_slice(zone, shard, (my_idx * band,) + (0,) * (shard.ndim - 1))


def _exchange(name, grads, row_bands, gathered):
    n_items = len(grads)
    outs = []
    for gr, band, whole in zip(grads, row_bands, gathered):
        if whole:
            blk = gr.shape
        elif band is None:
            blk = gr.shape[1:]
        else:
            blk = (band,) + gr.shape[1:]
        outs.append(_sds((N_DEV,) + blk, gr.dtype))
    flips = [(dx, dy, dc) for dx in (0, 1) for dy in (0, 1) for dc in (0, 1)][1:]

    def body(*refs):
        srcs, dsts = refs[:n_items], refs[n_items:2 * n_items]
        send_sems, recv_sems, local_sems = refs[2 * n_items:]
        x, y, c = _place()
        my_idx = _linear(x, y, c)

        def src_of(t, idx):
            return srcs[t] if gathered[t] else _block_of(srcs[t], idx, row_bands[t])

        copies = []
        for t in range(n_items):
            local = pltpu.make_async_copy(src_of(t, my_idx), dsts[t].at[my_idx], local_sems.at[t])
            local.start()
            copies.append(local)
            for k, (dx, dy, dc) in enumerate(flips):
                peer = (1 - x if dx else x, 1 - y if dy else y, 1 - c if dc else c)
                cp = pltpu.make_async_remote_copy(
                    src_ref=src_of(t, _linear(*peer)), dst_ref=dsts[t].at[my_idx],
                    send_sem=send_sems.at[t, k], recv_sem=recv_sems.at[t, k], device_id=peer, device_id_type=MESH)
                cp.start()
                copies.append(cp)
        for cp in copies:
            cp.wait()

    return pl.pallas_call(
        body, name=name, in_specs=[ANY] * n_items, out_specs=[ANY] * n_items, out_shape=outs,
        scratch_shapes=[pltpu.SemaphoreType.DMA((n_items, 7)), pltpu.SemaphoreType.DMA((n_items, 7)),
                        pltpu.SemaphoreType.DMA((n_items,))],
    )(*grads)


def _adam_math(g, w, m, v):
    m2 = ADAM_B1 * m + (1.0 - ADAM_B1) * g
    v2 = ADAM_B2 * v + (1.0 - ADAM_B2) * (g * g)
    m_hat = m2 / (1.0 - ADAM_B1 ** ADAM_STEP)
    v_hat = v2 / (1.0 - ADAM_B2 ** ADAM_STEP)
    return -ADAM_LR * (m_hat / (jnp.sqrt(v_hat) + ADAM_EPS) + ADAM_WD * w), m2, v2


def _adamw(name, recv, grads, band, my_idx, w, m, v, layer, prev):
    n_layers, r, c = w.shape
    tr = r
    for cand in (256, 176, 128):
        if r > cand and r % cand == 0:
            tr = cand
            break
    steps = r // tr

    def body(idx_ref, recv_ref, own_ref, w_ref, m_ref, v_ref, *rest):
        g_ref, d_ref, m2_ref, v2_ref = rest[-4:]
        g = own_ref[...].astype(F32)
        for s in range(N_PEERS):
            g = g + recv_ref[s].astype(F32)
        delta, m2, v2 = _adam_math(g, w_ref[...], m_ref[...], v_ref[...])
        g_ref[...] = g
        d_ref[...] = delta
        m2_ref[...] = m2
        v2_ref[...] = v2

    blk = pl.BlockSpec((None, tr, c), lambda i, idx: (layer, i, 0))
    if band is None:
        own = pl.BlockSpec((None, tr, c), lambda i, idx: (idx[0], i, 0))
    else:
        own = pl.BlockSpec((tr, c), lambda i, idx: (idx[0] * steps + i, 0))
    ins = [recv, grads, w, m, v]
    specs = [pl.BlockSpec((N_PEERS, tr, c), lambda i, idx: (0, i, 0)), own, blk, blk, blk]
    aliases = {}
    if prev is not None:
        ins += list(prev)
        specs += [ANY] * 4
        aliases = {1 + len(ins) - 4 + n: n for n in range(4)}
    return _call(body, name, (steps,), ins, specs, [_sds(w.shape, F32)] * 4, [blk] * 4, aliases=aliases, prefetch=my_idx)


def _adamw_small(name, recv, w, m, v):
    r, c = w.shape

    def body(recv_ref, w_ref, m_ref, v_ref, g_ref, d_ref, m2_ref, v2_ref):
        g = recv_ref[0]
        for s in range(1, N_DEV):
            g = g + recv_ref[s]
        delta, m2, v2 = _adam_math(g, w_ref[...], m_ref[...], v_ref[...])
        g_ref[...] = g
        d_ref[...] = delta
        m2_ref[...] = m2
        v2_ref[...] = v2

    full = pl.BlockSpec((r, c), lambda i: (0, 0))
    return _call(body, name, (1,), [recv, w, m, v], [pl.BlockSpec((N_DEV, r, c), lambda i: (0, 0, 0)), full, full, full],
                 [_sds((r, c), F32)] * 4, [full] * 4)


def _ffn_fwd(tag, h, g, wi, wo, after=()):
    m, d = h.shape
    tm = _row_tile(m, 512)
    n = _rmsnorm_fwd(tag + "_norm", h, g, after)
    tall = _row_tile(m, 1024)
    ab = _mm(tag + "_in", n, wi, grid=(N_DEV, m // tall), dims=NT,
             a_spec=pl.BlockSpec((tall, d), lambda s, i: (i, 0)), b_spec=pl.BlockSpec((None, FF_SHARD, d), lambda s, i: (s, 0, 0)),
             out=_sds((N_DEV, m, FF_SHARD), F32), o_spec=pl.BlockSpec((None, tall, FF_SHARD), lambda s, i: (s, i, 0)))
    act = _swiglu_fwd(tag + "_act", ab)
    nk = N_DEV // 2
    row = pl.BlockSpec((tm, d), lambda i, k: (i, 0))
    h2 = _mm(tag + "_out", act, wo, grid=(m // tm, nk), dims=NN, nk=nk, k_axis=1, alpha=FFN_RES,
             a_spec=pl.BlockSpec((None, tm, FF_SHARD), lambda i, k: (k, i, 0)), b_spec=pl.BlockSpec((FF_SHARD, d), lambda i, k: (k, 0)),
             out=_sds((m, d), F32), o_spec=row, res=h, res_spec=row, acc_shape=(tm, d))
    return h2, (h, n, ab, act)


def _ffn_bwd(tag, dh, saved, g, wi, wo, after=()):
    h, n, ab, act = saved
    m, d = h.shape
    tm = _row_tile(m, 512)
    nk = N_DEV // 2
    ds = _mm(tag + "_dact", dh, wo, grid=(nk, m // tm), dims=NT, alpha=FFN_RES, after=after,
             a_spec=pl.BlockSpec((tm, d), lambda j, i: (i, 0)), b_spec=pl.BlockSpec((FF_SHARD, d), lambda j, i: (j, 0)),
             out=_sds((nk, m, FF_SHARD), F32), o_spec=pl.BlockSpec((None, tm, FF_SHARD), lambda j, i: (j, i, 0)))
    dwo = _mm(tag + "_dwo", act, dh, grid=(nk, m // tm), dims=TN, nk=m // tm, k_axis=1, alpha=FFN_RES,
              a_spec=pl.BlockSpec((None, tm, FF_SHARD), lambda j, k: (j, k, 0)), b_spec=pl.BlockSpec((tm, d), lambda j, k: (k, 0)),
              out=_sds((D_FF, d), BF16), o_spec=pl.BlockSpec((FF_SHARD, d), lambda j, k: (j, 0)), acc_shape=(FF_SHARD, d))
    dab = _swiglu_bwd(tag + "_dab", ab, ds)
    row = pl.BlockSpec((tm, d), lambda i, k: (i, 0))
    one = pl.BlockSpec((1, d), lambda i, k: (0, 0))
    dh2, dg = _mm(tag + "_dn", dab, wi, grid=(m // tm, N_DEV), dims=NN, nk=N_DEV, k_axis=1,
                  a_spec=pl.BlockSpec((None, tm, FF_SHARD), lambda i, k: (k, i, 0)),
                  b_spec=pl.BlockSpec((None, FF_SHARD, d), lambda i, k: (k, 0, 0)),
                  out=_sds((m, d), F32), o_spec=row, rms=(h, g, dh, row, one), acc_shape=(tm, d))
    dwi = _mm(tag + "_dwi", dab, n, grid=(N_DEV, m // tm), dims=TN, nk=m // tm, k_axis=1,
              a_spec=pl.BlockSpec((None, tm, FF_SHARD), lambda s, k: (s, k, 0)), b_spec=pl.BlockSpec((tm, d), lambda s, k: (k, 0)),
              out=_sds((N_DEV, FF_SHARD, d), BF16), o_spec=pl.BlockSpec((None, FF_SHARD, d), lambda s, k: (s, 0, 0)),
              acc_shape=(FF_SHARD, d))
    return dh2, dg, dwi, dwo


def _wgrad(name, a, b, tk):
    m = a.shape[0]
    ka, kb = a.shape[1], b.shape[1]
    return _mm(name, a, b, grid=(m // tk,), dims=TN, nk=m // tk, k_axis=0,
               a_spec=pl.BlockSpec((tk, ka), lambda k: (k, 0)), b_spec=pl.BlockSpec((tk, kb), lambda k: (k, 0)),
               out=_sds((ka, kb), BF16), o_spec=pl.BlockSpec((ka, kb), lambda k: (0, 0)), acc_shape=(ka, kb))


def _mixer_fwd(tag, h, g, w_in_p, bf, conv_w, conv_b, g_conv, w_ao, w_co, w_out):
    m, d = h.shape
    t = _row_tile(m, 256)
    u = _rmsnorm_fwd(tag + "_norm", h, g)
    q, k, v, zf, zc, zg = _inproj(tag + "_inproj", u, w_in_p)
    c = _fgate_fwd(tag + "_fgate", zf, bf)
    ta = _row_tile(m, ATTN_TILE)
    ct = c[:, :N_HEADS].T.reshape(N_HEADS, m // ta, ta).transpose(1, 0, 2)
    o, lse = _attn_fwd(tag + "_attn", q, k, v, ct, ta)
    cv, y = _conv_fwd(tag + "_conv", zc, conv_w, conv_b, g_conv, t)
    h2, mg, ya, yc = _mixout_fwd(tag + "_mixout", o, cv, zg, h, w_ao, w_co, w_out, t)
    return h2, (h, u, q, k, v, zf, zc, zg, ct, o, lse, cv, y, mg, ya, yc)


def _mixer_bwd(tag, dh, saved, g, w_in_p, bf, conv_w, g_conv, w_ao, w_co, w_out, after=()):
    h, u, q, k, v, zf, zc, zg, ct, o, lse, cv, y, mg, ya, yc = saved
    m, d = h.shape
    t = _row_tile(m, 256)
    tk = _row_tile(m, 512)
    do, dcv, dzg, dya, dyc = _mixout_bwd(tag + "_dmixout", dh, zg, ya, yc, w_ao, w_co, w_out, t, after)
    d_wout = _wgrad(tag + "_dwout", mg, dh, tk)
    d_wao = _wgrad(tag + "_dwao", o, dya, tk)
    d_wco = _wgrad(tag + "_dwco", cv, dyc, tk)
    dq, dk, dv, dcs, drs = _attn_bwd(tag + "_dattn", q, k, v, ct, o, lse, do, _row_tile(m, ATTN_TILE))
    dc = drs[:, :, :2].transpose(1, 0, 2).reshape(m, N_HEADS) - dcs[:, :, :2, :].transpose(0, 2, 1, 3).reshape(N_HEADS, m).T
    dc = jnp.pad(dc, ((0, 0), (0, F_PAD - N_HEADS)))
    dzf, dbf = _fgate_bwd(tag + "_dfgate", dc, zf, bf)
    dzc, dconv_w, dconv_b, dg_conv = _conv_bwd(tag + "_dconv", dcv, y, zc, conv_w, g_conv, t)
    dz = jnp.concatenate([dq.astype(BF16), dk.astype(BF16), dv.astype(BF16), dzf, dzc, dzg], axis=1)
    tm = _row_tile(m, 256)
    row = pl.BlockSpec((tm, d), lambda i: (i, 0))
    one = pl.BlockSpec((1, d), lambda i: (0, 0))
    dh2, dg = _mm(tag + "_du", dz, w_in_p, grid=(m // tm,), dims=NT,
                  a_spec=pl.BlockSpec((tm, IN_PAD), lambda i: (i, 0)), b_spec=pl.BlockSpec((d, IN_PAD), lambda i: (0, 0)),
                  out=_sds((m, d), F32), o_spec=row, rms=(h, g, dh, row, one))
    rt = 256
    d_win_p = _mm(tag + "_dwin", u, dz, grid=(d // rt, m // tk), dims=TN, nk=m // tk, k_axis=1,
                  a_spec=pl.BlockSpec((tk, rt), lambda r, kk: (kk, r)), b_spec=pl.BlockSpec((tk, IN_PAD), lambda r, kk: (kk, 0)),
                  out=_sds((d, IN_PAD), BF16), o_spec=pl.BlockSpec((rt, IN_PAD), lambda r, kk: (r, 0)), acc_shape=(rt, IN_PAD))
    return dh2, dg, d_win_p, dbf, d_wao, dconv_w, dconv_b, dg_conv, d_wco, d_wout


def _col_shards(full, n_cols):
    r = full.shape[0]
    return full.reshape(r, N_DEV, n_cols).transpose(1, 0, 2)


def _from_col_shards(stacked):
    _, r, n = stacked.shape
    return stacked.transpose(1, 0, 2).reshape(r, N_DEV * n)


SMALL_ROWS = 24


def _pack_small(g_ff1, g_mix, g_ff2, g_ple, g_final, conv_b, g_conv, b_f, loss_row):
    n_layers = g_ff1.shape[0]
    bf_row = jnp.pad(b_f.reshape(1, n_layers * N_HEADS), ((0, 0), (0, D_MODEL - n_layers * N_HEADS)))
    parts = [g_ff1, g_mix, g_ff2, g_ple, g_final.reshape(1, D_MODEL), conv_b.reshape(-1, D_MODEL),
             g_conv.reshape(-1, D_MODEL), bf_row, loss_row]
    packed = jnp.concatenate(parts, axis=0)
    return jnp.pad(packed, ((0, SMALL_ROWS - packed.shape[0]), (0, 0)))


def _unpack_small(packed, n_layers):
    ln = n_layers
    cr = n_layers * CONV_CH // D_MODEL
    pos = 4 * ln + 1
    return dict(
        g_ff1=packed[0:ln], g_mix=packed[ln:2 * ln], g_ff2=packed[2 * ln:3 * ln], g_ple=packed[3 * ln:4 * ln],
        g_final=packed[4 * ln], conv_b=packed[pos:pos + cr].reshape(ln, CONV_CH),
        g_conv=packed[pos + cr:pos + 2 * cr].reshape(ln, CONV_CH),
        b_f=packed[pos + 2 * cr, :ln * N_HEADS].reshape(ln, N_HEADS), loss=packed[pos + 2 * cr + 1, 0])


BIG = ("w_ff1_in", "w_ff1_out", "w_in", "w_attn_out", "conv_w", "w_conv_out", "w_out", "w_ff2_in", "w_ff2_out",
       "w_ple_gate", "w_ple_proj")
ROW_BAND = dict(w_ff1_in=None, w_ff1_out=D_FF // N_DEV, w_in=None, w_attn_out=None, conv_w=None, w_conv_out=None,
                w_out=D_MODEL // N_DEV, w_ff2_in=None, w_ff2_out=D_FF // N_DEV, w_ple_gate=D_MODEL // N_DEV, w_ple_proj=None)
FF_IN = ("w_ff1_in", "w_ff2_in")
FIRST_PARTS = (("w_ff1_in", "w_ff1_out"), ("w_in", "w_attn_out", "conv_w", "w_conv_out", "w_out"),
               ("w_ff2_in", "w_ff2_out", "w_ple_gate", "w_ple_proj"))
LAST_PARTS = (("w_ple_gate", "w_ple_proj", "w_ff2_in", "w_ff2_out"),
              ("w_in", "w_attn_out", "conv_w", "w_conv_out", "w_out"), ("w_ff1_in", "w_ff1_out"))
SMALL = ("g_ff1", "g_mix", "g_ff2", "g_ple", "g_final", "conv_b", "g_conv", "b_f")
ORDER = ("g_ff1", "w_ff1_in", "w_ff1_out", "g_mix", "w_in", "b_f", "w_attn_out", "conv_w", "conv_b", "g_conv", "w_conv_out",
         "w_out", "g_ff2", "w_ff2_in", "w_ff2_out", "g_ple", "w_ple_gate", "w_ple_proj", "g_final")


def kernel(x, p, g_ff1, w_ff1_in, w_ff1_out, g_mix, w_in, b_f, w_attn_out, conv_w, conv_b, g_conv, w_conv_out, w_out, g_ff2, w_ff2_in, w_ff2_out, g_ple, w_ple_gate, w_ple_proj, g_final, loss_target, m_g_ff1, m_w_ff1_in, m_w_ff1_out, m_g_mix, m_w_in, m_b_f, m_w_attn_out, m_conv_w, m_conv_b, m_g_conv, m_w_conv_out, m_w_out, m_g_ff2, m_w_ff2_in, m_w_ff2_out, m_g_ple, m_w_ple_gate, m_w_ple_proj, m_g_final, v_g_ff1, v_w_ff1_in, v_w_ff1_out, v_g_mix, v_w_in, v_b_f, v_w_attn_out, v_conv_w, v_conv_b, v_g_conv, v_w_conv_out, v_w_out, v_g_ff2, v_w_ff2_in, v_w_ff2_out, v_g_ple, v_w_ple_gate, v_w_ple_proj, v_g_final):
    local = dict(locals())
    W = {n: local[n] for n in ORDER}
    M1 = {n: local["m_" + n] for n in ORDER}
    V1 = {n: local["v_" + n] for n in ORDER}
    for group in (W, M1, V1):
        for n in FF_IN:
            group[n] = group[n].transpose(0, 2, 1)
    n_layers = g_ff1.shape[0]
    m_rows = x.shape[1]
    t = _row_tile(m_rows, 256)
    my_idx = _linear(*_place()).astype(jnp.int32)
    idx_arr = my_idx.reshape(1)

    def gather_start(l, part, follows):
        shards = [W[n][l].astype(F32 if n == "conv_w" else BF16) for n in part]
        part_bands = [ROW_BAND[n] for n in part]
        zones = [_own_block_filled(s, band, my_idx) for s, band in zip(shards, part_bands)]
        return part, part_bands, _split_start(f"gather_start_{l}_{part[0]}", False, shards, zones, part_bands, follows)

    def gather_finish(l, started, follows):
        part, part_bands, flight = started
        lands = _split_wait(f"gather_wait_{l}_{part[0]}", False, flight, part_bands, follows)[1]
        return dict(zip(part, _pass_to_sibling("gather_pass", lands, part_bands)))

    def weights_of(got):
        fw = {}
        if "w_ff1_in" in got:
            fw.update(wi1=got["w_ff1_in"], wo1=got["w_ff1_out"])
        if "w_in" in got:
            w_in_full = _from_col_shards(got["w_in"])
            zeros = jnp.zeros((D_MODEL, F_PAD - N_HEADS), BF16)
            fw.update(
                w_in_p=jnp.concatenate([w_in_full[:, :P_F + N_HEADS], zeros, w_in_full[:, P_F + N_HEADS:]], axis=1),
                w_ao=_from_col_shards(got["w_attn_out"]), w_co=_from_col_shards(got["w_conv_out"]), w_out=got["w_out"],
                conv_w=jnp.pad(_from_col_shards(got["conv_w"]), ((0, CONV_HALO - CONV_K), (0, 0))))
        if "w_ff2_in" in got:
            fw.update(wi2=got["w_ff2_in"], wo2=got["w_ff2_out"], w_pg=got["w_ple_gate"], w_pp=_from_col_shards(got["w_ple_proj"]))
        return fw

    def small_of(l):
        return dict(
            bf=jnp.pad(b_f[l].reshape(1, N_HEADS), ((0, 0), (0, F_PAD - N_HEADS))),
            g1=g_ff1[l].reshape(1, -1), gm=g_mix[l].reshape(1, -1), g2=g_ff2[l].reshape(1, -1), gp=g_ple[l].reshape(1, -1),
            conv_b=conv_b[l].reshape(1, -1), g_conv=g_conv[l].reshape(1, -1))

    h = x[0]
    saved, full = [], []
    flights = [gather_start(0, part, h) for part in FIRST_PARTS]
    for l in range(n_layers):
        fw = small_of(l)
        got = gather_finish(l, flights[0], h)
        fw.update(weights_of(got))
        after, coming = (), []
        if l + 1 < n_layers:
            coming = [gather_start(l + 1, BIG, got[flights[0][0][0]])]
            after = (coming[0][2][4],)
        h, s1 = _ffn_fwd("ff1", h, fw["g1"], fw["wi1"], fw["wo1"], after)
        if len(flights) > 1:
            fw.update(weights_of(gather_finish(l, flights[1], h)))
        h, s2 = _mixer_fwd("mix", h, fw["gm"], fw["w_in_p"], fw["bf"], fw["conv_w"], fw["conv_b"], fw["g_conv"],
                           fw["w_ao"], fw["w_co"], fw["w_out"])
        if len(flights) > 2:
            fw.update(weights_of(gather_finish(l, flights[2], h)))
        flights = coming
        full.append(fw)
        h, s3 = _ffn_fwd("ff2", h, fw["g2"], fw["wi2"], fw["wo2"])
        h_in = h
        pl_in = p[l, 0]
        h, n_ple, gl, pp = _ple_fwd("ple", h, pl_in, fw["gp"], fw["w_pg"], fw["w_pp"], t)
        saved.append((s1, s2, s3, (h_in, pl_in, n_ple, gl, pp)))

    loss_row, dh, dg_final = _loss_head("loss_head", h, g_final.reshape(1, -1), loss_target[0], t)

    small_grads = {n: [None] * n_layers for n in ("g_ff1", "g_mix", "g_ff2", "g_ple", "conv_b", "g_conv", "b_f")}
    stacked = {n: None for n in BIG}
    tk = _row_tile(m_rows, 512)

    def start_ready(l, grads, parts_left, follows):
        started, token = [], ()
        for part in [names for names in parts_left if all(n in grads for n in names)]:
            parts_left.remove(part)
            part_bands = [ROW_BAND[n] for n in part]
            terms = [grads[n] for n in part]
            zones = [lax.empty((N_PEERS,) + (g.shape[1:] if band is None else (band,) + g.shape[1:]), g.dtype)
                     for g, band in zip(terms, part_bands)]
            flight = _split_start(f"scatter_start_{l}_{part[0]}", True, terms, zones, part_bands, follows)
            started.append((l, part, part_bands, flight))
            token = (flight[4],)
        return started, token

    def apply(item, follows):
        l, part, part_bands, flight = item
        terms, recv = _split_wait(f"scatter_wait_{l}_{part[0]}", True, flight, part_bands, follows)
        for n, own, rc, band in zip(part, terms, recv, part_bands):
            stacked[n] = _adamw("adamw_" + n, rc, own, band, idx_arr, W[n], M1[n], V1[n], l, stacked[n])

    pending, after = [], ()
    for l in reversed(range(n_layers)):
        parts_left = list(LAST_PARTS if l == 0 else (BIG,))
        mine, grads = [], {}
        fw = full[l]
        s1, s2, s3, (h_in, pl_in, n_ple, gl, pp) = saved[l]
        dh, dgl, dpp, dgp = _ple_bwd("ple_bwd", dh, h_in, fw["gp"], gl, pp, fw["w_pg"], t, after)
        d_wpg = _wgrad("ple_dwgate", n_ple, dgl, tk)
        d_wpp = _wgrad("ple_dwproj", pl_in, dpp, tk)
        dh, dg2, d_wi2, d_wo2 = _ffn_bwd("ff2", dh, s3, fw["g2"], fw["wi2"], fw["wo2"])
        grads.update(w_ple_gate=d_wpg, w_ple_proj=_col_shards(d_wpp, D_MODEL // N_DEV), w_ff2_in=d_wi2, w_ff2_out=d_wo2)
        new, token = start_ready(l, grads, parts_left, dh)
        mine += new
        dh, dgm, d_win_p, dbf, d_wao, dconv_w, dconv_b, dg_conv, d_wco, d_wout = _mixer_bwd(
            "mix", dh, s2, fw["gm"], fw["w_in_p"], fw["bf"], fw["conv_w"], fw["g_conv"], fw["w_ao"], fw["w_co"], fw["w_out"],
            token)
        d_win = jnp.concatenate([d_win_p[:, :P_F + N_HEADS], d_win_p[:, P_C:]], axis=1)
        grads.update(w_in=_col_shards(d_win, IN_SHARD), w_attn_out=_col_shards(d_wao, D_MODEL // N_DEV),
                     conv_w=_col_shards(dconv_w[:CONV_K], CONV_CH // N_DEV), w_conv_out=_col_shards(d_wco, D_MODEL // N_DEV),
                     w_out=d_wout)
        new, token = start_ready(l, grads, parts_left, dh)
        mine += new
        dh, dg1, d_wi1, d_wo1 = _ffn_bwd("ff1", dh, s1, fw["g1"], fw["wi1"], fw["wo1"], token)
        grads.update(w_ff1_in=d_wi1, w_ff1_out=d_wo1)
        small_grads["g_ff1"][l], small_grads["g_mix"][l], small_grads["g_ff2"][l], small_grads["g_ple"][l] = dg1, dgm, dg2, dgp
        small_grads["conv_b"][l], small_grads["g_conv"][l] = dconv_b, dg_conv
        small_grads["b_f"][l] = dbf[:, :N_HEADS]
        for item in pending:
            apply(item, dh)
        new, after = start_ready(l, grads, parts_left, dh)
        pending = mine + new
    for item in pending:
        apply(item, dh)

    cat = {n: jnp.concatenate(small_grads[n], axis=0) for n in small_grads}
    g_pack = _pack_small(cat["g_ff1"], cat["g_mix"], cat["g_ff2"], cat["g_ple"], dg_final, cat["conv_b"], cat["g_conv"],
                         cat["b_f"], loss_row[:, :1] * jnp.ones((1, D_MODEL), F32))
    zero_row = jnp.zeros((1, D_MODEL), F32)
    packs = [_pack_small(*[src[n] for n in ("g_ff1", "g_mix", "g_ff2", "g_ple", "g_final", "conv_b", "g_conv", "b_f")], zero_row)
             for src in (W, M1, V1)]
    (recv_small,) = _exchange("gather_small", [g_pack], [None], [True])
    outs_small = [_unpack_small(a, n_layers) for a in _adamw_small("adamw_small", recv_small, *packs)]

    def pick(kind, n):
        if n in SMALL:
            return outs_small[kind][n]
        return stacked[n][kind].transpose(0, 2, 1) if n in FF_IN else stacked[n][kind]

    result = [outs_small[0]["loss"], dh.reshape(x.shape)]
    for kind in range(4):
        result += [pick(kind, n) for n in ORDER]
    return tuple(result)
```

```python
import functools
import math

import jax
import jax.numpy as jnp
from jax import lax
from jax.experimental import pallas as pl
from jax.experimental.pallas import tpu as pltpu

F32 = jnp.float32
BF16 = jnp.bfloat16

N_DEV = 8
D_MODEL = 1024
N_HEADS = 8
HEAD_DIM = 64
ATTN_W = N_HEADS * HEAD_DIM
CONV_CH = 512
CONV_K = 31
CONV_HALO = 32
D_FF = 2816
FF_SHARD = 2 * D_FF // N_DEV
D_PLE = 256
EPS = 1e-6
FFN_RES = 0.5
LANES = 128
IN_COLS = 3 * ATTN_W + N_HEADS + 2 * CONV_CH + 2 * D_MODEL
IN_SHARD = IN_COLS // N_DEV
F_PAD = LANES
P_Q, P_K, P_V = 0, ATTN_W, 2 * ATTN_W
P_F = 3 * ATTN_W
P_C = P_F + F_PAD
P_G = P_C + 2 * CONV_CH
IN_PAD = P_G + 2 * D_MODEL
NEG_BIG = -1e30
ATTN_TILE = 256

ADAM_LR, ADAM_B1, ADAM_B2, ADAM_EPS, ADAM_WD, ADAM_STEP = 0.001, 0.9, 0.999, 1e-08, 0.01, 10

VMEM_CAP = 60 * 1024 * 1024
VMEM_SLACK = 12 * 1024 * 1024

NN = (((1,), (0,)), ((), ()))
NT = (((1,), (1,)), ((), ()))
TN = (((0,), (0,)), ((), ()))

MESH = pl.DeviceIdType.MESH
ANY = pl.BlockSpec(memory_space=pl.ANY)


def _nbytes(shape, dtype):
    return math.prod(d for d in shape if d is not None) * jnp.dtype(dtype).itemsize


def _params(block_bytes, n_axes):
    limit = min(VMEM_CAP, 2 * block_bytes + VMEM_SLACK)
    return pltpu.CompilerParams(dimension_semantics=("arbitrary",) * n_axes, vmem_limit_bytes=limit)


def _call(body, name, grid, in_arrays, in_specs, out_shapes, out_specs, scratch=(), aliases=None, extra_bytes=0,
          after=(), prefetch=None):
    total = extra_bytes
    for a, s in zip(in_arrays, in_specs):
        if s.block_shape is not None:
            total += _nbytes(s.block_shape, a.dtype)
    for o, s in zip(out_shapes, out_specs):
        if s.block_shape is not None:
            total += _nbytes(s.block_shape, o.dtype)
    n_in, n_after = len(in_arrays), len(after)
    lead = 0 if prefetch is None else 1

    def with_after(*refs):
        return body(*refs[:lead + n_in], *refs[lead + n_in + n_after:])

    operands = [pltpu.with_memory_space_constraint(a, pltpu.HBM) for a in (*in_arrays, *after)]
    specs = list(in_specs) + [ANY] * n_after
    common = dict(name=name, out_shape=[pltpu.HBM(o.shape, o.dtype) for o in out_shapes], input_output_aliases=aliases or {},
                  compiler_params=_params(total, len(grid)))
    if prefetch is None:
        return pl.pallas_call(with_after if n_after else body, grid=grid, in_specs=specs, out_specs=list(out_specs),
                              scratch_shapes=list(scratch), **common)(*operands)
    grid_spec = pltpu.PrefetchScalarGridSpec(num_scalar_prefetch=1, grid=grid, in_specs=specs, out_specs=list(out_specs),
                                             scratch_shapes=list(scratch))
    return pl.pallas_call(with_after if n_after else body, grid_spec=grid_spec, **common)(prefetch, *operands)


def _sig(x):
    return 1.0 / (1.0 + jnp.exp(-x))


def _dot(a, b, dims):
    return lax.dot_general(a.astype(BF16), b.astype(BF16), dims, preferred_element_type=F32)


def _rms_stats(x):
    r = lax.rsqrt(jnp.mean(x * x, axis=-1, keepdims=True) + EPS)
    return r, x * r


def _rms_bwd(dn, x, g):
    r, xh = _rms_stats(x)
    t = dn * g
    dx = r * (t - xh * jnp.mean(t * xh, axis=-1, keepdims=True))
    return dx, jnp.sum(dn * xh, axis=0, keepdims=True)


def _row_tile(m, want):
    t = min(m, want)
    assert m % t == 0
    return t


def _mm(name, a, b, *, grid, a_spec, b_spec, out, o_spec, dims, nk=None, k_axis=None, alpha=1.0,
        res=None, res_spec=None, rms=None, acc_shape=None, after=()):
    first_axes = len(grid)

    def body(*refs):
        refs = list(refs)
        a_ref, b_ref = refs[:2]
        pos = 2
        if res is not None:
            res_ref = refs[pos]
            pos += 1
        if rms is not None:
            h_ref, g_ref, dres_ref = refs[pos:pos + 3]
            pos += 3
        o_ref = refs[pos]
        pos += 1
        if rms is not None:
            dg_ref = refs[pos]
            pos += 1
        acc_ref = refs[pos] if k_axis is not None else None
        ids = [pl.program_id(ax) for ax in range(first_axes)]
        row_axes = [ids[ax] == 0 for ax in range(first_axes) if ax != k_axis]
        is_first = functools.reduce(jnp.logical_and, row_axes) if row_axes else None

        part = _dot(a_ref[...], b_ref[...], dims)

        def finish(acc):
            val = acc if alpha == 1.0 else acc * alpha
            if res is not None:
                val = val + res_ref[...]
            if rms is not None:
                dx, dg = _rms_bwd(val, h_ref[...], g_ref[...])
                o_ref[...] = (dres_ref[...] + dx).astype(o_ref.dtype)

                @pl.when(is_first)
                def _():
                    dg_ref[...] = dg

                @pl.when(jnp.logical_not(is_first))
                def _():
                    dg_ref[...] += dg
            else:
                o_ref[...] = val.astype(o_ref.dtype)

        if k_axis is None:
            finish(part)
        else:
            k = ids[k_axis]

            @pl.when(k == 0)
            def _():
                acc_ref[...] = part

            @pl.when(k > 0)
            def _():
                acc_ref[...] += part

            @pl.when(k == nk - 1)
            def _():
                finish(acc_ref[...])

    in_arrays, in_specs = [a, b], [a_spec, b_spec]
    if res is not None:
        in_arrays.append(res)
        in_specs.append(res_spec)
    outs, o_specs = [out], [o_spec]
    if rms is not None:
        h, g, dres, row_spec, g_spec = rms
        in_arrays += [h, g, dres]
        in_specs += [row_spec, g_spec, row_spec]
        outs.append(jax.ShapeDtypeStruct(g.shape, F32))
        o_specs.append(g_spec)
    scratch, extra = [], 0
    if k_axis is not None:
        scratch = [pltpu.VMEM(acc_shape, F32)]
        extra = _nbytes(acc_shape, F32)
    res_out = _call(body, name, grid, in_arrays, in_specs, outs, o_specs, scratch, extra_bytes=extra, after=after)
    return res_out if rms is not None else res_out[0]


def _sds(shape, dtype):
    return jax.ShapeDtypeStruct(shape, dtype)


def _rmsnorm_fwd(name, h, g, after=()):
    m, d = h.shape
    tm = _row_tile(m, 512)

    def body(h_ref, g_ref, o_ref):
        _, xh = _rms_stats(h_ref[...])
        o_ref[...] = (xh * g_ref[...]).astype(BF16)

    row = pl.BlockSpec((tm, d), lambda i: (i, 0))
    return _call(body, name, (m // tm,), [h, g], [row, pl.BlockSpec((1, d), lambda i: (0, 0))],
                 [_sds((m, d), BF16)], [row], after=after)[0]


def _swiglu_fwd(name, ab):
    _, m, w = ab.shape
    half = N_DEV // 2
    tm = _row_tile(m, 512)
    ab4 = ab.reshape(2, half, m, w)

    def body(ab_ref, o_ref):
        a, b = ab_ref[0], ab_ref[1]
        o_ref[...] = (a * _sig(a) * b).astype(BF16)

    return _call(body, name, (half, m // tm), [ab4],
                 [pl.BlockSpec((2, None, tm, w), lambda j, i: (0, j, i, 0))],
                 [_sds((half, m, w), BF16)], [pl.BlockSpec((None, tm, w), lambda j, i: (j, i, 0))])[0]


def _swiglu_bwd(name, ab, ds):
    _, m, w = ab.shape
    half = N_DEV // 2
    tm = _row_tile(m, 512)
    ab4 = ab.reshape(2, half, m, w)

    def body(ab_ref, ds_ref, o_ref):
        a, b, d = ab_ref[0], ab_ref[1], ds_ref[...]
        sg = _sig(a)
        o_ref[0] = (d * b * sg * (1.0 + a * (1.0 - sg))).astype(BF16)
        o_ref[1] = (d * a * sg).astype(BF16)

    blk = pl.BlockSpec((2, None, tm, w), lambda j, i: (0, j, i, 0))
    out = _call(body, name, (half, m // tm), [ab4, ds],
                [blk, pl.BlockSpec((None, tm, w), lambda j, i: (j, i, 0))],
                [_sds((2, half, m, w), BF16)], [blk])[0]
    return out.reshape(N_DEV, m, w)


def _inproj(name, u, w_in_p):
    m, d = u.shape
    tm = _row_tile(m, 256)

    def body(u_ref, w_ref, q_ref, k_ref, v_ref, f_ref, c_ref, g_ref):
        z = _dot(u_ref[...], w_ref[...], NN)
        q_ref[...] = z[:, P_Q:P_K].astype(BF16)
        k_ref[...] = z[:, P_K:P_V].astype(BF16)
        v_ref[...] = z[:, P_V:P_F].astype(BF16)
        f_ref[...] = z[:, P_F:P_C]
        c_ref[...] = z[:, P_C:P_G]
        g_ref[...] = z[:, P_G:IN_PAD]

    def rows(width):
        return pl.BlockSpec((tm, width), lambda i: (i, 0))

    widths = (ATTN_W, ATTN_W, ATTN_W, F_PAD, 2 * CONV_CH, 2 * D_MODEL)
    dtypes = (BF16, BF16, BF16, F32, F32, F32)
    return _call(body, name, (m // tm,), [u, w_in_p], [rows(d), pl.BlockSpec((d, IN_PAD), lambda i: (0, 0))],
                 [_sds((m, wd), dt) for wd, dt in zip(widths, dtypes)], [rows(wd) for wd in widths])


def _cumsum_rows(x, reverse):
    m = x.shape[0]
    row = lax.broadcasted_iota(jnp.int32, x.shape, 0)
    sh = 1
    while sh < m:
        if reverse:
            x = x + jnp.where(row < m - sh, pltpu.roll(x, m - sh, axis=0), 0.0)
        else:
            x = x + jnp.where(row >= sh, pltpu.roll(x, sh, axis=0), 0.0)
        sh *= 2
    return x


def _log_sigmoid(x):
    return jnp.minimum(x, 0.0) - jnp.log(1.0 + jnp.exp(-jnp.abs(x)))


def _fgate_fwd(name, zf, bf):
    m, w = zf.shape

    def body(z_ref, b_ref, c_ref):
        c_ref[...] = _cumsum_rows(_log_sigmoid(z_ref[...] + b_ref[...]), reverse=False)

    full = pl.BlockSpec((m, w), lambda i: (0, 0))
    return _call(body, name, (1,), [zf, bf], [full, pl.BlockSpec((1, w), lambda i: (0, 0))], [_sds((m, w), F32)], [full])[0]


def _fgate_bwd(name, dc, zf, bf):
    m, w = zf.shape

    def body(dc_ref, z_ref, b_ref, dz_ref, db_ref):
        dls = _cumsum_rows(dc_ref[...], reverse=True)
        dz = dls * _sig(-(z_ref[...] + b_ref[...]))
        lane = lax.broadcasted_iota(jnp.int32, dz.shape, 1)
        dz = jnp.where(lane < N_HEADS, dz, 0.0)
        dz_ref[...] = dz.astype(BF16)
        db_ref[...] = jnp.sum(dz, axis=0, keepdims=True)

    full = pl.BlockSpec((m, w), lambda i: (0, 0))
    one = pl.BlockSpec((1, w), lambda i: (0, 0))
    return _call(body, name, (1,), [dc, zf, bf], [full, full, one], [_sds((m, w), BF16), _sds((1, w), F32)], [full, one])


def _lane_pick(x, idx):
    lane = lax.broadcasted_iota(jnp.int32, x.shape, 1)
    return jnp.sum(jnp.where(lane == idx, x, 0.0), axis=1, keepdims=True)


def _row_pick(x, idx):
    sub = lax.broadcasted_iota(jnp.int32, x.shape, 0)
    return jnp.sum(jnp.where(sub == idx, x, 0.0), axis=0, keepdims=True)


def _attn_fwd(name, q, k, v, ct, t):
    m = q.shape[0]
    n_chunks = m // t
    pairs = N_HEADS // 2
    scale = 1.0 / math.sqrt(HEAD_DIM)

    def body(q_ref, k_ref, v_ref, ct_ref, o_ref, lse_ref):
        p = pl.program_id(0)
        i = pl.program_id(1)
        lane = lax.broadcasted_iota(jnp.int32, (t, LANES), 1)
        first = lane < HEAD_DIM
        q2 = q_ref[...] * scale
        zero = jnp.zeros_like(q2)
        q_heads = (jnp.where(first, q2, zero), jnp.where(first, zero, q2))
        below = lax.broadcasted_iota(jnp.int32, (t, t), 0) >= lax.broadcasted_iota(jnp.int32, (t, t), 1)

        def step(j, carry, diagonal):
            off = pl.multiple_of(j * t, t)
            kj = k_ref[pl.ds(off, t), :]
            vj = v_ref[pl.ds(off, t), :]
            ctj = ct_ref[j]
            new = []
            for e in range(2):
                mx, den, acc = carry[3 * e:3 * e + 3]
                s = _dot(q_heads[e], kj, NT) - _row_pick(ctj, 2 * p + e)
                if diagonal:
                    s = jnp.where(below, s, NEG_BIG)
                mx2 = jnp.maximum(mx, jnp.max(s, axis=1, keepdims=True))
                corr = jnp.exp(mx - mx2)
                pe = jnp.exp(s - mx2)
                new += [mx2, corr * den + jnp.sum(pe, axis=1, keepdims=True), corr * acc + _dot(pe, vj, NN)]
            return tuple(new)

        col = jnp.full((t, 1), NEG_BIG, F32), jnp.zeros((t, 1), F32), jnp.zeros((t, LANES), F32)
        fin = step(i, lax.fori_loop(0, i, lambda j, carry: step(j, carry, False), col + col), True)
        o_ref[...] = jnp.where(first, fin[2] / fin[1], fin[5] / fin[4]).astype(BF16)
        lse_a = fin[0] + jnp.log(fin[1])
        lse_b = fin[3] + jnp.log(fin[4])
        lse_ref[...] = jnp.where(lane == 0, lse_a, jnp.where(lane == 1, lse_b, 0.0))

    seq = pl.BlockSpec((m, LANES), lambda p, i: (0, p))
    blk = pl.BlockSpec((t, LANES), lambda p, i: (i, p))
    return _call(body, name, (pairs, n_chunks), [q, k, v, ct],
                 [blk, seq, seq, pl.BlockSpec((n_chunks, 8, t), lambda p, i: (0, 0, 0))],
                 [_sds((m, ATTN_W), BF16), _sds((pairs, m, LANES), F32)],
                 [blk, pl.BlockSpec((None, t, LANES), lambda p, i: (p, i, 0))])


def _attn_bwd(name, q, k, v, ct, o, lse, do, t):
    m = q.shape[0]
    n_chunks = m // t
    pairs = N_HEADS // 2
    scale = 1.0 / math.sqrt(HEAD_DIM)

    def body(q_ref, do_ref, o_ref, lse_ref, ct_ref, k_ref, v_ref, dq_ref, dk_ref, dv_ref, dcs_ref, drs_ref):
        p = pl.program_id(0)
        j = pl.program_id(1)

        @pl.when(j == 0)
        def _():
            dq_ref[...] = jnp.zeros_like(dq_ref)
            drs_ref[...] = jnp.zeros_like(drs_ref)

        lane = lax.broadcasted_iota(jnp.int32, (t, LANES), 1)
        first = lane < HEAD_DIM
        kj, vj = k_ref[...] * scale, v_ref[...]
        zero = jnp.zeros_like(kj)
        k_heads = (jnp.where(first, kj, zero), jnp.where(first, zero, kj))
        v_heads = (jnp.where(first, vj, zero), jnp.where(first, zero, vj))
        ctj = ct_ref[j]
        c_rows = (_row_pick(ctj, 2 * p), _row_pick(ctj, 2 * p + 1))
        below = lax.broadcasted_iota(jnp.int32, (t, t), 0) >= lax.broadcasted_iota(jnp.int32, (t, t), 1)

        def step(i, carry, diagonal):
            off = pl.multiple_of(i * t, t)
            qi = q_ref[pl.ds(off, t), :]
            doi = do_ref[pl.ds(off, t), :]
            prod = doi * o_ref[pl.ds(off, t), :].astype(F32)
            lsei = lse_ref[pl.ds(off, t), :]
            dob = doi.astype(BF16)
            new = []
            dq = jnp.zeros((t, LANES), F32)
            drow = jnp.zeros((t, LANES), F32)
            for e in range(2):
                dk, dv, dcol = carry[3 * e:3 * e + 3]
                delta = jnp.sum(jnp.where(first if e == 0 else jnp.logical_not(first), prod, 0.0), axis=1, keepdims=True)
                pe = jnp.exp(_dot(qi, k_heads[e], NT) - c_rows[e] - _lane_pick(lsei, e))
                if diagonal:
                    pe = jnp.where(below, pe, 0.0)
                dsc = pe * (_dot(dob, v_heads[e], NT) - delta)
                dsb = dsc.astype(BF16)
                dq = dq + _dot(dsb, k_heads[e], NN)
                drow = jnp.where(lane == e, jnp.sum(dsc, axis=1, keepdims=True), drow)
                new += [dk + _dot(dsb, qi, TN), dv + _dot(pe, dob, TN), dcol + jnp.sum(dsc, axis=0, keepdims=True)]
            dq_ref[pl.ds(off, t), :] += dq
            drs_ref[pl.ds(off, t), :] += drow
            return tuple(new)

        z = jnp.zeros((t, LANES), F32), jnp.zeros((t, LANES), F32), jnp.zeros((1, t), F32)
        fin = lax.fori_loop(j + 1, n_chunks, lambda i, carry: step(i, carry, False), step(j, z + z, True))
        dk_ref[...] = jnp.where(first, fin[0], fin[3]) * scale
        dv_ref[...] = jnp.where(first, fin[1], fin[4])
        sub = lax.broadcasted_iota(jnp.int32, (8, t), 0)
        dcs_ref[...] = jnp.where(sub == 0, fin[2], jnp.where(sub == 1, fin[5], 0.0))

    seq = pl.BlockSpec((m, LANES), lambda p, j: (0, p))
    blk = pl.BlockSpec((t, LANES), lambda p, j: (j, p))
    return _call(body, name, (pairs, n_chunks), [q, do, o, lse, ct, k, v],
                 [seq, seq, seq, pl.BlockSpec((None, m, LANES), lambda p, j: (p, 0, 0)),
                  pl.BlockSpec((n_chunks, 8, t), lambda p, j: (0, 0, 0)), blk, blk],
                 [_sds((m, ATTN_W), F32), _sds((m, ATTN_W), F32), _sds((m, ATTN_W), F32),
                  _sds((pairs, n_chunks, 8, t), F32), _sds((pairs, m, LANES), F32)],
                 [seq, blk, blk, pl.BlockSpec((None, None, 8, t), lambda p, j: (p, j, 0, 0)),
                  pl.BlockSpec((None, m, LANES), lambda p, j: (p, 0, 0))])


def _glu(z):
    return z[:, :CONV_CH] * _sig(z[:, CONV_CH:])


def _shifted(x, lead, tm):
    n = x.shape[0]
    return pltpu.roll(x, (n - lead) % n, axis=0)[:tm]


def _conv_fwd(name, zc, w, b, g, tm):
    m = zc.shape[0]
    hb = tm // CONV_HALO

    def body(cur_ref, prev_ref, w_ref, b_ref, g_ref, cv_ref, y_ref):
        i = pl.program_id(0)
        a_prev = jnp.where(i > 0, _glu(prev_ref[...]), 0.0)
        af = jnp.concatenate([a_prev, _glu(cur_ref[...])], axis=0)
        y = jnp.zeros((tm, CONV_CH), F32)
        for tap in range(CONV_K):
            y = y + w_ref[pl.ds(tap, 1), :] * _shifted(af, CONV_HALO - (CONV_K - 1) + tap, tm)
        y = y + b_ref[...]
        y_ref[...] = y
        _, xh = _rms_stats(y)
        rn = xh * g_ref[...]
        cv_ref[...] = (rn * _sig(rn)).astype(BF16)

    one = pl.BlockSpec((1, CONV_CH), lambda i: (0, 0))
    out = pl.BlockSpec((tm, CONV_CH), lambda i: (i, 0))
    return _call(body, name, (m // tm,), [zc, zc, w, b, g],
                 [pl.BlockSpec((tm, 2 * CONV_CH), lambda i: (i, 0)),
                  pl.BlockSpec((CONV_HALO, 2 * CONV_CH), lambda i: (jnp.maximum(i * hb - 1, 0), 0)),
                  pl.BlockSpec((CONV_HALO, CONV_CH), lambda i: (0, 0)), one, one],
                 [_sds((m, CONV_CH), BF16), _sds((m, CONV_CH), F32)], [out, out])


def _conv_bwd(name, dcv, y, zc, w, g, tm):
    m = zc.shape[0]
    hb = tm // CONV_HALO
    n_blocks = m // tm

    def body(dcv_ref, dcvn_ref, y_ref, yn_ref, cur_ref, prev_ref, w_ref, g_ref, dz_ref, dw_ref, db_ref, dg_ref):
        i = pl.program_id(0)
        gv = g_ref[...]

        def dy_of(d, yv):
            r, xh = _rms_stats(yv)
            rn = xh * gv
            sg = _sig(rn)
            drn = d * sg * (1.0 + rn * (1.0 - sg))
            tt = drn * gv
            return r * (tt - xh * jnp.mean(tt * xh, axis=-1, keepdims=True)), drn * xh

        dy, dgt = dy_of(dcv_ref[...], y_ref[...])
        dy_next, _ = dy_of(dcvn_ref[...], yn_ref[...])
        dyf = jnp.concatenate([dy, jnp.where(i < n_blocks - 1, dy_next, 0.0)], axis=0)
        cur = cur_ref[...]
        af = jnp.concatenate([jnp.where(i > 0, _glu(prev_ref[...]), 0.0), _glu(cur)], axis=0)

        @pl.when(i == 0)
        def _():
            dw_ref[...] = jnp.zeros_like(dw_ref)
            db_ref[...] = jnp.zeros_like(db_ref)
            dg_ref[...] = jnp.zeros_like(dg_ref)

        da = jnp.zeros((tm, CONV_CH), F32)
        for tap in range(CONV_K):
            da = da + w_ref[pl.ds(tap, 1), :] * _shifted(dyf, CONV_K - 1 - tap, tm)
            a_tap = _shifted(af, CONV_HALO - (CONV_K - 1) + tap, tm)
            dw_ref[pl.ds(tap, 1), :] += jnp.sum(dy * a_tap, axis=0, keepdims=True)
        db_ref[...] += jnp.sum(dy, axis=0, keepdims=True)
        dg_ref[...] += jnp.sum(dgt, axis=0, keepdims=True)
        c1, sg2 = cur[:, :CONV_CH], _sig(cur[:, CONV_CH:])
        dz_ref[:, :CONV_CH] = (da * sg2).astype(BF16)
        dz_ref[:, CONV_CH:] = (da * c1 * sg2 * (1.0 - sg2)).astype(BF16)

    one = pl.BlockSpec((1, CONV_CH), lambda i: (0, 0))
    taps = pl.BlockSpec((CONV_HALO, CONV_CH), lambda i: (0, 0))
    row = pl.BlockSpec((tm, CONV_CH), lambda i: (i, 0))
    nxt = pl.BlockSpec((CONV_HALO, CONV_CH), lambda i: (jnp.minimum((i + 1) * hb, m // CONV_HALO - 1), 0))
    row2 = pl.BlockSpec((tm, 2 * CONV_CH), lambda i: (i, 0))
    return _call(body, name, (n_blocks,), [dcv, dcv, y, y, zc, zc, w, g],
                 [row, nxt, row, nxt, row2,
                  pl.BlockSpec((CONV_HALO, 2 * CONV_CH), lambda i: (jnp.maximum(i * hb - 1, 0), 0)), taps, one],
                 [_sds((m, 2 * CONV_CH), BF16), _sds((CONV_HALO, CONV_CH), F32), _sds((1, CONV_CH), F32), _sds((1, CONV_CH), F32)],
                 [row2, taps, one, one])


def _mixout_fwd(name, o, cv, zg, h, w_ao, w_co, w_out, tm):
    m, d = h.shape

    def body(o_ref, cv_ref, zg_ref, h_ref, wa_ref, wc_ref, wo_ref, hn_ref, mg_ref, ya_ref, yc_ref):
        ya = _dot(o_ref[...], wa_ref[...], NN)
        yc = _dot(cv_ref[...], wc_ref[...], NN)
        zg_v = zg_ref[...]
        mg = (_sig(zg_v[:, :d]) * ya + _sig(zg_v[:, d:]) * yc).astype(BF16)
        ya_ref[...] = ya
        yc_ref[...] = yc
        mg_ref[...] = mg
        hn_ref[...] = h_ref[...] + _dot(mg, wo_ref[...], NN)

    def rows(wd):
        return pl.BlockSpec((tm, wd), lambda i: (i, 0))

    def whole(a):
        return pl.BlockSpec(a.shape, lambda i: (0, 0))

    return _call(body, name, (m // tm,), [o, cv, zg, h, w_ao, w_co, w_out],
                 [rows(ATTN_W), rows(CONV_CH), rows(2 * d), rows(d), whole(w_ao), whole(w_co), whole(w_out)],
                 [_sds((m, d), F32), _sds((m, d), BF16), _sds((m, d), F32), _sds((m, d), F32)],
                 [rows(d), rows(d), rows(d), rows(d)])


def _mixout_bwd(name, dh, zg, ya, yc, w_ao, w_co, w_out, tm, after=()):
    m, d = dh.shape

    def body(dh_ref, zg_ref, ya_ref, yc_ref, wa_ref, wc_ref, wo_ref, do_ref, dcv_ref, dzg_ref, dya_ref, dyc_ref):
        dm = _dot(dh_ref[...], wo_ref[...], NT)
        zg_v = zg_ref[...]
        sa, sc = _sig(zg_v[:, :d]), _sig(zg_v[:, d:])
        dya = (dm * sa).astype(BF16)
        dyc = (dm * sc).astype(BF16)
        dzg_ref[:, :d] = (dm * ya_ref[...] * sa * (1.0 - sa)).astype(BF16)
        dzg_ref[:, d:] = (dm * yc_ref[...] * sc * (1.0 - sc)).astype(BF16)
        dya_ref[...] = dya
        dyc_ref[...] = dyc
        do_ref[...] = _dot(dya, wa_ref[...], NT)
        dcv_ref[...] = _dot(dyc, wc_ref[...], NT)

    def rows(wd):
        return pl.BlockSpec((tm, wd), lambda i: (i, 0))

    def whole(a):
        return pl.BlockSpec(a.shape, lambda i: (0, 0))

    return _call(body, name, (m // tm,), [dh, zg, ya, yc, w_ao, w_co, w_out],
                 [rows(d), rows(2 * d), rows(d), rows(d), whole(w_ao), whole(w_co), whole(w_out)],
                 [_sds((m, ATTN_W), F32), _sds((m, CONV_CH), F32), _sds((m, 2 * d), BF16), _sds((m, d), BF16), _sds((m, d), BF16)],
                 [rows(ATTN_W), rows(CONV_CH), rows(2 * d), rows(d), rows(d)], after=after)


def _ple_fwd(name, h, p, g, w_gate, w_proj, tm):
    m, d = h.shape

    def body(h_ref, p_ref, g_ref, wg_ref, wp_ref, hn_ref, n_ref, gl_ref, pp_ref):
        hv = h_ref[...]
        _, xh = _rms_stats(hv)
        n = (xh * g_ref[...]).astype(BF16)
        gl = _dot(n, wg_ref[...], NN)
        pp = _dot(p_ref[...], wp_ref[...], NN)
        n_ref[...] = n
        gl_ref[...] = gl
        pp_ref[...] = pp
        hn_ref[...] = hv + _sig(gl) * pp

    def rows(wd):
        return pl.BlockSpec((tm, wd), lambda i: (i, 0))

    return _call(body, name, (m // tm,), [h, p, g, w_gate, w_proj],
                 [rows(d), rows(D_PLE), pl.BlockSpec((1, d), lambda i: (0, 0)),
                  pl.BlockSpec((d, d), lambda i: (0, 0)), pl.BlockSpec((D_PLE, d), lambda i: (0, 0))],
                 [_sds((m, d), F32), _sds((m, d), BF16), _sds((m, d), F32), _sds((m, d), F32)],
                 [rows(d)] * 4)


def _ple_bwd(name, dh, h, g, gl, pp, w_gate, tm, after=()):
    m, d = h.shape

    def body(dh_ref, h_ref, g_ref, gl_ref, pp_ref, wg_ref, dhn_ref, dgl_ref, dpp_ref, dg_ref):
        dhv = dh_ref[...]
        sg = _sig(gl_ref[...])
        dgl = (dhv * pp_ref[...] * sg * (1.0 - sg)).astype(BF16)
        dgl_ref[...] = dgl
        dpp_ref[...] = (dhv * sg).astype(BF16)
        dx, dg = _rms_bwd(_dot(dgl, wg_ref[...], NT), h_ref[...], g_ref[...])
        dhn_ref[...] = dhv + dx

        @pl.when(pl.program_id(0) == 0)
        def _():
            dg_ref[...] = dg

        @pl.when(pl.program_id(0) > 0)
        def _():
            dg_ref[...] += dg

    rows = pl.BlockSpec((tm, d), lambda i: (i, 0))
    one = pl.BlockSpec((1, d), lambda i: (0, 0))
    return _call(body, name, (m // tm,), [dh, h, g, gl, pp, w_gate],
                 [rows, rows, one, rows, rows, pl.BlockSpec((d, d), lambda i: (0, 0))],
                 [_sds((m, d), F32), _sds((m, d), BF16), _sds((m, d), BF16), _sds((1, d), F32)],
                 [rows, rows, rows, one], after=after)


def _loss_head(name, h, g, target, tm):
    m, d = h.shape

    def body(h_ref, g_ref, t_ref, loss_ref, dh_ref, dg_ref):
        hv, gv = h_ref[...], g_ref[...]
        r, xh = _rms_stats(hv)
        err = xh * gv - t_ref[...]
        dy = err * (1.0 / d)
        tt = dy * gv
        dh_ref[...] = r * (tt - xh * jnp.mean(tt * xh, axis=-1, keepdims=True))
        dg = jnp.sum(dy * xh, axis=0, keepdims=True)
        part = jnp.zeros((1, LANES), F32) + 0.5 * jnp.sum(jnp.sum(err * err, axis=1, keepdims=True), axis=0, keepdims=True) * (1.0 / d)

        @pl.when(pl.program_id(0) == 0)
        def _():
            dg_ref[...] = dg
            loss_ref[...] = part

        @pl.when(pl.program_id(0) > 0)
        def _():
            dg_ref[...] += dg
            loss_ref[...] += part

    rows = pl.BlockSpec((tm, d), lambda i: (i, 0))
    one = pl.BlockSpec((1, d), lambda i: (0, 0))
    return _call(body, name, (m // tm,), [h, g, target], [rows, one, rows],
                 [_sds((1, LANES), F32), _sds((m, d), F32), _sds((1, d), F32)],
                 [pl.BlockSpec((1, LANES), lambda i: (0, 0)), rows, one])


def _place():
    return lax.axis_index("x"), lax.axis_index("y"), lax.axis_index("c")


def _linear(px, py, pc):
    return 4 * px + 2 * py + pc


def _block_of(ref, idx, rows_per_block):
    if rows_per_block is None:
        return ref.at[idx]
    return ref.at[pl.ds(pl.multiple_of(idx * rows_per_block, 16), rows_per_block), :]


FLIPS = tuple((dx, dy, dc) for dx in (0, 1) for dy in (0, 1) for dc in (0, 1))[1:]
N_PEERS = len(FLIPS)
GATHER_FLIPS = ((0, 0, 1), (1, 0, 0), (0, 1, 0), (1, 1, 0))
HBM = pl.BlockSpec(memory_space=pltpu.HBM)
SEM = pl.BlockSpec(memory_space=pltpu.SEMAPHORE)
SIDE_EFFECT = pltpu.SideEffectType.DATAFLOW_SIDE_EFFECTING


def _peer(x, y, c, flip):
    return (1 - x if flip[0] else x, 1 - y if flip[1] else y, 1 - c if flip[2] else c)


def _split_copies(scatter, srcs, lands, bands, send_sems, recv_sems):
    x, y, c = _place()
    my_idx = _linear(x, y, c)
    flips = FLIPS if scatter else GATHER_FLIPS
    copies = []
    for t in range(len(srcs)):
        for k, flip in enumerate(flips):
            peer = _peer(x, y, c, flip)
            if scatter:
                src, dst = _block_of(srcs[t], _linear(*peer), bands[t]), lands[t].at[k]
            else:
                src, dst = srcs[t], _block_of(lands[t], my_idx, bands[t])
            sem = t * len(flips) + k
            copies.append(pltpu.make_async_remote_copy(
                src_ref=src, dst_ref=dst, send_sem=send_sems.at[sem], recv_sem=recv_sems.at[sem],
                device_id=peer, device_id_type=MESH))
    return copies


def _pass_to_sibling(name, lands, bands):
    n = len(lands)

    def body(*refs):
        zones = refs[n:2 * n]
        send_sems, recv_sems = refs[2 * n:]
        x, y, c = _place()
        chips = [(1 - x, y), (x, 1 - y), (1 - x, 1 - y)]

        def copy(t, j, core):
            blk = _block_of(zones[t], _linear(*chips[j], core), bands[t])
            return pltpu.make_async_remote_copy(
                src_ref=blk, dst_ref=blk, send_sem=send_sems.at[t * 3 + j], recv_sem=recv_sems.at[t * 3 + j],
                device_id=(x, y, 1 - c), device_id_type=MESH)

        sent = [copy(t, j, c) for t in range(n) for j in range(3)]
        for cp in sent:
            cp.start()
        for t in range(n):
            for j in range(3):
                copy(t, j, 1 - c).wait_recv()
        for cp in sent:
            cp.wait_send()

    return pl.pallas_call(
        body, name=name, in_specs=[ANY] * n, out_specs=[ANY] * n, out_shape=[_sds(a.shape, a.dtype) for a in lands],
        input_output_aliases={i: i for i in range(n)},
        scratch_shapes=[pltpu.SemaphoreType.DMA((3 * n,)), pltpu.SemaphoreType.DMA((3 * n,))],
    )(*lands)


def _split_start(name, scatter, srcs, lands, bands, follows):
    n = len(srcs)

    def body(*refs):
        send_sems, recv_sems = refs[2 * n + 1], refs[2 * n + 2]
        for cp in _split_copies(scatter, refs[:n], refs[n:2 * n], bands, send_sems, recv_sems):
            cp.start()
        token = refs[-1]
        token[...] = jnp.zeros_like(token)

    pinned = [pltpu.with_memory_space_constraint(a, pltpu.HBM) for a in (*srcs, *lands)]
    n_sems = n * len(FLIPS if scatter else GATHER_FLIPS)
    outs = pl.pallas_call(
        body, name=name,
        out_shape=(pltpu.SemaphoreType.DMA((n_sems,)), pltpu.SemaphoreType.DMA((n_sems,)),
                   *[pltpu.HBM(a.shape, a.dtype) for a in pinned], _sds((8, LANES), F32)),
        in_specs=[HBM] * (2 * n) + [ANY], out_specs=(SEM, SEM, *[HBM] * (2 * n), pl.BlockSpec(memory_space=pltpu.VMEM)),
        input_output_aliases={i: 2 + i for i in range(2 * n)},
        compiler_params=pltpu.CompilerParams(has_side_effects=SIDE_EFFECT),
    )(*pinned, follows)
    return outs[0], outs[1], outs[2:2 + n], outs[2 + n:2 + 2 * n], outs[-1]


def _split_wait(name, scatter, started, bands, follows):
    send_sems, recv_sems, srcs, lands, _ = started
    n = len(srcs)

    def body(*refs):
        for cp in _split_copies(scatter, refs[:n], refs[n:2 * n], bands, refs[2 * n], refs[2 * n + 1]):
            cp.wait_send()
            cp.wait_recv()

    outs = pl.pallas_call(
        body, name=name, out_shape=tuple(pltpu.HBM(a.shape, a.dtype) for a in (*srcs, *lands)),
        in_specs=[HBM] * (2 * n) + [SEM, SEM, ANY], out_specs=[HBM] * (2 * n),
        input_output_aliases={i: i for i in range(2 * n)},
        compiler_params=pltpu.CompilerParams(has_side_effects=SIDE_EFFECT),
    )(*srcs, *lands, send_sems, recv_sems, follows)
    return outs[:n], outs[n:]


def _own_block_filled(shard, band, my_idx):
    if band is None:
        zone = lax.empty((N_DEV,) + shard.shape, shard.dtype)
        return lax.dynamic_update_slice(zone, shard[None], (my_idx,) + (0,) * shard.ndim)
    zone = lax.empty((N_DEV * band,) + shard.shape[1:], shard.dtype)
    return lax.dynamic_update_slice(zone, shard, (my_idx * band,) + (0,) * (shard.ndim - 1))


def _exchange(name, grads, row_bands, gathered):
    n_items = len(grads)
    outs = []
    for gr, band, whole in zip(grads, row_bands, gathered):
        if whole:
            blk = gr.shape
        elif band is None:
            blk = gr.shape[1:]
        else:
            blk = (band,) + gr.shape[1:]
        outs.append(_sds((N_DEV,) + blk, gr.dtype))
    flips = [(dx, dy, dc) for dx in (0, 1) for dy in (0, 1) for dc in (0, 1)][1:]

    def body(*refs):
        srcs, dsts = refs[:n_items], refs[n_items:2 * n_items]
        send_sems, recv_sems, local_sems = refs[2 * n_items:]
        x, y, c = _place()
        my_idx = _linear(x, y, c)

        def src_of(t, idx):
            return srcs[t] if gathered[t] else _block_of(srcs[t], idx, row_bands[t])

        copies = []
        for t in range(n_items):
            local = pltpu.make_async_copy(src_of(t, my_idx), dsts[t].at[my_idx], local_sems.at[t])
            local.start()
            copies.append(local)
            for k, (dx, dy, dc) in enumerate(flips):
                peer = (1 - x if dx else x, 1 - y if dy else y, 1 - c if dc else c)
                cp = pltpu.make_async_remote_copy(
                    src_ref=src_of(t, _linear(*peer)), dst_ref=dsts[t].at[my_idx],
                    send_sem=send_sems.at[t, k], recv_sem=recv_sems.at[t, k], device_id=peer, device_id_type=MESH)
                cp.start()
                copies.append(cp)
        for cp in copies:
            cp.wait()

    return pl.pallas_call(
        body, name=name, in_specs=[ANY] * n_items, out_specs=[ANY] * n_items, out_shape=outs,
        scratch_shapes=[pltpu.SemaphoreType.DMA((n_items, 7)), pltpu.SemaphoreType.DMA((n_items, 7)),
                        pltpu.SemaphoreType.DMA((n_items,))],
    )(*grads)


def _adam_math(g, w, m, v):
    m2 = ADAM_B1 * m + (1.0 - ADAM_B1) * g
    v2 = ADAM_B2 * v + (1.0 - ADAM_B2) * (g * g)
    m_hat = m2 / (1.0 - ADAM_B1 ** ADAM_STEP)
    v_hat = v2 / (1.0 - ADAM_B2 ** ADAM_STEP)
    return -ADAM_LR * (m_hat / (jnp.sqrt(v_hat) + ADAM_EPS) + ADAM_WD * w), m2, v2


def _adamw(name, recv, grads, band, my_idx, w, m, v, layer, prev):
    n_layers, r, c = w.shape
    tr = r
    for cand in (256, 176, 128):
        if r > cand and r % cand == 0:
            tr = cand
            break
    steps = r // tr

    def body(idx_ref, recv_ref, own_ref, w_ref, m_ref, v_ref, *rest):
        g_ref, d_ref, m2_ref, v2_ref = rest[-4:]
        g = own_ref[...].astype(F32)
        for s in range(N_PEERS):
            g = g + recv_ref[s].astype(F32)
        delta, m2, v2 = _adam_math(g, w_ref[...], m_ref[...], v_ref[...])
        g_ref[...] = g
        d_ref[...] = delta
        m2_ref[...] = m2
        v2_ref[...] = v2

    blk = pl.BlockSpec((None, tr, c), lambda i, idx: (layer, i, 0))
    if band is None:
        own = pl.BlockSpec((None, tr, c), lambda i, idx: (idx[0], i, 0))
    else:
        own = pl.BlockSpec((tr, c), lambda i, idx: (idx[0] * steps + i, 0))
    ins = [recv, grads, w, m, v]
    specs = [pl.BlockSpec((N_PEERS, tr, c), lambda i, idx: (0, i, 0)), own, blk, blk, blk]
    aliases = {}
    if prev is not None:
        ins += list(prev)
        specs += [ANY] * 4
        aliases = {1 + len(ins) - 4 + n: n for n in range(4)}
    return _call(body, name, (steps,), ins, specs, [_sds(w.shape, F32)] * 4, [blk] * 4, aliases=aliases, prefetch=my_idx)


def _adamw_small(name, recv, w, m, v):
    r, c = w.shape

    def body(recv_ref, w_ref, m_ref, v_ref, g_ref, d_ref, m2_ref, v2_ref):
        g = recv_ref[0]
        for s in range(1, N_DEV):
            g = g + recv_ref[s]
        delta, m2, v2 = _adam_math(g, w_ref[...], m_ref[...], v_ref[...])
        g_ref[...] = g
        d_ref[...] = delta
        m2_ref[...] = m2
        v2_ref[...] = v2

    full = pl.BlockSpec((r, c), lambda i: (0, 0))
    return _call(body, name, (1,), [recv, w, m, v], [pl.BlockSpec((N_DEV, r, c), lambda i: (0, 0, 0)), full, full, full],
                 [_sds((r, c), F32)] * 4, [full] * 4)


def _ffn_fwd(tag, h, g, wi, wo, after=()):
    m, d = h.shape
    tm = _row_tile(m, 512)
    n = _rmsnorm_fwd(tag + "_norm", h, g, after)
    tall = _row_tile(m, 1024)
    ab = _mm(tag + "_in", n, wi, grid=(N_DEV, m // tall), dims=NT,
             a_spec=pl.BlockSpec((tall, d), lambda s, i: (i, 0)), b_spec=pl.BlockSpec((None, FF_SHARD, d), lambda s, i: (s, 0, 0)),
             out=_sds((N_DEV, m, FF_SHARD), F32), o_spec=pl.BlockSpec((None, tall, FF_SHARD), lambda s, i: (s, i, 0)))
    act = _swiglu_fwd(tag + "_act", ab)
    nk = N_DEV // 2
    row = pl.BlockSpec((tm, d), lambda i, k: (i, 0))
    h2 = _mm(tag + "_out", act, wo, grid=(m // tm, nk), dims=NN, nk=nk, k_axis=1, alpha=FFN_RES,
             a_spec=pl.BlockSpec((None, tm, FF_SHARD), lambda i, k: (k, i, 0)), b_spec=pl.BlockSpec((FF_SHARD, d), lambda i, k: (k, 0)),
             out=_sds((m, d), F32), o_spec=row, res=h, res_spec=row, acc_shape=(tm, d))
    return h2, (h, n, ab, act)


def _ffn_bwd(tag, dh, saved, g, wi, wo, after=()):
    h, n, ab, act = saved
    m, d = h.shape
    tm = _row_tile(m, 512)
    nk = N_DEV // 2
    ds = _mm(tag + "_dact", dh, wo, grid=(nk, m // tm), dims=NT, alpha=FFN_RES, after=after,
             a_spec=pl.BlockSpec((tm, d), lambda j, i: (i, 0)), b_spec=pl.BlockSpec((FF_SHARD, d), lambda j, i: (j, 0)),
             out=_sds((nk, m, FF_SHARD), F32), o_spec=pl.BlockSpec((None, tm, FF_SHARD), lambda j, i: (j, i, 0)))
    dwo = _mm(tag + "_dwo", act, dh, grid=(nk, m // tm), dims=TN, nk=m // tm, k_axis=1, alpha=FFN_RES,
              a_spec=pl.BlockSpec((None, tm, FF_SHARD), lambda j, k: (j, k, 0)), b_spec=pl.BlockSpec((tm, d), lambda j, k: (k, 0)),
              out=_sds((D_FF, d), BF16), o_spec=pl.BlockSpec((FF_SHARD, d), lambda j, k: (j, 0)), acc_shape=(FF_SHARD, d))
    dab = _swiglu_bwd(tag + "_dab", ab, ds)
    row = pl.BlockSpec((tm, d), lambda i, k: (i, 0))
    one = pl.BlockSpec((1, d), lambda i, k: (0, 0))
    dh2, dg = _mm(tag + "_dn", dab, wi, grid=(m // tm, N_DEV), dims=NN, nk=N_DEV, k_axis=1,
                  a_spec=pl.BlockSpec((None, tm, FF_SHARD), lambda i, k: (k, i, 0)),
                  b_spec=pl.BlockSpec((None, FF_SHARD, d), lambda i, k: (k, 0, 0)),
                  out=_sds((m, d), F32), o_spec=row, rms=(h, g, dh, row, one), acc_shape=(tm, d))
    dwi = _mm(tag + "_dwi", dab, n, grid=(N_DEV, m // tm), dims=TN, nk=m // tm, k_axis=1,
              a_spec=pl.BlockSpec((None, tm, FF_SHARD), lambda s, k: (s, k, 0)), b_spec=pl.BlockSpec((tm, d), lambda s, k: (k, 0)),
              out=_sds((N_DEV, FF_SHARD, d), BF16), o_spec=pl.BlockSpec((None, FF_SHARD, d), lambda s, k: (s, 0, 0)),
              acc_shape=(FF_SHARD, d))
    return dh2, dg, dwi, dwo


def _wgrad(name, a, b, tk):
    m = a.shape[0]
    ka, kb = a.shape[1], b.shape[1]
    return _mm(name, a, b, grid=(m // tk,), dims=TN, nk=m // tk, k_axis=0,
               a_spec=pl.BlockSpec((tk, ka), lambda k: (k, 0)), b_spec=pl.BlockSpec((tk, kb), lambda k: (k, 0)),
               out=_sds((ka, kb), BF16), o_spec=pl.BlockSpec((ka, kb), lambda k: (0, 0)), acc_shape=(ka, kb))


def _mixer_fwd(tag, h, g, w_in_p, bf, conv_w, conv_b, g_conv, w_ao, w_co, w_out, after=()):
    m, d = h.shape
    t = _row_tile(m, 256)
    u = _rmsnorm_fwd(tag + "_norm", h, g, after)
    q, k, v, zf, zc, zg = _inproj(tag + "_inproj", u, w_in_p)
    c = _fgate_fwd(tag + "_fgate", zf, bf)
    ta = _row_tile(m, ATTN_TILE)
    ct = c[:, :N_HEADS].T.reshape(N_HEADS, m // ta, ta).transpose(1, 0, 2)
    o, lse = _attn_fwd(tag + "_attn", q, k, v, ct, ta)
    cv, y = _conv_fwd(tag + "_conv", zc, conv_w, conv_b, g_conv, t)
    h2, mg, ya, yc = _mixout_fwd(tag + "_mixout", o, cv, zg, h, w_ao, w_co, w_out, t)
    return h2, (h, u, q, k, v, zf, zc, zg, ct, o, lse, cv, y, mg, ya, yc)


def _mixer_bwd(tag, dh, saved, g, w_in_p, bf, conv_w, g_conv, w_ao, w_co, w_out, after=()):
    h, u, q, k, v, zf, zc, zg, ct, o, lse, cv, y, mg, ya, yc = saved
    m, d = h.shape
    t = _row_tile(m, 256)
    tk = _row_tile(m, 512)
    do, dcv, dzg, dya, dyc = _mixout_bwd(tag + "_dmixout", dh, zg, ya, yc, w_ao, w_co, w_out, t, after)
    d_wout = _wgrad(tag + "_dwout", mg, dh, tk)
    d_wao = _wgrad(tag + "_dwao", o, dya, tk)
    d_wco = _wgrad(tag + "_dwco", cv, dyc, tk)
    dq, dk, dv, dcs, drs = _attn_bwd(tag + "_dattn", q, k, v, ct, o, lse, do, _row_tile(m, ATTN_TILE))
    dc = drs[:, :, :2].transpose(1, 0, 2).reshape(m, N_HEADS) - dcs[:, :, :2, :].transpose(0, 2, 1, 3).reshape(N_HEADS, m).T
    dc = jnp.pad(dc, ((0, 0), (0, F_PAD - N_HEADS)))
    dzf, dbf = _fgate_bwd(tag + "_dfgate", dc, zf, bf)
    dzc, dconv_w, dconv_b, dg_conv = _conv_bwd(tag + "_dconv", dcv, y, zc, conv_w, g_conv, t)
    dz = jnp.concatenate([dq.astype(BF16), dk.astype(BF16), dv.astype(BF16), dzf, dzc, dzg], axis=1)
    tm = _row_tile(m, 256)
    row = pl.BlockSpec((tm, d), lambda i: (i, 0))
    one = pl.BlockSpec((1, d), lambda i: (0, 0))
    dh2, dg = _mm(tag + "_du", dz, w_in_p, grid=(m // tm,), dims=NT,
                  a_spec=pl.BlockSpec((tm, IN_PAD), lambda i: (i, 0)), b_spec=pl.BlockSpec((d, IN_PAD), lambda i: (0, 0)),
                  out=_sds((m, d), F32), o_spec=row, rms=(h, g, dh, row, one))
    rt = 256
    d_win_p = _mm(tag + "_dwin", u, dz, grid=(d // rt, m // tk), dims=TN, nk=m // tk, k_axis=1,
                  a_spec=pl.BlockSpec((tk, rt), lambda r, kk: (kk, r)), b_spec=pl.BlockSpec((tk, IN_PAD), lambda r, kk: (kk, 0)),
                  out=_sds((d, IN_PAD), BF16), o_spec=pl.BlockSpec((rt, IN_PAD), lambda r, kk: (r, 0)), acc_shape=(rt, IN_PAD))
    return dh2, dg, d_win_p, dbf, d_wao, dconv_w, dconv_b, dg_conv, d_wco, d_wout


def _col_shards(full, n_cols):
    r = full.shape[0]
    return full.reshape(r, N_DEV, n_cols).transpose(1, 0, 2)


def _from_col_shards(stacked):
    _, r, n = stacked.shape
    return stacked.transpose(1, 0, 2).reshape(r, N_DEV * n)


SMALL_ROWS = 24


def _pack_small(g_ff1, g_mix, g_ff2, g_ple, g_final, conv_b, g_conv, b_f, loss_row):
    n_layers = g_ff1.shape[0]
    bf_row = jnp.pad(b_f.reshape(1, n_layers * N_HEADS), ((0, 0), (0, D_MODEL - n_layers * N_HEADS)))
    parts = [g_ff1, g_mix, g_ff2, g_ple, g_final.reshape(1, D_MODEL), conv_b.reshape(-1, D_MODEL),
             g_conv.reshape(-1, D_MODEL), bf_row, loss_row]
    packed = jnp.concatenate(parts, axis=0)
    return jnp.pad(packed, ((0, SMALL_ROWS - packed.shape[0]), (0, 0)))


def _unpack_small(packed, n_layers):
    ln = n_layers
    cr = n_layers * CONV_CH // D_MODEL
    pos = 4 * ln + 1
    return dict(
        g_ff1=packed[0:ln], g_mix=packed[ln:2 * ln], g_ff2=packed[2 * ln:3 * ln], g_ple=packed[3 * ln:4 * ln],
        g_final=packed[4 * ln], conv_b=packed[pos:pos + cr].reshape(ln, CONV_CH),
        g_conv=packed[pos + cr:pos + 2 * cr].reshape(ln, CONV_CH),
        b_f=packed[pos + 2 * cr, :ln * N_HEADS].reshape(ln, N_HEADS), loss=packed[pos + 2 * cr + 1, 0])


BIG = ("w_ff1_in", "w_ff1_out", "w_in", "w_attn_out", "conv_w", "w_conv_out", "w_out", "w_ff2_in", "w_ff2_out",
       "w_ple_gate", "w_ple_proj")
ROW_BAND = dict(w_ff1_in=None, w_ff1_out=D_FF // N_DEV, w_in=None, w_attn_out=None, conv_w=None, w_conv_out=None,
                w_out=D_MODEL // N_DEV, w_ff2_in=None, w_ff2_out=D_FF // N_DEV, w_ple_gate=D_MODEL // N_DEV, w_ple_proj=None)
FF_IN = ("w_ff1_in", "w_ff2_in")
FIRST_PARTS = (("w_ff1_in", "w_ff1_out"), ("w_in", "w_attn_out", "conv_w", "w_conv_out", "w_out"),
               ("w_ff2_in", "w_ff2_out", "w_ple_gate", "w_ple_proj"))
LAST_PARTS = (("w_ple_gate", "w_ple_proj", "w_ff2_in", "w_ff2_out"),
              ("w_in", "w_attn_out", "conv_w", "w_conv_out", "w_out"), ("w_ff1_in", "w_ff1_out"))
SMALL = ("g_ff1", "g_mix", "g_ff2", "g_ple", "g_final", "conv_b", "g_conv", "b_f")
ORDER = ("g_ff1", "w_ff1_in", "w_ff1_out", "g_mix", "w_in", "b_f", "w_attn_out", "conv_w", "conv_b", "g_conv", "w_conv_out",
         "w_out", "g_ff2", "w_ff2_in", "w_ff2_out", "g_ple", "w_ple_gate", "w_ple_proj", "g_final")


def kernel(x, p, g_ff1, w_ff1_in, w_ff1_out, g_mix, w_in, b_f, w_attn_out, conv_w, conv_b, g_conv, w_conv_out, w_out, g_ff2, w_ff2_in, w_ff2_out, g_ple, w_ple_gate, w_ple_proj, g_final, loss_target, m_g_ff1, m_w_ff1_in, m_w_ff1_out, m_g_mix, m_w_in, m_b_f, m_w_attn_out, m_conv_w, m_conv_b, m_g_conv, m_w_conv_out, m_w_out, m_g_ff2, m_w_ff2_in, m_w_ff2_out, m_g_ple, m_w_ple_gate, m_w_ple_proj, m_g_final, v_g_ff1, v_w_ff1_in, v_w_ff1_out, v_g_mix, v_w_in, v_b_f, v_w_attn_out, v_conv_w, v_conv_b, v_g_conv, v_w_conv_out, v_w_out, v_g_ff2, v_w_ff2_in, v_w_ff2_out, v_g_ple, v_w_ple_gate, v_w_ple_proj, v_g_final):
    local = dict(locals())
    W = {n: local[n] for n in ORDER}
    M1 = {n: local["m_" + n] for n in ORDER}
    V1 = {n: local["v_" + n] for n in ORDER}
    for group in (W, M1, V1):
        for n in FF_IN:
            group[n] = group[n].transpose(0, 2, 1)
    n_layers = g_ff1.shape[0]
    m_rows = x.shape[1]
    t = _row_tile(m_rows, 256)
    my_idx = _linear(*_place()).astype(jnp.int32)
    idx_arr = my_idx.reshape(1)

    def gather_start(l, part, follows):
        shards = [W[n][l].astype(F32 if n == "conv_w" else BF16) for n in part]
        part_bands = [ROW_BAND[n] for n in part]
        zones = [_own_block_filled(s, band, my_idx) for s, band in zip(shards, part_bands)]
        return part, part_bands, _split_start(f"gather_start_{l}_{part[0]}", False, shards, zones, part_bands, follows)

    def gather_finish(l, started, follows):
        part, part_bands, flight = started
        lands = _split_wait(f"gather_wait_{l}_{part[0]}", False, flight, part_bands, follows)[1]
        return dict(zip(part, _pass_to_sibling("gather_pass", lands, part_bands)))

    def weights_of(got):
        fw = {}
        if "w_ff1_in" in got:
            fw.update(wi1=got["w_ff1_in"], wo1=got["w_ff1_out"])
        if "w_in" in got:
            w_in_full = _from_col_shards(got["w_in"])
            zeros = jnp.zeros((D_MODEL, F_PAD - N_HEADS), BF16)
            fw.update(
                w_in_p=jnp.concatenate([w_in_full[:, :P_F + N_HEADS], zeros, w_in_full[:, P_F + N_HEADS:]], axis=1),
                w_ao=_from_col_shards(got["w_attn_out"]), w_co=_from_col_shards(got["w_conv_out"]), w_out=got["w_out"],
                conv_w=jnp.pad(_from_col_shards(got["conv_w"]), ((0, CONV_HALO - CONV_K), (0, 0))))
        if "w_ff2_in" in got:
            fw.update(wi2=got["w_ff2_in"], wo2=got["w_ff2_out"], w_pg=got["w_ple_gate"], w_pp=_from_col_shards(got["w_ple_proj"]))
        return fw

    def small_of(l):
        return dict(
            bf=jnp.pad(b_f[l].reshape(1, N_HEADS), ((0, 0), (0, F_PAD - N_HEADS))),
            g1=g_ff1[l].reshape(1, -1), gm=g_mix[l].reshape(1, -1), g2=g_ff2[l].reshape(1, -1), gp=g_ple[l].reshape(1, -1),
            conv_b=conv_b[l].reshape(1, -1), g_conv=g_conv[l].reshape(1, -1))

    h = x[0]
    saved, full = [], []
    flights = [gather_start(0, part, h) for part in FIRST_PARTS]
    for l in range(n_layers):
        fw = small_of(l)
        got = gather_finish(l, flights[0], h)
        fw.update(weights_of(got))
        after, after_mix, coming = (), (), []
        if l + 1 < n_layers and len(flights) == 1:
            coming = [gather_start(l + 1, BIG, got[flights[0][0][0]])]
            after = (coming[0][2][4],)
        h, s1 = _ffn_fwd("ff1", h, fw["g1"], fw["wi1"], fw["wo1"], after)
        if len(flights) > 1:
            got = gather_finish(l, flights[1], h)
            fw.update(weights_of(got))
            if l + 1 < n_layers:
                coming = [gather_start(l + 1, BIG, got[flights[1][0][0]])]
                after_mix = (coming[0][2][4],)
        h, s2 = _mixer_fwd("mix", h, fw["gm"], fw["w_in_p"], fw["bf"], fw["conv_w"], fw["conv_b"], fw["g_conv"],
                           fw["w_ao"], fw["w_co"], fw["w_out"], after_mix)
        if len(flights) > 2:
            fw.update(weights_of(gather_finish(l, flights[2], h)))
        flights = coming
        full.append(fw)
        h, s3 = _ffn_fwd("ff2", h, fw["g2"], fw["wi2"], fw["wo2"])
        h_in = h
        pl_in = p[l, 0]
        h, n_ple, gl, pp = _ple_fwd("ple", h, pl_in, fw["gp"], fw["w_pg"], fw["w_pp"], t)
        saved.append((s1, s2, s3, (h_in, pl_in, n_ple, gl, pp)))

    loss_row, dh, dg_final = _loss_head("loss_head", h, g_final.reshape(1, -1), loss_target[0], t)

    small_grads = {n: [None] * n_layers for n in ("g_ff1", "g_mix", "g_ff2", "g_ple", "conv_b", "g_conv", "b_f")}
    stacked = {n: None for n in BIG}
    tk = _row_tile(m_rows, 512)

    def start_ready(l, grads, parts_left, follows):
        started, token = [], ()
        for part in [names for names in parts_left if all(n in grads for n in names)]:
            parts_left.remove(part)
            part_bands = [ROW_BAND[n] for n in part]
            terms = [grads[n] for n in part]
            zones = [lax.empty((N_PEERS,) + (g.shape[1:] if band is None else (band,) + g.shape[1:]), g.dtype)
                     for g, band in zip(terms, part_bands)]
            flight = _split_start(f"scatter_start_{l}_{part[0]}", True, terms, zones, part_bands, follows)
            started.append((l, part, part_bands, flight))
            token = (flight[4],)
        return started, token

    def apply(item, follows):
        l, part, part_bands, flight = item
        terms, recv = _split_wait(f"scatter_wait_{l}_{part[0]}", True, flight, part_bands, follows)
        for n, own, rc, band in zip(part, terms, recv, part_bands):
            stacked[n] = _adamw("adamw_" + n, rc, own, band, idx_arr, W[n], M1[n], V1[n], l, stacked[n])

    pending, after = [], ()
    for l in reversed(range(n_layers)):
        parts_left = list(LAST_PARTS if l == 0 else (BIG,))
        mine, grads = [], {}
        fw = full[l]
        s1, s2, s3, (h_in, pl_in, n_ple, gl, pp) = saved[l]
        dh, dgl, dpp, dgp = _ple_bwd("ple_bwd", dh, h_in, fw["gp"], gl, pp, fw["w_pg"], t, after)
        d_wpg = _wgrad("ple_dwgate", n_ple, dgl, tk)
        d_wpp = _wgrad("ple_dwproj", pl_in, dpp, tk)
        dh, dg2, d_wi2, d_wo2 = _ffn_bwd("ff2", dh, s3, fw["g2"], fw["wi2"], fw["wo2"])
        grads.update(w_ple_gate=d_wpg, w_ple_proj=_col_shards(d_wpp, D_MODEL // N_DEV), w_ff2_in=d_wi2, w_ff2_out=d_wo2)
        new, token = start_ready(l, grads, parts_left, dh)
        mine += new
        dh, dgm, d_win_p, dbf, d_wao, dconv_w, dconv_b, dg_conv, d_wco, d_wout = _mixer_bwd(
            "mix", dh, s2, fw["gm"], fw["w_in_p"], fw["bf"], fw["conv_w"], fw["g_conv"], fw["w_ao"], fw["w_co"], fw["w_out"],
            token)
        d_win = jnp.concatenate([d_win_p[:, :P_F + N_HEADS], d_win_p[:, P_C:]], axis=1)
        grads.update(w_in=_col_shards(d_win, IN_SHARD), w_attn_out=_col_shards(d_wao, D_MODEL // N_DEV),
                     conv_w=_col_shards(dconv_w[:CONV_K], CONV_CH // N_DEV), w_conv_out=_col_shards(d_wco, D_MODEL // N_DEV),
                     w_out=d_wout)
        new, token = start_ready(l, grads, parts_left, dh)
        mine += new
        dh, dg1, d_wi1, d_wo1 = _ffn_bwd("ff1", dh, s1, fw["g1"], fw["wi1"], fw["wo1"], token)
        grads.update(w_ff1_in=d_wi1, w_ff1_out=d_wo1)
        small_grads["g_ff1"][l], small_grads["g_mix"][l], small_grads["g_ff2"][l], small_grads["g_ple"][l] = dg1, dgm, dg2, dgp
        small_grads["conv_b"][l], small_grads["g_conv"][l] = dconv_b, dg_conv
        small_grads["b_f"][l] = dbf[:, :N_HEADS]
        for item in pending:
            apply(item, dh)
        new, after = start_ready(l, grads, parts_left, dh)
        pending = mine + new
    for item in pending:
        apply(item, dh)

    cat = {n: jnp.concatenate(small_grads[n], axis=0) for n in small_grads}
    g_pack = _pack_small(cat["g_ff1"], cat["g_mix"], cat["g_ff2"], cat["g_ple"], dg_final, cat["conv_b"], cat["g_conv"],
                         cat["b_f"], loss_row[:, :1] * jnp.ones((1, D_MODEL), F32))
    zero_row = jnp.zeros((1, D_MODEL), F32)
    packs = [_pack_small(*[src[n] for n in ("g_ff1", "g_mix", "g_ff2", "g_ple", "g_final", "conv_b", "g_conv", "b_f")], zero_row)
             for src in (W, M1, V1)]
    (recv_small,) = _exchange("gather_small", [g_pack], [None], [True])
    outs_small = [_unpack_small(a, n_layers) for a in _adamw_small("adamw_small", recv_small, *packs)]

    def pick(kind, n):
        if n in SMALL:
            return outs_small[kind][n]
        return stacked[n][kind].transpose(0, 2, 1) if n in FF_IN else stacked[n][kind]

    result = [outs_small[0]["loss"], dh.reshape(x.shape)]
    for kind in range(4):
        result += [pick(kind, n) for n in ORDER]
    return tuple(result)
```

```python
import functools
import math

import jax
import jax.numpy as jnp
from jax import lax
from jax.experimental import pallas as pl
from jax.experimental.pallas import tpu as pltpu

F32 = jnp.float32
BF16 = jnp.bfloat16

N_DEV = 8
D_MODEL = 1024
N_HEADS = 8
HEAD_DIM = 64
ATTN_W = N_HEADS * HEAD_DIM
CONV_CH = 512
CONV_K = 31
CONV_HALO = 32
D_FF = 2816
FF_SHARD = 2 * D_FF // N_DEV
D_PLE = 256
EPS = 1e-6
FFN_RES = 0.5
LANES = 128
IN_COLS = 3 * ATTN_W + N_HEADS + 2 * CONV_CH + 2 * D_MODEL
IN_SHARD = IN_COLS // N_DEV
F_PAD = LANES
P_Q, P_K, P_V = 0, ATTN_W, 2 * ATTN_W
P_F = 3 * ATTN_W
P_C = P_F + F_PAD
P_G = P_C + 2 * CONV_CH
IN_PAD = P_G + 2 * D_MODEL
NEG_BIG = -1e30
ATTN_TILE = 256

ADAM_LR, ADAM_B1, ADAM_B2, ADAM_EPS, ADAM_WD, ADAM_STEP = 0.001, 0.9, 0.999, 1e-08, 0.01, 10

VMEM_CAP = 60 * 1024 * 1024
VMEM_SLACK = 12 * 1024 * 1024

NN = (((1,), (0,)), ((), ()))
NT = (((1,), (1,)), ((), ()))
TN = (((0,), (0,)), ((), ()))

MESH = pl.DeviceIdType.MESH
ANY = pl.BlockSpec(memory_space=pl.ANY)


def _nbytes(shape, dtype):
    return math.prod(d for d in shape if d is not None) * jnp.dtype(dtype).itemsize


def _params(block_bytes, n_axes):
    limit = min(VMEM_CAP, 2 * block_bytes + VMEM_SLACK)
    return pltpu.CompilerParams(dimension_semantics=("arbitrary",) * n_axes, vmem_limit_bytes=limit)


def _call(body, name, grid, in_arrays, in_specs, out_shapes, out_specs, scratch=(), aliases=None, extra_bytes=0,
          after=(), prefetch=None):
    total = extra_bytes
    for a, s in zip(in_arrays, in_specs):
        if s.block_shape is not None:
            total += _nbytes(s.block_shape, a.dtype)
    for o, s in zip(out_shapes, out_specs):
        if s.block_shape is not None:
            total += _nbytes(s.block_shape, o.dtype)
    n_in, n_after = len(in_arrays), len(after)
    lead = 0 if prefetch is None else 1

    def with_after(*refs):
        return body(*refs[:lead + n_in], *refs[lead + n_in + n_after:])

    operands = [pltpu.with_memory_space_constraint(a, pltpu.HBM) for a in (*in_arrays, *after)]
    specs = list(in_specs) + [ANY] * n_after
    common = dict(name=name, out_shape=[pltpu.HBM(o.shape, o.dtype) for o in out_shapes], input_output_aliases=aliases or {},
                  compiler_params=_params(total, len(grid)))
    if prefetch is None:
        return pl.pallas_call(with_after if n_after else body, grid=grid, in_specs=specs, out_specs=list(out_specs),
                              scratch_shapes=list(scratch), **common)(*operands)
    grid_spec = pltpu.PrefetchScalarGridSpec(num_scalar_prefetch=1, grid=grid, in_specs=specs, out_specs=list(out_specs),
                                             scratch_shapes=list(scratch))
    return pl.pallas_call(with_after if n_after else body, grid_spec=grid_spec, **common)(prefetch, *operands)


def _sig(x):
    return 1.0 / (1.0 + jnp.exp(-x))


def _dot(a, b, dims):
    return lax.dot_general(a.astype(BF16), b.astype(BF16), dims, preferred_element_type=F32)


def _rms_stats(x):
    r = lax.rsqrt(jnp.mean(x * x, axis=-1, keepdims=True) + EPS)
    return r, x * r


def _rms_bwd(dn, x, g):
    r, xh = _rms_stats(x)
    t = dn * g
    dx = r * (t - xh * jnp.mean(t * xh, axis=-1, keepdims=True))
    return dx, jnp.sum(dn * xh, axis=0, keepdims=True)


def _row_tile(m, want):
    t = min(m, want)
    assert m % t == 0
    return t


def _mm(name, a, b, *, grid, a_spec, b_spec, out, o_spec, dims, nk=None, k_axis=None, alpha=1.0,
        res=None, res_spec=None, rms=None, acc_shape=None, after=()):
    first_axes = len(grid)

    def body(*refs):
        refs = list(refs)
        a_ref, b_ref = refs[:2]
        pos = 2
        if res is not None:
            res_ref = refs[pos]
            pos += 1
        if rms is not None:
            h_ref, g_ref, dres_ref = refs[pos:pos + 3]
            pos += 3
        o_ref = refs[pos]
        pos += 1
        if rms is not None:
            dg_ref = refs[pos]
            pos += 1
        acc_ref = refs[pos] if k_axis is not None else None
        ids = [pl.program_id(ax) for ax in range(first_axes)]
        row_axes = [ids[ax] == 0 for ax in range(first_axes) if ax != k_axis]
        is_first = functools.reduce(jnp.logical_and, row_axes) if row_axes else None

        part = _dot(a_ref[...], b_ref[...], dims)

        def finish(acc):
            val = acc if alpha == 1.0 else acc * alpha
            if res is not None:
                val = val + res_ref[...]
            if rms is not None:
                dx, dg = _rms_bwd(val, h_ref[...], g_ref[...])
                o_ref[...] = (dres_ref[...] + dx).astype(o_ref.dtype)

                @pl.when(is_first)
                def _():
                    dg_ref[...] = dg

                @pl.when(jnp.logical_not(is_first))
                def _():
                    dg_ref[...] += dg
            else:
                o_ref[...] = val.astype(o_ref.dtype)

        if k_axis is None:
            finish(part)
        else:
            k = ids[k_axis]

            @pl.when(k == 0)
            def _():
                acc_ref[...] = part

            @pl.when(k > 0)
            def _():
                acc_ref[...] += part

            @pl.when(k == nk - 1)
            def _():
                finish(acc_ref[...])

    in_arrays, in_specs = [a, b], [a_spec, b_spec]
    if res is not None:
        in_arrays.append(res)
        in_specs.append(res_spec)
    outs, o_specs = [out], [o_spec]
    if rms is not None:
        h, g, dres, row_spec, g_spec = rms
        in_arrays += [h, g, dres]
        in_specs += [row_spec, g_spec, row_spec]
        outs.append(jax.ShapeDtypeStruct(g.shape, F32))
        o_specs.append(g_spec)
    scratch, extra = [], 0
    if k_axis is not None:
        scratch = [pltpu.VMEM(acc_shape, F32)]
        extra = _nbytes(acc_shape, F32)
    res_out = _call(body, name, grid, in_arrays, in_specs, outs, o_specs, scratch, extra_bytes=extra, after=after)
    return res_out if rms is not None else res_out[0]


def _sds(shape, dtype):
    return jax.ShapeDtypeStruct(shape, dtype)


def _rmsnorm_fwd(name, h, g, after=()):
    m, d = h.shape
    tm = _row_tile(m, 512)

    def body(h_ref, g_ref, o_ref):
        _, xh = _rms_stats(h_ref[...])
        o_ref[...] = (xh * g_ref[...]).astype(BF16)

    row = pl.BlockSpec((tm, d), lambda i: (i, 0))
    return _call(body, name, (m // tm,), [h, g], [row, pl.BlockSpec((1, d), lambda i: (0, 0))],
                 [_sds((m, d), BF16)], [row], after=after)[0]


def _swiglu_fwd(name, ab):
    _, m, w = ab.shape
    half = N_DEV // 2
    tm = _row_tile(m, 512)
    ab4 = ab.reshape(2, half, m, w)

    def body(ab_ref, o_ref):
        a, b = ab_ref[0], ab_ref[1]
        o_ref[...] = (a * _sig(a) * b).astype(BF16)

    return _call(body, name, (half, m // tm), [ab4],
                 [pl.BlockSpec((2, None, tm, w), lambda j, i: (0, j, i, 0))],
                 [_sds((half, m, w), BF16)], [pl.BlockSpec((None, tm, w), lambda j, i: (j, i, 0))])[0]


def _swiglu_bwd(name, ab, ds):
    _, m, w = ab.shape
    half = N_DEV // 2
    tm = _row_tile(m, 512)
    ab4 = ab.reshape(2, half, m, w)

    def body(ab_ref, ds_ref, o_ref):
        a, b, d = ab_ref[0], ab_ref[1], ds_ref[...]
        sg = _sig(a)
        o_ref[0] = (d * b * sg * (1.0 + a * (1.0 - sg))).astype(BF16)
        o_ref[1] = (d * a * sg).astype(BF16)

    blk = pl.BlockSpec((2, None, tm, w), lambda j, i: (0, j, i, 0))
    out = _call(body, name, (half, m // tm), [ab4, ds],
                [blk, pl.BlockSpec((None, tm, w), lambda j, i: (j, i, 0))],
                [_sds((2, half, m, w), BF16)], [blk])[0]
    return out.reshape(N_DEV, m, w)


def _inproj(name, u, w_in_p):
    m, d = u.shape
    tm = _row_tile(m, 256)

    def body(u_ref, w_ref, q_ref, k_ref, v_ref, f_ref, c_ref, g_ref):
        z = _dot(u_ref[...], w_ref[...], NN)
        q_ref[...] = z[:, P_Q:P_K].astype(BF16)
        k_ref[...] = z[:, P_K:P_V].astype(BF16)
        v_ref[...] = z[:, P_V:P_F].astype(BF16)
        f_ref[...] = z[:, P_F:P_C]
        c_ref[...] = z[:, P_C:P_G]
        g_ref[...] = z[:, P_G:IN_PAD]

    def rows(width):
        return pl.BlockSpec((tm, width), lambda i: (i, 0))

    widths = (ATTN_W, ATTN_W, ATTN_W, F_PAD, 2 * CONV_CH, 2 * D_MODEL)
    dtypes = (BF16, BF16, BF16, F32, F32, F32)
    return _call(body, name, (m // tm,), [u, w_in_p], [rows(d), pl.BlockSpec((d, IN_PAD), lambda i: (0, 0))],
                 [_sds((m, wd), dt) for wd, dt in zip(widths, dtypes)], [rows(wd) for wd in widths])


def _cumsum_rows(x, reverse):
    m = x.shape[0]
    row = lax.broadcasted_iota(jnp.int32, x.shape, 0)
    sh = 1
    while sh < m:
        if reverse:
            x = x + jnp.where(row < m - sh, pltpu.roll(x, m - sh, axis=0), 0.0)
        else:
            x = x + jnp.where(row >= sh, pltpu.roll(x, sh, axis=0), 0.0)
        sh *= 2
    return x


def _log_sigmoid(x):
    return jnp.minimum(x, 0.0) - jnp.log(1.0 + jnp.exp(-jnp.abs(x)))


def _fgate_fwd(name, zf, bf):
    m, w = zf.shape

    def body(z_ref, b_ref, c_ref):
        c_ref[...] = _cumsum_rows(_log_sigmoid(z_ref[...] + b_ref[...]), reverse=False)

    full = pl.BlockSpec((m, w), lambda i: (0, 0))
    return _call(body, name, (1,), [zf, bf], [full, pl.BlockSpec((1, w), lambda i: (0, 0))], [_sds((m, w), F32)], [full])[0]


def _fgate_bwd(name, dc, zf, bf):
    m, w = zf.shape

    def body(dc_ref, z_ref, b_ref, dz_ref, db_ref):
        dls = _cumsum_rows(dc_ref[...], reverse=True)
        dz = dls * _sig(-(z_ref[...] + b_ref[...]))
        lane = lax.broadcasted_iota(jnp.int32, dz.shape, 1)
        dz = jnp.where(lane < N_HEADS, dz, 0.0)
        dz_ref[...] = dz.astype(BF16)
        db_ref[...] = jnp.sum(dz, axis=0, keepdims=True)

    full = pl.BlockSpec((m, w), lambda i: (0, 0))
    one = pl.BlockSpec((1, w), lambda i: (0, 0))
    return _call(body, name, (1,), [dc, zf, bf], [full, full, one], [_sds((m, w), BF16), _sds((1, w), F32)], [full, one])


def _lane_pick(x, idx):
    lane = lax.broadcasted_iota(jnp.int32, x.shape, 1)
    return jnp.sum(jnp.where(lane == idx, x, 0.0), axis=1, keepdims=True)


def _row_pick(x, idx):
    sub = lax.broadcasted_iota(jnp.int32, x.shape, 0)
    return jnp.sum(jnp.where(sub == idx, x, 0.0), axis=0, keepdims=True)


def _attn_fwd(name, q, k, v, ct, t):
    m = q.shape[0]
    n_chunks = m // t
    pairs = N_HEADS // 2
    scale = 1.0 / math.sqrt(HEAD_DIM)

    def body(q_ref, k_ref, v_ref, ct_ref, o_ref, lse_ref):
        p = pl.program_id(0)
        i = pl.program_id(1)
        lane = lax.broadcasted_iota(jnp.int32, (t, LANES), 1)
        first = lane < HEAD_DIM
        q2 = q_ref[...] * scale
        zero = jnp.zeros_like(q2)
        q_heads = (jnp.where(first, q2, zero), jnp.where(first, zero, q2))
        below = lax.broadcasted_iota(jnp.int32, (t, t), 0) >= lax.broadcasted_iota(jnp.int32, (t, t), 1)

        def step(j, carry, diagonal):
            off = pl.multiple_of(j * t, t)
            kj = k_ref[pl.ds(off, t), :]
            vj = v_ref[pl.ds(off, t), :]
            ctj = ct_ref[j]
            new = []
            for e in range(2):
                mx, den, acc = carry[3 * e:3 * e + 3]
                s = _dot(q_heads[e], kj, NT) - _row_pick(ctj, 2 * p + e)
                if diagonal:
                    s = jnp.where(below, s, NEG_BIG)
                mx2 = jnp.maximum(mx, jnp.max(s, axis=1, keepdims=True))
                corr = jnp.exp(mx - mx2)
                pe = jnp.exp(s - mx2)
                new += [mx2, corr * den + jnp.sum(pe, axis=1, keepdims=True), corr * acc + _dot(pe, vj, NN)]
            return tuple(new)

        col = jnp.full((t, 1), NEG_BIG, F32), jnp.zeros((t, 1), F32), jnp.zeros((t, LANES), F32)
        fin = step(i, lax.fori_loop(0, i, lambda j, carry: step(j, carry, False), col + col), True)
        o_ref[...] = jnp.where(first, fin[2] / fin[1], fin[5] / fin[4]).astype(BF16)
        lse_a = fin[0] + jnp.log(fin[1])
        lse_b = fin[3] + jnp.log(fin[4])
        lse_ref[...] = jnp.where(lane == 0, lse_a, jnp.where(lane == 1, lse_b, 0.0))

    seq = pl.BlockSpec((m, LANES), lambda p, i: (0, p))
    blk = pl.BlockSpec((t, LANES), lambda p, i: (i, p))
    return _call(body, name, (pairs, n_chunks), [q, k, v, ct],
                 [blk, seq, seq, pl.BlockSpec((n_chunks, 8, t), lambda p, i: (0, 0, 0))],
                 [_sds((m, ATTN_W), BF16), _sds((pairs, m, LANES), F32)],
                 [blk, pl.BlockSpec((None, t, LANES), lambda p, i: (p, i, 0))])


def _attn_bwd(name, q, k, v, ct, o, lse, do, t):
    m = q.shape[0]
    n_chunks = m // t
    pairs = N_HEADS // 2
    scale = 1.0 / math.sqrt(HEAD_DIM)

    def body(q_ref, do_ref, o_ref, lse_ref, ct_ref, k_ref, v_ref, dq_ref, dk_ref, dv_ref, dcs_ref, drs_ref):
        p = pl.program_id(0)
        j = pl.program_id(1)

        @pl.when(j == 0)
        def _():
            dq_ref[...] = jnp.zeros_like(dq_ref)
            drs_ref[...] = jnp.zeros_like(drs_ref)

        lane = lax.broadcasted_iota(jnp.int32, (t, LANES), 1)
        first = lane < HEAD_DIM
        kj, vj = k_ref[...] * scale, v_ref[...]
        zero = jnp.zeros_like(kj)
        k_heads = (jnp.where(first, kj, zero), jnp.where(first, zero, kj))
        v_heads = (jnp.where(first, vj, zero), jnp.where(first, zero, vj))
        ctj = ct_ref[j]
        c_rows = (_row_pick(ctj, 2 * p), _row_pick(ctj, 2 * p + 1))
        below = lax.broadcasted_iota(jnp.int32, (t, t), 0) >= lax.broadcasted_iota(jnp.int32, (t, t), 1)

        def step(i, carry, diagonal):
            off = pl.multiple_of(i * t, t)
            qi = q_ref[pl.ds(off, t), :]
            doi = do_ref[pl.ds(off, t), :]
            prod = doi * o_ref[pl.ds(off, t), :].astype(F32)
            lsei = lse_ref[pl.ds(off, t), :]
            dob = doi.astype(BF16)
            new = []
            dq = jnp.zeros((t, LANES), F32)
            drow = jnp.zeros((t, LANES), F32)
            for e in range(2):
                dk, dv, dcol = carry[3 * e:3 * e + 3]
                delta = jnp.sum(jnp.where(first if e == 0 else jnp.logical_not(first), prod, 0.0), axis=1, keepdims=True)
                pe = jnp.exp(_dot(qi, k_heads[e], NT) - c_rows[e] - _lane_pick(lsei, e))
                if diagonal:
                    pe = jnp.where(below, pe, 0.0)
                dsc = pe * (_dot(dob, v_heads[e], NT) - delta)
                dsb = dsc.astype(BF16)
                dq = dq + _dot(dsb, k_heads[e], NN)
                drow = jnp.where(lane == e, jnp.sum(dsc, axis=1, keepdims=True), drow)
                new += [dk + _dot(dsb, qi, TN), dv + _dot(pe, dob, TN), dcol + jnp.sum(dsc, axis=0, keepdims=True)]
            dq_ref[pl.ds(off, t), :] += dq
            drs_ref[pl.ds(off, t), :] += drow
            return tuple(new)

        z = jnp.zeros((t, LANES), F32), jnp.zeros((t, LANES), F32), jnp.zeros((1, t), F32)
        fin = lax.fori_loop(j + 1, n_chunks, lambda i, carry: step(i, carry, False), step(j, z + z, True))
        dk_ref[...] = jnp.where(first, fin[0], fin[3]) * scale
        dv_ref[...] = jnp.where(first, fin[1], fin[4])
        sub = lax.broadcasted_iota(jnp.int32, (8, t), 0)
        dcs_ref[...] = jnp.where(sub == 0, fin[2], jnp.where(sub == 1, fin[5], 0.0))

    seq = pl.BlockSpec((m, LANES), lambda p, j: (0, p))
    blk = pl.BlockSpec((t, LANES), lambda p, j: (j, p))
    return _call(body, name, (pairs, n_chunks), [q, do, o, lse, ct, k, v],
                 [seq, seq, seq, pl.BlockSpec((None, m, LANES), lambda p, j: (p, 0, 0)),
                  pl.BlockSpec((n_chunks, 8, t), lambda p, j: (0, 0, 0)), blk, blk],
                 [_sds((m, ATTN_W), F32), _sds((m, ATTN_W), F32), _sds((m, ATTN_W), F32),
                  _sds((pairs, n_chunks, 8, t), F32), _sds((pairs, m, LANES), F32)],
                 [seq, blk, blk, pl.BlockSpec((None, None, 8, t), lambda p, j: (p, j, 0, 0)),
                  pl.BlockSpec((None, m, LANES), lambda p, j: (p, 0, 0))])


def _glu(z):
    return z[:, :CONV_CH] * _sig(z[:, CONV_CH:])


def _shifted(x, lead, tm):
    n = x.shape[0]
    return pltpu.roll(x, (n - lead) % n, axis=0)[:tm]


def _conv_fwd(name, zc, w, b, g, tm):
    m = zc.shape[0]
    hb = tm // CONV_HALO

    def body(cur_ref, prev_ref, w_ref, b_ref, g_ref, cv_ref, y_ref):
        i = pl.program_id(0)
        a_prev = jnp.where(i > 0, _glu(prev_ref[...]), 0.0)
        af = jnp.concatenate([a_prev, _glu(cur_ref[...])], axis=0)
        y = jnp.zeros((tm, CONV_CH), F32)
        for tap in range(CONV_K):
            y = y + w_ref[pl.ds(tap, 1), :] * _shifted(af, CONV_HALO - (CONV_K - 1) + tap, tm)
        y = y + b_ref[...]
        y_ref[...] = y
        _, xh = _rms_stats(y)
        rn = xh * g_ref[...]
        cv_ref[...] = (rn * _sig(rn)).astype(BF16)

    one = pl.BlockSpec((1, CONV_CH), lambda i: (0, 0))
    out = pl.BlockSpec((tm, CONV_CH), lambda i: (i, 0))
    return _call(body, name, (m // tm,), [zc, zc, w, b, g],
                 [pl.BlockSpec((tm, 2 * CONV_CH), lambda i: (i, 0)),
                  pl.BlockSpec((CONV_HALO, 2 * CONV_CH), lambda i: (jnp.maximum(i * hb - 1, 0), 0)),
                  pl.BlockSpec((CONV_HALO, CONV_CH), lambda i: (0, 0)), one, one],
                 [_sds((m, CONV_CH), BF16), _sds((m, CONV_CH), F32)], [out, out])


def _conv_bwd(name, dcv, y, zc, w, g, tm):
    m = zc.shape[0]
    hb = tm // CONV_HALO
    n_blocks = m // tm

    def body(dcv_ref, dcvn_ref, y_ref, yn_ref, cur_ref, prev_ref, w_ref, g_ref, dz_ref, dw_ref, db_ref, dg_ref):
        i = pl.program_id(0)
        gv = g_ref[...]

        def dy_of(d, yv):
            r, xh = _rms_stats(yv)
            rn = xh * gv
            sg = _sig(rn)
            drn = d * sg * (1.0 + rn * (1.0 - sg))
            tt = drn * gv
            return r * (tt - xh * jnp.mean(tt * xh, axis=-1, keepdims=True)), drn * xh

        dy, dgt = dy_of(dcv_ref[...], y_ref[...])
        dy_next, _ = dy_of(dcvn_ref[...], yn_ref[...])
        dyf = jnp.concatenate([dy, jnp.where(i < n_blocks - 1, dy_next, 0.0)], axis=0)
        cur = cur_ref[...]
        af = jnp.concatenate([jnp.where(i > 0, _glu(prev_ref[...]), 0.0), _glu(cur)], axis=0)

        @pl.when(i == 0)
        def _():
            dw_ref[...] = jnp.zeros_like(dw_ref)
            db_ref[...] = jnp.zeros_like(db_ref)
            dg_ref[...] = jnp.zeros_like(dg_ref)

        da = jnp.zeros((tm, CONV_CH), F32)
        for tap in range(CONV_K):
            da = da + w_ref[pl.ds(tap, 1), :] * _shifted(dyf, CONV_K - 1 - tap, tm)
            a_tap = _shifted(af, CONV_HALO - (CONV_K - 1) + tap, tm)
            dw_ref[pl.ds(tap, 1), :] += jnp.sum(dy * a_tap, axis=0, keepdims=True)
        db_ref[...] += jnp.sum(dy, axis=0, keepdims=True)
        dg_ref[...] += jnp.sum(dgt, axis=0, keepdims=True)
        c1, sg2 = cur[:, :CONV_CH], _sig(cur[:, CONV_CH:])
        dz_ref[:, :CONV_CH] = (da * sg2).astype(BF16)
        dz_ref[:, CONV_CH:] = (da * c1 * sg2 * (1.0 - sg2)).astype(BF16)

    one = pl.BlockSpec((1, CONV_CH), lambda i: (0, 0))
    taps = pl.BlockSpec((CONV_HALO, CONV_CH), lambda i: (0, 0))
    row = pl.BlockSpec((tm, CONV_CH), lambda i: (i, 0))
    nxt = pl.BlockSpec((CONV_HALO, CONV_CH), lambda i: (jnp.minimum((i + 1) * hb, m // CONV_HALO - 1), 0))
    row2 = pl.BlockSpec((tm, 2 * CONV_CH), lambda i: (i, 0))
    return _call(body, name, (n_blocks,), [dcv, dcv, y, y, zc, zc, w, g],
                 [row, nxt, row, nxt, row2,
                  pl.BlockSpec((CONV_HALO, 2 * CONV_CH), lambda i: (jnp.maximum(i * hb - 1, 0), 0)), taps, one],
                 [_sds((m, 2 * CONV_CH), BF16), _sds((CONV_HALO, CONV_CH), F32), _sds((1, CONV_CH), F32), _sds((1, CONV_CH), F32)],
                 [row2, taps, one, one])


def _mixout_fwd(name, o, cv, zg, h, w_ao, w_co, w_out, tm):
    m, d = h.shape

    def body(o_ref, cv_ref, zg_ref, h_ref, wa_ref, wc_ref, wo_ref, hn_ref, mg_ref, ya_ref, yc_ref):
        ya = _dot(o_ref[...], wa_ref[...], NN)
        yc = _dot(cv_ref[...], wc_ref[...], NN)
        zg_v = zg_ref[...]
        mg = (_sig(zg_v[:, :d]) * ya + _sig(zg_v[:, d:]) * yc).astype(BF16)
        ya_ref[...] = ya
        yc_ref[...] = yc
        mg_ref[...] = mg
        hn_ref[...] = h_ref[...] + _dot(mg, wo_ref[...], NN)

    def rows(wd):
        return pl.BlockSpec((tm, wd), lambda i: (i, 0))

    def whole(a):
        return pl.BlockSpec(a.shape, lambda i: (0, 0))

    return _call(body, name, (m // tm,), [o, cv, zg, h, w_ao, w_co, w_out],
                 [rows(ATTN_W), rows(CONV_CH), rows(2 * d), rows(d), whole(w_ao), whole(w_co), whole(w_out)],
                 [_sds((m, d), F32), _sds((m, d), BF16), _sds((m, d), F32), _sds((m, d), F32)],
                 [rows(d), rows(d), rows(d), rows(d)])


def _mixout_bwd(name, dh, zg, ya, yc, w_ao, w_co, w_out, tm, after=()):
    m, d = dh.shape

    def body(dh_ref, zg_ref, ya_ref, yc_ref, wa_ref, wc_ref, wo_ref, do_ref, dcv_ref, dzg_ref, dya_ref, dyc_ref):
        dm = _dot(dh_ref[...], wo_ref[...], NT)
        zg_v = zg_ref[...]
        sa, sc = _sig(zg_v[:, :d]), _sig(zg_v[:, d:])
        dya = (dm * sa).astype(BF16)
        dyc = (dm * sc).astype(BF16)
        dzg_ref[:, :d] = (dm * ya_ref[...] * sa * (1.0 - sa)).astype(BF16)
        dzg_ref[:, d:] = (dm * yc_ref[...] * sc * (1.0 - sc)).astype(BF16)
        dya_ref[...] = dya
        dyc_ref[...] = dyc
        do_ref[...] = _dot(dya, wa_ref[...], NT)
        dcv_ref[...] = _dot(dyc, wc_ref[...], NT)

    def rows(wd):
        return pl.BlockSpec((tm, wd), lambda i: (i, 0))

    def whole(a):
        return pl.BlockSpec(a.shape, lambda i: (0, 0))

    return _call(body, name, (m // tm,), [dh, zg, ya, yc, w_ao, w_co, w_out],
                 [rows(d), rows(2 * d), rows(d), rows(d), whole(w_ao), whole(w_co), whole(w_out)],
                 [_sds((m, ATTN_W), F32), _sds((m, CONV_CH), F32), _sds((m, 2 * d), BF16), _sds((m, d), BF16), _sds((m, d), BF16)],
                 [rows(ATTN_W), rows(CONV_CH), rows(2 * d), rows(d), rows(d)], after=after)


def _ple_fwd(name, h, p, g, w_gate, w_proj, tm):
    m, d = h.shape

    def body(h_ref, p_ref, g_ref, wg_ref, wp_ref, hn_ref, n_ref, gl_ref, pp_ref):
        hv = h_ref[...]
        _, xh = _rms_stats(hv)
        n = (xh * g_ref[...]).astype(BF16)
        gl = _dot(n, wg_ref[...], NN)
        pp = _dot(p_ref[...], wp_ref[...], NN)
        n_ref[...] = n
        gl_ref[...] = gl
        pp_ref[...] = pp
        hn_ref[...] = hv + _sig(gl) * pp

    def rows(wd):
        return pl.BlockSpec((tm, wd), lambda i: (i, 0))

    return _call(body, name, (m // tm,), [h, p, g, w_gate, w_proj],
                 [rows(d), rows(D_PLE), pl.BlockSpec((1, d), lambda i: (0, 0)),
                  pl.BlockSpec((d, d), lambda i: (0, 0)), pl.BlockSpec((D_PLE, d), lambda i: (0, 0))],
                 [_sds((m, d), F32), _sds((m, d), BF16), _sds((m, d), F32), _sds((m, d), F32)],
                 [rows(d)] * 4)


def _ple_bwd(name, dh, h, g, gl, pp, w_gate, tm, after=()):
    m, d = h.shape

    def body(dh_ref, h_ref, g_ref, gl_ref, pp_ref, wg_ref, dhn_ref, dgl_ref, dpp_ref, dg_ref):
        dhv = dh_ref[...]
        sg = _sig(gl_ref[...])
        dgl = (dhv * pp_ref[...] * sg * (1.0 - sg)).astype(BF16)
        dgl_ref[...] = dgl
        dpp_ref[...] = (dhv * sg).astype(BF16)
        dx, dg = _rms_bwd(_dot(dgl, wg_ref[...], NT), h_ref[...], g_ref[...])
        dhn_ref[...] = dhv + dx

        @pl.when(pl.program_id(0) == 0)
        def _():
            dg_ref[...] = dg

        @pl.when(pl.program_id(0) > 0)
        def _():
            dg_ref[...] += dg

    rows = pl.BlockSpec((tm, d), lambda i: (i, 0))
    one = pl.BlockSpec((1, d), lambda i: (0, 0))
    return _call(body, name, (m // tm,), [dh, h, g, gl, pp, w_gate],
                 [rows, rows, one, rows, rows, pl.BlockSpec((d, d), lambda i: (0, 0))],
                 [_sds((m, d), F32), _sds((m, d), BF16), _sds((m, d), BF16), _sds((1, d), F32)],
                 [rows, rows, rows, one], after=after)


def _loss_head(name, h, g, target, tm):
    m, d = h.shape

    def body(h_ref, g_ref, t_ref, loss_ref, dh_ref, dg_ref):
        hv, gv = h_ref[...], g_ref[...]
        r, xh = _rms_stats(hv)
        err = xh * gv - t_ref[...]
        dy = err * (1.0 / d)
        tt = dy * gv
        dh_ref[...] = r * (tt - xh * jnp.mean(tt * xh, axis=-1, keepdims=True))
        dg = jnp.sum(dy * xh, axis=0, keepdims=True)
        part = jnp.zeros((1, LANES), F32) + 0.5 * jnp.sum(jnp.sum(err * err, axis=1, keepdims=True), axis=0, keepdims=True) * (1.0 / d)

        @pl.when(pl.program_id(0) == 0)
        def _():
            dg_ref[...] = dg
            loss_ref[...] = part

        @pl.when(pl.program_id(0) > 0)
        def _():
            dg_ref[...] += dg
            loss_ref[...] += part

    rows = pl.BlockSpec((tm, d), lambda i: (i, 0))
    one = pl.BlockSpec((1, d), lambda i: (0, 0))
    return _call(body, name, (m // tm,), [h, g, target], [rows, one, rows],
                 [_sds((1, LANES), F32), _sds((m, d), F32), _sds((1, d), F32)],
                 [pl.BlockSpec((1, LANES), lambda i: (0, 0)), rows, one])


def _place():
    return lax.axis_index("x"), lax.axis_index("y"), lax.axis_index("c")


def _linear(px, py, pc):
    return 4 * px + 2 * py + pc


def _block_of(ref, idx, rows_per_block):
    if rows_per_block is None:
        return ref.at[idx]
    return ref.at[pl.ds(pl.multiple_of(idx * rows_per_block, 16), rows_per_block), :]


FLIPS = tuple((dx, dy, dc) for dx in (0, 1) for dy in (0, 1) for dc in (0, 1))[1:]
N_PEERS = len(FLIPS)
GATHER_FLIPS = ((0, 0, 1), (1, 0, 0), (0, 1, 0), (1, 1, 0))
HBM = pl.BlockSpec(memory_space=pltpu.HBM)
SEM = pl.BlockSpec(memory_space=pltpu.SEMAPHORE)
SIDE_EFFECT = pltpu.SideEffectType.DATAFLOW_SIDE_EFFECTING


def _peer(x, y, c, flip):
    return (1 - x if flip[0] else x, 1 - y if flip[1] else y, 1 - c if flip[2] else c)


def _split_copies(scatter, srcs, lands, bands, send_sems, recv_sems):
    x, y, c = _place()
    my_idx = _linear(x, y, c)
    flips = FLIPS if scatter else GATHER_FLIPS
    copies = []
    for t in range(len(srcs)):
        for k, flip in enumerate(flips):
            peer = _peer(x, y, c, flip)
            if scatter:
                src, dst = _block_of(srcs[t], _linear(*peer), bands[t]), lands[t].at[k]
            else:
                src, dst = srcs[t], _block_of(lands[t], my_idx, bands[t])
            sem = t * len(flips) + k
            copies.append(pltpu.make_async_remote_copy(
                src_ref=src, dst_ref=dst, send_sem=send_sems.at[sem], recv_sem=recv_sems.at[sem],
                device_id=peer, device_id_type=MESH))
    return copies


def _pass_to_sibling(name, lands, bands):
    n = len(lands)

    def body(*refs):
        zones = refs[n:2 * n]
        send_sems, recv_sems = refs[2 * n:]
        x, y, c = _place()
        chips = [(1 - x, y), (x, 1 - y), (1 - x, 1 - y)]

        def copy(t, j, core):
            blk = _block_of(zones[t], _linear(*chips[j], core), bands[t])
            return pltpu.make_async_remote_copy(
                src_ref=blk, dst_ref=blk, send_sem=send_sems.at[t * 3 + j], recv_sem=recv_sems.at[t * 3 + j],
                device_id=(x, y, 1 - c), device_id_type=MESH)

        sent = [copy(t, j, c) for t in range(n) for j in range(3)]
        for cp in sent:
            cp.start()
        for t in range(n):
            for j in range(3):
                copy(t, j, 1 - c).wait_recv()
        for cp in sent:
            cp.wait_send()

    return pl.pallas_call(
        body, name=name, in_specs=[ANY] * n, out_specs=[ANY] * n, out_shape=[_sds(a.shape, a.dtype) for a in lands],
        input_output_aliases={i: i for i in range(n)},
        scratch_shapes=[pltpu.SemaphoreType.DMA((3 * n,)), pltpu.SemaphoreType.DMA((3 * n,))],
    )(*lands)


def _split_start(name, scatter, srcs, lands, bands, follows):
    n = len(srcs)

    def body(*refs):
        send_sems, recv_sems = refs[2 * n + 1], refs[2 * n + 2]
        for cp in _split_copies(scatter, refs[:n], refs[n:2 * n], bands, send_sems, recv_sems):
            cp.start()
        token = refs[-1]
        token[...] = jnp.zeros_like(token)

    pinned = [pltpu.with_memory_space_constraint(a, pltpu.HBM) for a in (*srcs, *lands)]
    n_sems = n * len(FLIPS if scatter else GATHER_FLIPS)
    outs = pl.pallas_call(
        body, name=name,
        out_shape=(pltpu.SemaphoreType.DMA((n_sems,)), pltpu.SemaphoreType.DMA((n_sems,)),
                   *[pltpu.HBM(a.shape, a.dtype) for a in pinned], _sds((8, LANES), F32)),
        in_specs=[HBM] * (2 * n) + [ANY], out_specs=(SEM, SEM, *[HBM] * (2 * n), pl.BlockSpec(memory_space=pltpu.VMEM)),
        input_output_aliases={i: 2 + i for i in range(2 * n)},
        compiler_params=pltpu.CompilerParams(has_side_effects=SIDE_EFFECT),
    )(*pinned, follows)
    return outs[0], outs[1], outs[2:2 + n], outs[2 + n:2 + 2 * n], outs[-1]


def _split_wait(name, scatter, started, bands, follows):
    send_sems, recv_sems, srcs, lands, _ = started
    n = len(srcs)

    def body(*refs):
        for cp in _split_copies(scatter, refs[:n], refs[n:2 * n], bands, refs[2 * n], refs[2 * n + 1]):
            cp.wait_send()
            cp.wait_recv()

    outs = pl.pallas_call(
        body, name=name, out_shape=tuple(pltpu.HBM(a.shape, a.dtype) for a in (*srcs, *lands)),
        in_specs=[HBM] * (2 * n) + [SEM, SEM, ANY], out_specs=[HBM] * (2 * n),
        input_output_aliases={i: i for i in range(2 * n)},
        compiler_params=pltpu.CompilerParams(has_side_effects=SIDE_EFFECT),
    )(*srcs, *lands, send_sems, recv_sems, follows)
    return outs[:n], outs[n:]


def _own_block_filled(shard, band, my_idx):
    if band is None:
        zone = lax.empty((N_DEV,) + shard.shape, shard.dtype)
        return lax.dynamic_update_slice(zone, shard[None], (my_idx,) + (0,) * shard.ndim)
    zone = lax.empty((N_DEV * band,) + shard.shape[1:], shard.dtype)
    return lax.dynamic_update_slice(zone, shard, (my_idx * band,) + (0,) * (shard.ndim - 1))


def _exchange(name, grads, row_bands, gathered):
    n_items = len(grads)
    outs = []
    for gr, band, whole in zip(grads, row_bands, gathered):
        if whole:
            blk = gr.shape
        elif band is None:
            blk = gr.shape[1:]
        else:
            blk = (band,) + gr.shape[1:]
        outs.append(_sds((N_DEV,) + blk, gr.dtype))
    flips = [(dx, dy, dc) for dx in (0, 1) for dy in (0, 1) for dc in (0, 1)][1:]

    def body(*refs):
        srcs, dsts = refs[:n_items], refs[n_items:2 * n_items]
        send_sems, recv_sems, local_sems = refs[2 * n_items:]
        x, y, c = _place()
        my_idx = _linear(x, y, c)

        def src_of(t, idx):
            return srcs[t] if gathered[t] else _block_of(srcs[t], idx, row_bands[t])

        copies = []
        for t in range(n_items):
            local = pltpu.make_async_copy(src_of(t, my_idx), dsts[t].at[my_idx], local_sems.at[t])
            local.start()
            copies.append(local)
            for k, (dx, dy, dc) in enumerate(flips):
                peer = (1 - x if dx else x, 1 - y if dy else y, 1 - c if dc else c)
                cp = pltpu.make_async_remote_copy(
                    src_ref=src_of(t, _linear(*peer)), dst_ref=dsts[t].at[my_idx],
                    send_sem=send_sems.at[t, k], recv_sem=recv_sems.at[t, k], device_id=peer, device_id_type=MESH)
                cp.start()
                copies.append(cp)
        for cp in copies:
            cp.wait()

    return pl.pallas_call(
        body, name=name, in_specs=[ANY] * n_items, out_specs=[ANY] * n_items, out_shape=outs,
        scratch_shapes=[pltpu.SemaphoreType.DMA((n_items, 7)), pltpu.SemaphoreType.DMA((n_items, 7)),
                        pltpu.SemaphoreType.DMA((n_items,))],
    )(*grads)


def _adam_math(g, w, m, v):
    m2 = ADAM_B1 * m + (1.0 - ADAM_B1) * g
    v2 = ADAM_B2 * v + (1.0 - ADAM_B2) * (g * g)
    m_hat = m2 / (1.0 - ADAM_B1 ** ADAM_STEP)
    v_hat = v2 / (1.0 - ADAM_B2 ** ADAM_STEP)
    return -ADAM_LR * (m_hat / (jnp.sqrt(v_hat) + ADAM_EPS) + ADAM_WD * w), m2, v2


def _adamw(name, recv, grads, band, my_idx, w, m, v, layer, prev):
    n_layers, r, c = w.shape
    tr = r
    for cand in (256, 176, 128):
        if r > cand and r % cand == 0:
            tr = cand
            break
    steps = r // tr

    def body(idx_ref, recv_ref, own_ref, w_ref, m_ref, v_ref, *rest):
        g_ref, d_ref, m2_ref, v2_ref = rest[-4:]
        g = own_ref[...].astype(F32)
        for s in range(N_PEERS):
            g = g + recv_ref[s].astype(F32)
        delta, m2, v2 = _adam_math(g, w_ref[...], m_ref[...], v_ref[...])
        g_ref[...] = g
        d_ref[...] = delta
        m2_ref[...] = m2
        v2_ref[...] = v2

    blk = pl.BlockSpec((None, tr, c), lambda i, idx: (layer, i, 0))
    if band is None:
        own = pl.BlockSpec((None, tr, c), lambda i, idx: (idx[0], i, 0))
    else:
        own = pl.BlockSpec((tr, c), lambda i, idx: (idx[0] * steps + i, 0))
    ins = [recv, grads, w, m, v]
    specs = [pl.BlockSpec((N_PEERS, tr, c), lambda i, idx: (0, i, 0)), own, blk, blk, blk]
    aliases = {}
    if prev is not None:
        ins += list(prev)
        specs += [ANY] * 4
        aliases = {1 + len(ins) - 4 + n: n for n in range(4)}
    return _call(body, name, (steps,), ins, specs, [_sds(w.shape, F32)] * 4, [blk] * 4, aliases=aliases, prefetch=my_idx)


def _adamw_small(name, recv, w, m, v):
    r, c = w.shape

    def body(recv_ref, w_ref, m_ref, v_ref, g_ref, d_ref, m2_ref, v2_ref):
        g = recv_ref[0]
        for s in range(1, N_DEV):
            g = g + recv_ref[s]
        delta, m2, v2 = _adam_math(g, w_ref[...], m_ref[...], v_ref[...])
        g_ref[...] = g
        d_ref[...] = delta
        m2_ref[...] = m2
        v2_ref[...] = v2

    full = pl.BlockSpec((r, c), lambda i: (0, 0))
    return _call(body, name, (1,), [recv, w, m, v], [pl.BlockSpec((N_DEV, r, c), lambda i: (0, 0, 0)), full, full, full],
                 [_sds((r, c), F32)] * 4, [full] * 4)


def _ffn_fwd(tag, h, g, wi, wo, after=()):
    m, d = h.shape
    tm = _row_tile(m, 512)
    n = _rmsnorm_fwd(tag + "_norm", h, g, after)
    tall = _row_tile(m, 1024)
    ab = _mm(tag + "_in", n, wi, grid=(N_DEV, m // tall), dims=NT,
             a_spec=pl.BlockSpec((tall, d), lambda s, i: (i, 0)), b_spec=pl.BlockSpec((None, FF_SHARD, d), lambda s, i: (s, 0, 0)),
             out=_sds((N_DEV, m, FF_SHARD), F32), o_spec=pl.BlockSpec((None, tall, FF_SHARD), lambda s, i: (s, i, 0)))
    act = _swiglu_fwd(tag + "_act", ab)
    nk = N_DEV // 2
    row = pl.BlockSpec((tm, d), lambda i, k: (i, 0))
    h2 = _mm(tag + "_out", act, wo, grid=(m // tm, nk), dims=NN, nk=nk, k_axis=1, alpha=FFN_RES,
             a_spec=pl.BlockSpec((None, tm, FF_SHARD), lambda i, k: (k, i, 0)), b_spec=pl.BlockSpec((FF_SHARD, d), lambda i, k: (k, 0)),
             out=_sds((m, d), F32), o_spec=row, res=h, res_spec=row, acc_shape=(tm, d))
    return h2, (h, n, ab, act)


def _ffn_bwd(tag, dh, saved, g, wi, wo, after=()):
    h, n, ab, act = saved
    m, d = h.shape
    tm = _row_tile(m, 1024)
    nk = N_DEV // 2
    ds = _mm(tag + "_dact", dh, wo, grid=(nk, m // tm), dims=NT, alpha=FFN_RES, after=after,
             a_spec=pl.BlockSpec((tm, d), lambda j, i: (i, 0)), b_spec=pl.BlockSpec((FF_SHARD, d), lambda j, i: (j, 0)),
             out=_sds((nk, m, FF_SHARD), F32), o_spec=pl.BlockSpec((None, tm, FF_SHARD), lambda j, i: (j, i, 0)))
    dwo = _mm(tag + "_dwo", act, dh, grid=(nk, m // tm), dims=TN, nk=m // tm, k_axis=1, alpha=FFN_RES,
              a_spec=pl.BlockSpec((None, tm, FF_SHARD), lambda j, k: (j, k, 0)), b_spec=pl.BlockSpec((tm, d), lambda j, k: (k, 0)),
              out=_sds((D_FF, d), BF16), o_spec=pl.BlockSpec((FF_SHARD, d), lambda j, k: (j, 0)), acc_shape=(FF_SHARD, d))
    dab = _swiglu_bwd(tag + "_dab", ab, ds)
    row = pl.BlockSpec((tm, d), lambda i, k: (i, 0))
    one = pl.BlockSpec((1, d), lambda i, k: (0, 0))
    dh2, dg = _mm(tag + "_dn", dab, wi, grid=(m // tm, N_DEV), dims=NN, nk=N_DEV, k_axis=1,
                  a_spec=pl.BlockSpec((None, tm, FF_SHARD), lambda i, k: (k, i, 0)),
                  b_spec=pl.BlockSpec((None, FF_SHARD, d), lambda i, k: (k, 0, 0)),
                  out=_sds((m, d), F32), o_spec=row, rms=(h, g, dh, row, one), acc_shape=(tm, d))
    dwi = _mm(tag + "_dwi", dab, n, grid=(N_DEV, m // tm), dims=TN, nk=m // tm, k_axis=1,
              a_spec=pl.BlockSpec((None, tm, FF_SHARD), lambda s, k: (s, k, 0)), b_spec=pl.BlockSpec((tm, d), lambda s, k: (k, 0)),
              out=_sds((N_DEV, FF_SHARD, d), BF16), o_spec=pl.BlockSpec((None, FF_SHARD, d), lambda s, k: (s, 0, 0)),
              acc_shape=(FF_SHARD, d))
    return dh2, dg, dwi, dwo


def _wgrad(name, a, b, tk):
    m = a.shape[0]
    ka, kb = a.shape[1], b.shape[1]
    return _mm(name, a, b, grid=(m // tk,), dims=TN, nk=m // tk, k_axis=0,
               a_spec=pl.BlockSpec((tk, ka), lambda k: (k, 0)), b_spec=pl.BlockSpec((tk, kb), lambda k: (k, 0)),
               out=_sds((ka, kb), BF16), o_spec=pl.BlockSpec((ka, kb), lambda k: (0, 0)), acc_shape=(ka, kb))


def _mixer_fwd(tag, h, g, w_in_p, bf, conv_w, conv_b, g_conv, w_ao, w_co, w_out, after=()):
    m, d = h.shape
    t = _row_tile(m, 256)
    u = _rmsnorm_fwd(tag + "_norm", h, g, after)
    q, k, v, zf, zc, zg = _inproj(tag + "_inproj", u, w_in_p)
    c = _fgate_fwd(tag + "_fgate", zf, bf)
    ta = _row_tile(m, ATTN_TILE)
    ct = c[:, :N_HEADS].T.reshape(N_HEADS, m // ta, ta).transpose(1, 0, 2)
    o, lse = _attn_fwd(tag + "_attn", q, k, v, ct, ta)
    cv, y = _conv_fwd(tag + "_conv", zc, conv_w, conv_b, g_conv, t)
    h2, mg, ya, yc = _mixout_fwd(tag + "_mixout", o, cv, zg, h, w_ao, w_co, w_out, t)
    return h2, (h, u, q, k, v, zf, zc, zg, ct, o, lse, cv, y, mg, ya, yc)


def _mixer_bwd(tag, dh, saved, g, w_in_p, bf, conv_w, g_conv, w_ao, w_co, w_out, after=()):
    h, u, q, k, v, zf, zc, zg, ct, o, lse, cv, y, mg, ya, yc = saved
    m, d = h.shape
    t = _row_tile(m, 256)
    tk = _row_tile(m, 512)
    do, dcv, dzg, dya, dyc = _mixout_bwd(tag + "_dmixout", dh, zg, ya, yc, w_ao, w_co, w_out, t, after)
    d_wout = _wgrad(tag + "_dwout", mg, dh, tk)
    d_wao = _wgrad(tag + "_dwao", o, dya, tk)
    d_wco = _wgrad(tag + "_dwco", cv, dyc, tk)
    dq, dk, dv, dcs, drs = _attn_bwd(tag + "_dattn", q, k, v, ct, o, lse, do, _row_tile(m, ATTN_TILE))
    dc = drs[:, :, :2].transpose(1, 0, 2).reshape(m, N_HEADS) - dcs[:, :, :2, :].transpose(0, 2, 1, 3).reshape(N_HEADS, m).T
    dc = jnp.pad(dc, ((0, 0), (0, F_PAD - N_HEADS)))
    dzf, dbf = _fgate_bwd(tag + "_dfgate", dc, zf, bf)
    dzc, dconv_w, dconv_b, dg_conv = _conv_bwd(tag + "_dconv", dcv, y, zc, conv_w, g_conv, t)
    dz = jnp.concatenate([dq.astype(BF16), dk.astype(BF16), dv.astype(BF16), dzf, dzc, dzg], axis=1)
    tm = _row_tile(m, 256)
    row = pl.BlockSpec((tm, d), lambda i: (i, 0))
    one = pl.BlockSpec((1, d), lambda i: (0, 0))
    dh2, dg = _mm(tag + "_du", dz, w_in_p, grid=(m // tm,), dims=NT,
                  a_spec=pl.BlockSpec((tm, IN_PAD), lambda i: (i, 0)), b_spec=pl.BlockSpec((d, IN_PAD), lambda i: (0, 0)),
                  out=_sds((m, d), F32), o_spec=row, rms=(h, g, dh, row, one))
    rt = 256
    d_win_p = _mm(tag + "_dwin", u, dz, grid=(d // rt, m // tk), dims=TN, nk=m // tk, k_axis=1,
                  a_spec=pl.BlockSpec((tk, rt), lambda r, kk: (kk, r)), b_spec=pl.BlockSpec((tk, IN_PAD), lambda r, kk: (kk, 0)),
                  out=_sds((d, IN_PAD), BF16), o_spec=pl.BlockSpec((rt, IN_PAD), lambda r, kk: (r, 0)), acc_shape=(rt, IN_PAD))
    return dh2, dg, d_win_p, dbf, d_wao, dconv_w, dconv_b, dg_conv, d_wco, d_wout


def _col_shards(full, n_cols):
    r = full.shape[0]
    return full.reshape(r, N_DEV, n_cols).transpose(1, 0, 2)


def _from_col_shards(stacked):
    _, r, n = stacked.shape
    return stacked.transpose(1, 0, 2).reshape(r, N_DEV * n)


SMALL_ROWS = 24


def _pack_small(g_ff1, g_mix, g_ff2, g_ple, g_final, conv_b, g_conv, b_f, loss_row):
    n_layers = g_ff1.shape[0]
    bf_row = jnp.pad(b_f.reshape(1, n_layers * N_HEADS), ((0, 0), (0, D_MODEL - n_layers * N_HEADS)))
    parts = [g_ff1, g_mix, g_ff2, g_ple, g_final.reshape(1, D_MODEL), conv_b.reshape(-1, D_MODEL),
             g_conv.reshape(-1, D_MODEL), bf_row, loss_row]
    packed = jnp.concatenate(parts, axis=0)
    return jnp.pad(packed, ((0, SMALL_ROWS - packed.shape[0]), (0, 0)))


def _unpack_small(packed, n_layers):
    ln = n_layers
    cr = n_layers * CONV_CH // D_MODEL
    pos = 4 * ln + 1
    return dict(
        g_ff1=packed[0:ln], g_mix=packed[ln:2 * ln], g_ff2=packed[2 * ln:3 * ln], g_ple=packed[3 * ln:4 * ln],
        g_final=packed[4 * ln], conv_b=packed[pos:pos + cr].reshape(ln, CONV_CH),
        g_conv=packed[pos + cr:pos + 2 * cr].reshape(ln, CONV_CH),
        b_f=packed[pos + 2 * cr, :ln * N_HEADS].reshape(ln, N_HEADS), loss=packed[pos + 2 * cr + 1, 0])


BIG = ("w_ff1_in", "w_ff1_out", "w_in", "w_attn_out", "conv_w", "w_conv_out", "w_out", "w_ff2_in", "w_ff2_out",
       "w_ple_gate", "w_ple_proj")
ROW_BAND = dict(w_ff1_in=None, w_ff1_out=D_FF // N_DEV, w_in=None, w_attn_out=None, conv_w=None, w_conv_out=None,
                w_out=D_MODEL // N_DEV, w_ff2_in=None, w_ff2_out=D_FF // N_DEV, w_ple_gate=D_MODEL // N_DEV, w_ple_proj=None)
FF_IN = ("w_ff1_in", "w_ff2_in")
FIRST_PARTS = (("w_ff1_in", "w_ff1_out"), ("w_in", "w_attn_out", "conv_w", "w_conv_out", "w_out"),
               ("w_ff2_in", "w_ff2_out", "w_ple_gate", "w_ple_proj"))
LAST_PARTS = (("w_ple_gate", "w_ple_proj", "w_ff2_in", "w_ff2_out"),
              ("w_in", "w_attn_out", "conv_w", "w_conv_out", "w_out"), ("w_ff1_in", "w_ff1_out"))
SMALL = ("g_ff1", "g_mix", "g_ff2", "g_ple", "g_final", "conv_b", "g_conv", "b_f")
ORDER = ("g_ff1", "w_ff1_in", "w_ff1_out", "g_mix", "w_in", "b_f", "w_attn_out", "conv_w", "conv_b", "g_conv", "w_conv_out",
         "w_out", "g_ff2", "w_ff2_in", "w_ff2_out", "g_ple", "w_ple_gate", "w_ple_proj", "g_final")


def kernel(x, p, g_ff1, w_ff1_in, w_ff1_out, g_mix, w_in, b_f, w_attn_out, conv_w, conv_b, g_conv, w_conv_out, w_out, g_ff2, w_ff2_in, w_ff2_out, g_ple, w_ple_gate, w_ple_proj, g_final, loss_target, m_g_ff1, m_w_ff1_in, m_w_ff1_out, m_g_mix, m_w_in, m_b_f, m_w_attn_out, m_conv_w, m_conv_b, m_g_conv, m_w_conv_out, m_w_out, m_g_ff2, m_w_ff2_in, m_w_ff2_out, m_g_ple, m_w_ple_gate, m_w_ple_proj, m_g_final, v_g_ff1, v_w_ff1_in, v_w_ff1_out, v_g_mix, v_w_in, v_b_f, v_w_attn_out, v_conv_w, v_conv_b, v_g_conv, v_w_conv_out, v_w_out, v_g_ff2, v_w_ff2_in, v_w_ff2_out, v_g_ple, v_w_ple_gate, v_w_ple_proj, v_g_final):
    local = dict(locals())
    W = {n: local[n] for n in ORDER}
    M1 = {n: local["m_" + n] for n in ORDER}
    V1 = {n: local["v_" + n] for n in ORDER}
    for group in (W, M1, V1):
        for n in FF_IN:
            group[n] = group[n].transpose(0, 2, 1)
    n_layers = g_ff1.shape[0]
    m_rows = x.shape[1]
    t = _row_tile(m_rows, 256)
    my_idx = _linear(*_place()).astype(jnp.int32)
    idx_arr = my_idx.reshape(1)

    def gather_start(l, part, follows):
        shards = [W[n][l].astype(F32 if n == "conv_w" else BF16) for n in part]
        part_bands = [ROW_BAND[n] for n in part]
        zones = [_own_block_filled(s, band, my_idx) for s, band in zip(shards, part_bands)]
        return part, part_bands, _split_start(f"gather_start_{l}_{part[0]}", False, shards, zones, part_bands, follows)

    def gather_finish(l, started, follows):
        part, part_bands, flight = started
        lands = _split_wait(f"gather_wait_{l}_{part[0]}", False, flight, part_bands, follows)[1]
        return dict(zip(part, _pass_to_sibling("gather_pass", lands, part_bands)))

    def weights_of(got):
        fw = {}
        if "w_ff1_in" in got:
            fw.update(wi1=got["w_ff1_in"], wo1=got["w_ff1_out"])
        if "w_in" in got:
            w_in_full = _from_col_shards(got["w_in"])
            zeros = jnp.zeros((D_MODEL, F_PAD - N_HEADS), BF16)
            fw.update(
                w_in_p=jnp.concatenate([w_in_full[:, :P_F + N_HEADS], zeros, w_in_full[:, P_F + N_HEADS:]], axis=1),
                w_ao=_from_col_shards(got["w_attn_out"]), w_co=_from_col_shards(got["w_conv_out"]), w_out=got["w_out"],
                conv_w=jnp.pad(_from_col_shards(got["conv_w"]), ((0, CONV_HALO - CONV_K), (0, 0))))
        if "w_ff2_in" in got:
            fw.update(wi2=got["w_ff2_in"], wo2=got["w_ff2_out"], w_pg=got["w_ple_gate"], w_pp=_from_col_shards(got["w_ple_proj"]))
        return fw

    def small_of(l):
        return dict(
            bf=jnp.pad(b_f[l].reshape(1, N_HEADS), ((0, 0), (0, F_PAD - N_HEADS))),
            g1=g_ff1[l].reshape(1, -1), gm=g_mix[l].reshape(1, -1), g2=g_ff2[l].reshape(1, -1), gp=g_ple[l].reshape(1, -1),
            conv_b=conv_b[l].reshape(1, -1), g_conv=g_conv[l].reshape(1, -1))

    h = x[0]
    saved, full = [], []
    flights = [gather_start(0, part, h) for part in FIRST_PARTS]
    for l in range(n_layers):
        fw = small_of(l)
        got = gather_finish(l, flights[0], h)
        fw.update(weights_of(got))
        after, after_mix, coming = (), (), []
        if l + 1 < n_layers and len(flights) == 1:
            coming = [gather_start(l + 1, BIG, got[flights[0][0][0]])]
            after = (coming[0][2][4],)
        h, s1 = _ffn_fwd("ff1", h, fw["g1"], fw["wi1"], fw["wo1"], after)
        if len(flights) > 1:
            got = gather_finish(l, flights[1], h)
            fw.update(weights_of(got))
            if l + 1 < n_layers:
                coming = [gather_start(l + 1, BIG, got[flights[1][0][0]])]
                after_mix = (coming[0][2][4],)
        h, s2 = _mixer_fwd("mix", h, fw["gm"], fw["w_in_p"], fw["bf"], fw["conv_w"], fw["conv_b"], fw["g_conv"],
                           fw["w_ao"], fw["w_co"], fw["w_out"], after_mix)
        if len(flights) > 2:
            fw.update(weights_of(gather_finish(l, flights[2], h)))
        flights = coming
        full.append(fw)
        h, s3 = _ffn_fwd("ff2", h, fw["g2"], fw["wi2"], fw["wo2"])
        h_in = h
        pl_in = p[l, 0]
        h, n_ple, gl, pp = _ple_fwd("ple", h, pl_in, fw["gp"], fw["w_pg"], fw["w_pp"], t)
        saved.append((s1, s2, s3, (h_in, pl_in, n_ple, gl, pp)))

    loss_row, dh, dg_final = _loss_head("loss_head", h, g_final.reshape(1, -1), loss_target[0], t)

    small_grads = {n: [None] * n_layers for n in ("g_ff1", "g_mix", "g_ff2", "g_ple", "conv_b", "g_conv", "b_f")}
    stacked = {n: None for n in BIG}
    tk = _row_tile(m_rows, 512)

    def start_ready(l, grads, parts_left, follows):
        started, token = [], ()
        for part in [names for names in parts_left if all(n in grads for n in names)]:
            parts_left.remove(part)
            part_bands = [ROW_BAND[n] for n in part]
            terms = [grads[n] for n in part]
            zones = [lax.empty((N_PEERS,) + (g.shape[1:] if band is None else (band,) + g.shape[1:]), g.dtype)
                     for g, band in zip(terms, part_bands)]
            flight = _split_start(f"scatter_start_{l}_{part[0]}", True, terms, zones, part_bands, follows)
            started.append((l, part, part_bands, flight))
            token = (flight[4],)
        return started, token

    def apply(item, follows):
        l, part, part_bands, flight = item
        terms, recv = _split_wait(f"scatter_wait_{l}_{part[0]}", True, flight, part_bands, follows)
        for n, own, rc, band in zip(part, terms, recv, part_bands):
            stacked[n] = _adamw("adamw_" + n, rc, own, band, idx_arr, W[n], M1[n], V1[n], l, stacked[n])

    pending, after = [], ()
    for l in reversed(range(n_layers)):
        parts_left = list(LAST_PARTS if l == 0 else (BIG,))
        mine, grads = [], {}
        fw = full[l]
        s1, s2, s3, (h_in, pl_in, n_ple, gl, pp) = saved[l]
        dh, dgl, dpp, dgp = _ple_bwd("ple_bwd", dh, h_in, fw["gp"], gl, pp, fw["w_pg"], t, after)
        d_wpg = _wgrad("ple_dwgate", n_ple, dgl, tk)
        d_wpp = _wgrad("ple_dwproj", pl_in, dpp, tk)
        dh, dg2, d_wi2, d_wo2 = _ffn_bwd("ff2", dh, s3, fw["g2"], fw["wi2"], fw["wo2"])
        grads.update(w_ple_gate=d_wpg, w_ple_proj=_col_shards(d_wpp, D_MODEL // N_DEV), w_ff2_in=d_wi2, w_ff2_out=d_wo2)
        new, token = start_ready(l, grads, parts_left, dh)
        mine += new
        dh, dgm, d_win_p, dbf, d_wao, dconv_w, dconv_b, dg_conv, d_wco, d_wout = _mixer_bwd(
            "mix", dh, s2, fw["gm"], fw["w_in_p"], fw["bf"], fw["conv_w"], fw["g_conv"], fw["w_ao"], fw["w_co"], fw["w_out"],
            token)
        d_win = jnp.concatenate([d_win_p[:, :P_F + N_HEADS], d_win_p[:, P_C:]], axis=1)
        grads.update(w_in=_col_shards(d_win, IN_SHARD), w_attn_out=_col_shards(d_wao, D_MODEL // N_DEV),
                     conv_w=_col_shards(dconv_w[:CONV_K], CONV_CH // N_DEV), w_conv_out=_col_shards(d_wco, D_MODEL // N_DEV),
                     w_out=d_wout)
        new, token = start_ready(l, grads, parts_left, dh)
        mine += new
        dh, dg1, d_wi1, d_wo1 = _ffn_bwd("ff1", dh, s1, fw["g1"], fw["wi1"], fw["wo1"], token)
        grads.update(w_ff1_in=d_wi1, w_ff1_out=d_wo1)
        small_grads["g_ff1"][l], small_grads["g_mix"][l], small_grads["g_ff2"][l], small_grads["g_ple"][l] = dg1, dgm, dg2, dgp
        small_grads["conv_b"][l], small_grads["g_conv"][l] = dconv_b, dg_conv
        small_grads["b_f"][l] = dbf[:, :N_HEADS]
        for item in pending:
            apply(item, dh)
        new, after = start_ready(l, grads, parts_left, dh)
        pending = mine + new
    for item in pending:
        apply(item, dh)

    cat = {n: jnp.concatenate(small_grads[n], axis=0) for n in small_grads}
    g_pack = _pack_small(cat["g_ff1"], cat["g_mix"], cat["g_ff2"], cat["g_ple"], dg_final, cat["conv_b"], cat["g_conv"],
                         cat["b_f"], loss_row[:, :1] * jnp.ones((1, D_MODEL), F32))
    zero_row = jnp.zeros((1, D_MODEL), F32)
    packs = [_pack_small(*[src[n] for n in ("g_ff1", "g_mix", "g_ff2", "g_ple", "g_final", "conv_b", "g_conv", "b_f")], zero_row)
             for src in (W, M1, V1)]
    (recv_small,) = _exchange("gather_small", [g_pack], [None], [True])
    outs_small = [_unpack_small(a, n_layers) for a in _adamw_small("adamw_small", recv_small, *packs)]

    def pick(kind, n):
        if n in SMALL:
            return outs_small[kind][n]
        return stacked[n][kind].transpose(0, 2, 1) if n in FF_IN else stacked[n][kind]

    result = [outs_small[0]["loss"], dh.reshape(x.shape)]
    for kind in range(4):
        result += [pick(kind, n) for n in ORDER]
    return tuple(result)
```
